```python
import jax
import jax.numpy as jnp
from jax import lax
import numpy as np


D_MODEL = 1024
BATCH = 8
SEQ = 4096
DEPTH = 2

N_MIXERS = 4
GROUP_W = D_MODEL // N_MIXERS
N_GROUP_HEADS = 4
HEAD_DIM = GROUP_W // N_GROUP_HEADS
ROPE_BASE = 10000.0
RET_CHUNK = 64
RWKV_DECAY_LORA = D_MODEL // 16
RWKV_AAA_LORA = D_MODEL // 16
RWKV_GATE_LORA = D_MODEL // 8
RWKV_GN_EPS = 64e-5
CONV_W = 4
CONV_PAD_LEFT = CONV_W // 2
RGLRU_C = 8.0
GLA_KDIM = GROUP_W // 2
GLA_HEAD_K = GLA_KDIM // N_GROUP_HEADS
GLA_GATE_LORA = 16
GLA_TAU = 16.0
GLA_CHUNK = 64
N_EXPERTS = 16
EC_CAPACITY_FACTOR = 2
D_EXPERT = D_MODEL
NORM_EPS = 1e-6

SPLIT_SIZES = (
    GROUP_W, GROUP_W, GROUP_W, GROUP_W,
    GROUP_W, GROUP_W, GROUP_W,
    RWKV_DECAY_LORA, RWKV_AAA_LORA, RWKV_GATE_LORA,
    GROUP_W, GROUP_W,
    GLA_KDIM, GLA_KDIM, GROUP_W, GROUP_W, GLA_GATE_LORA,
)
IN_COLS = sum(SPLIT_SIZES)
SPLIT_POINTS = tuple(sum(SPLIT_SIZES[:i + 1]) for i in range(len(SPLIT_SIZES) - 1))

kernel_name = "hybrid_parallel_heads_ec_moe_encoder"


def _rmsnorm(x, w):
    xf = x.astype(jnp.float32)
    xf = xf * lax.rsqrt(jnp.mean(xf * xf, axis=-1, keepdims=True) + NORM_EPS)
    return (xf * w.astype(jnp.float32)).astype(x.dtype)


def _head_norm(y, w, center, eps=NORM_EPS):
    yf = y.astype(jnp.float32)
    if center:
        yf = yf - jnp.mean(yf, axis=-1, keepdims=True)
    yf = yf * lax.rsqrt(jnp.mean(yf * yf, axis=-1, keepdims=True) + eps)
    return yf * w.reshape(y.shape[-2], y.shape[-1])


def _heads(z, d):
    return z.reshape(z.shape[0], z.shape[1], -1, d)


def _flip(z):
    return jnp.flip(z, axis=1)


def _rotary(x, positions):
    d = x.shape[-1]
    inv = jnp.power(ROPE_BASE, -jnp.arange(0, d, 2, dtype=jnp.float32) / d)
    ang = positions.astype(jnp.float32)[..., None] * inv
    cos = jnp.cos(ang)[:, :, None, :]
    sin = jnp.sin(ang)[:, :, None, :]
    x1, x2 = x[..., : d // 2], x[..., d // 2:]
    return jnp.concatenate([x1 * cos - x2 * sin, x1 * sin + x2 * cos], axis=-1)


def _retention_chunked(q, k, v, log_gamma):
    B, T, H, d = q.shape
    C = RET_CHUNK
    n = T // C
    qc, kc, vc = (z.reshape(B, n, C, H, d) for z in (q, k, v))
    lg = log_gamma.astype(jnp.float32)
    idx = jnp.arange(C, dtype=jnp.float32)
    rel = idx[:, None] - idx[None, :]
    dmask = jnp.where(rel >= 0, jnp.exp(lg[:, None, None] * jnp.maximum(rel, 0.0)), 0.0)
    scores = jnp.einsum('bnihd,bnjhd->bnhij', qc, kc) * dmask
    y_intra = jnp.einsum('bnhij,bnjhe->bnihe', scores, vc)
    zeta = jnp.exp(lg[:, None] * (C - 1 - idx)[None, :])
    kv = jnp.einsum('bnjhd,hj,bnjhe->bnhde', kc, zeta, vc)
    chunk_decay = jnp.exp(lg * C)[:, None, None]

    def step(state, kv_i):
        return state * chunk_decay + kv_i, state

    _, prev = lax.scan(step, jnp.zeros((B, H, d, d), kv.dtype), jnp.moveaxis(kv, 1, 0))
    prev = jnp.moveaxis(prev, 0, 1)
    xi = jnp.exp(lg[:, None] * (idx + 1.0)[None, :])
    y_cross = jnp.einsum('bnihd,bnhde,hi->bnihe', qc, prev, xi)
    return (y_intra + y_cross).reshape(B, T, H, d)


def _retention_group(q, k, v, g, positions, log_decay, gn_w):
    B, T, _ = q.shape
    qh = _rotary(_heads(q, HEAD_DIM), positions)
    kh = _rotary(_heads(k, HEAD_DIM), positions) * (HEAD_DIM ** -0.5)
    vh = _heads(v, HEAD_DIM)
    y = _retention_chunked(qh, kh, vh, log_decay[0]) + _flip(
        _retention_chunked(_flip(qh), _flip(kh), _flip(vh), log_decay[1]))
    return _head_norm(y, gn_w, center=True).reshape(B, T, GROUP_W) * jax.nn.silu(g)


def _token_shift(z, mu):
    prev = jnp.pad(z, ((0, 0), (1, 0), (0, 0)))[:, :-1]
    return z + mu * (prev - z)


def _rwkv7_direction(r, k, v, xw, xa, mu_rkv, mu_w, mu_a, w0, w_up, a0, a_up, k_k, k_a, r_k):
    B, T, _ = r.shape
    r = _token_shift(r, mu_rkv[0])
    k = _token_shift(k, mu_rkv[1])
    v = _token_shift(v, mu_rkv[2])
    xw = _token_shift(xw, mu_w)
    xa = _token_shift(xa, mu_a)
    w_log = -jax.nn.softplus(-(w0 + jnp.tanh(xw) @ w_up)) - 0.5
    decay = jnp.exp(-jnp.exp(w_log))
    a = jax.nn.sigmoid(a0 + xa @ a_up)
    kk = _heads(k * k_k, HEAD_DIM)
    kk = kk / jnp.maximum(jnp.sqrt(jnp.sum(kk * kk, axis=-1, keepdims=True)), 1e-12)
    k = k * (1.0 + (a - 1.0) * k_a)
    r_h, k_h, v_h, w_h, a_h = (_heads(z, HEAD_DIM) for z in (r, k, v, decay, a))

    def step(S, inp):
        r_t, w_t, k_t, v_t, kk_t, a_t = inp
        sa = jnp.einsum('bhvk,bhk->bhv', S, -kk_t)
        S = (S * w_t[:, :, None, :] + sa[..., None] * (kk_t * a_t)[:, :, None, :]
             + v_t[..., None] * k_t[:, :, None, :])
        return S, jnp.einsum('bhvk,bhk->bhv', S, r_t)

    xs = tuple(jnp.moveaxis(z, 1, 0) for z in (r_h, w_h, k_h, v_h, kk, a_h))
    _, y = lax.scan(step, jnp.zeros((B, N_GROUP_HEADS, HEAD_DIM, HEAD_DIM), r.dtype), xs)
    y = jnp.moveaxis(y, 0, 1)
    bonus = jnp.sum(r_h * k_h * r_k, axis=-1, keepdims=True) * v_h
    return y, bonus


def _rwkv7_group(r, k, v, xw, xa, xg, mu_rkv, mu_w, mu_a, w0, w_up, a0, a_up, g_up,
                 k_k, k_a, r_k, gn_w):
    B, T, _ = r.shape
    y_f, b_f = _rwkv7_direction(r, k, v, xw, xa, mu_rkv, mu_w, mu_a,
                                w0[0], w_up[0], a0[0], a_up[0], k_k, k_a, r_k)
    y_b, b_b = _rwkv7_direction(_flip(r), _flip(k), _flip(v), _flip(xw), _flip(xa), mu_rkv, mu_w, mu_a,
                                w0[1], w_up[1], a0[1], a_up[1], k_k, k_a, r_k)
    y = y_f + _flip(y_b)
    bonus = b_f + _flip(b_b)
    o = (_head_norm(y, gn_w, center=True, eps=RWKV_GN_EPS) + bonus).reshape(B, T, GROUP_W)
    return o * (jax.nn.sigmoid(xg) @ g_up)


def _conv_centred(x, w, b):
    T = x.shape[1]
    xp = jnp.pad(x, ((0, 0), (CONV_PAD_LEFT, CONV_W - 1 - CONV_PAD_LEFT), (0, 0)))
    return sum(xp[:, j:j + T] * w[j] for j in range(CONV_W)) + b


def _lin_comb(left, right):
    a1, b1 = left
    a2, b2 = right
    return a1 * a2, a2 * b1 + b2


def _rglru_group(xr, gate, conv_w, conv_b, gate_w, gate_b, lam):
    B, T, _ = xr.shape
    xc = _conv_centred(xr, conv_w, conv_b)
    xh = _heads(xc, HEAD_DIM)

    def direction(dirn, reverse):
        gx = jnp.einsum('bthd,ghde->gbthe', xh, gate_w[dirn]).reshape(2, B, T, GROUP_W)
        gx = gx + gate_b[dirn][:, None, None, :]
        rec_gate = jax.nn.sigmoid(gx[0])
        in_gate = jax.nn.sigmoid(gx[1])
        log_a = -RGLRU_C * rec_gate * jax.nn.softplus(-lam[dirn])
        a = jnp.exp(log_a)
        bx = jnp.sqrt(-jnp.expm1(2.0 * log_a)) * (in_gate * xc)
        _, h = lax.associative_scan(_lin_comb, (a, bx), axis=1, reverse=reverse)
        return h

    h = direction(0, False) + direction(1, True)
    return h * jax.nn.gelu(gate)


def _gla_chunked(q, k, v, log_alpha):
    B, T, H, dk = q.shape
    dv = v.shape[-1]
    C = GLA_CHUNK
    n = T // C
    q, k, log_alpha = (z.reshape(B, n, C, H, dk) for z in (q, k, log_alpha))
    v = v.reshape(B, n, C, H, dv)
    bcum = jnp.cumsum(log_alpha, axis=2)
    b_last = bcum[:, :, -1:]
    q_dec = q * jnp.exp(bcum)
    k_dec = k * jnp.exp(-bcum)
    lower = jnp.tril(jnp.ones((C, C), dtype=bool))
    scores = jnp.where(lower, jnp.einsum('bnihk,bnjhk->bnhij', q_dec, k_dec), 0.0)
    y_intra = jnp.einsum('bnhij,bnjhv->bnihv', scores, v)
    kv = jnp.einsum('bnjhk,bnjhv->bnhkv', k * jnp.exp(b_last - bcum), v)
    chunk_decay = jnp.exp(b_last[:, :, 0])

    def step(S, inp):
        dec, kv_i = inp
        return S * dec[..., None] + kv_i, S

    _, s_prev = lax.scan(step, jnp.zeros((B, H, dk, dv), kv.dtype),
                         (jnp.moveaxis(chunk_decay, 1, 0), jnp.moveaxis(kv, 1, 0)))
    y_cross = jnp.einsum('bnihk,bnhkv->bnihv', q_dec, jnp.moveaxis(s_prev, 0, 1))
    return (y_intra + y_cross).reshape(B, T, H, dv)


def _gla_group(q, k, v, og, xa, alpha_up, alpha_b, gn_w):
    B, T, _ = q.shape
    qh = _heads(q, GLA_HEAD_K) * (GLA_HEAD_K ** -0.5)
    kh = _heads(k, GLA_HEAD_K)
    vh = _heads(v, HEAD_DIM)
    la_f = _heads(jax.nn.log_sigmoid(xa @ alpha_up[0] + alpha_b[0]) / GLA_TAU, GLA_HEAD_K)
    la_b = _heads(jax.nn.log_sigmoid(xa @ alpha_up[1] + alpha_b[1]) / GLA_TAU, GLA_HEAD_K)
    y = _gla_chunked(qh, kh, vh, la_f) + _flip(
        _gla_chunked(_flip(qh), _flip(kh), _flip(vh), _flip(la_b)))
    return _head_norm(y, gn_w, center=False).reshape(B, T, GROUP_W) * jax.nn.silu(og)


def _expert_choice_ffn(x, router_w, router_b, w_gate, w_up, w_down):
    B, T, D = x.shape
    cap = EC_CAPACITY_FACTOR * T // N_EXPERTS
    logits = jnp.einsum('btd,de->bte', x, router_w) + router_b
    aff = jax.nn.softmax(logits.astype(jnp.float32), axis=-1)
    gates, idx = lax.top_k(jnp.swapaxes(aff, 1, 2), cap)
    bidx = jnp.arange(B)[:, None, None]
    xs = x[bidx, idx]
    hid = jax.nn.silu(jnp.einsum('becd,edf->becf', xs, w_gate)) * jnp.einsum('becd,edf->becf', xs, w_up)
    out = jnp.einsum('becf,efd->becd', hid, w_down) * gates[..., None].astype(x.dtype)
    return jnp.zeros_like(x).at[bidx, idx].add(out.astype(x.dtype))


def setup_inputs(seed: int = 0) -> dict:
    key = jax.random.key(seed)
    ks = iter(jax.random.split(key, 40))
    f32 = jnp.float32
    L, D, W, H, hd = DEPTH, D_MODEL, GROUP_W, N_GROUP_HEADS, HEAD_DIM

    def normal(shape, scale):
        return scale * jax.random.normal(next(ks), shape, f32)

    def gain(shape):
        return 1.0 + normal(shape, 0.02)

    def uniform(shape, lo=0.0, hi=1.0):
        return jax.random.uniform(next(ks), shape, f32, lo, hi)

    x = normal((BATCH, SEQ, D), 1.0)
    positions = jnp.broadcast_to(jnp.arange(SEQ, dtype=jnp.int32), (BATCH, SEQ))
    norm_mix = gain((L, D))
    w_in = normal((L, D, IN_COLS), D ** -0.5)
    w_out = normal((L, D, D), D ** -0.5)
    base = jnp.log1p(-jnp.exp2(-5.0 - jnp.arange(H, dtype=f32)))
    ret_log_decay = base * (1.0 + normal((L, 2, H), 0.05))
    ret_gn = gain((L, W))
    rwkv_mu_rkv = uniform((L, 3, W))
    rwkv_mu_w = uniform((L, RWKV_DECAY_LORA))
    rwkv_mu_a = uniform((L, RWKV_AAA_LORA))
    ramp = jnp.linspace(0.0, 1.0, W, dtype=f32)
    rwkv_w0 = -6.0 + 5.0 * ramp ** 1.5 + normal((L, 2, W), 0.1)
    rwkv_w_up = normal((L, 2, RWKV_DECAY_LORA, W), 0.5 * RWKV_DECAY_LORA ** -0.5)
    rwkv_a0 = normal((L, 2, W), 0.1)
    rwkv_a_up = normal((L, 2, RWKV_AAA_LORA, W), 0.5 * RWKV_AAA_LORA ** -0.5)
    rwkv_g_up = normal((L, RWKV_GATE_LORA, W), RWKV_GATE_LORA ** -0.5)
    rwkv_k_k = 0.85 + normal((L, W), 0.02)
    rwkv_k_a = 1.0 + normal((L, W), 0.02)
    rwkv_r_k = normal((L, H, hd), 0.1)
    rwkv_gn = gain((L, W))
    lru_conv_w = normal((L, CONV_W, W), CONV_W ** -0.5)
    lru_conv_b = normal((L, W), 0.01)
    lru_gate_w = normal((L, 2, 2, H, hd, hd), hd ** -0.5)
    lru_gate_b = normal((L, 2, 2, W), 0.01)
    a_init = uniform((L, 2, W), 0.9, 0.999) ** (1.0 / RGLRU_C)
    lru_lambda = jnp.log(a_init) - jnp.log1p(-a_init)
    gla_alpha_up = normal((L, 2, GLA_GATE_LORA, GLA_KDIM), GLA_GATE_LORA ** -0.5)
    gla_alpha_b = 1.0 + 2.0 * uniform((L, 2, GLA_KDIM))
    gla_gn = gain((L, W))
    norm_ffn = gain((L, D))
    router_w = normal((L, D, N_EXPERTS), D ** -0.5)
    router_b = normal((L, N_EXPERTS), 0.01)
    exp_w_gate = normal((L, N_EXPERTS, D, D_EXPERT), D ** -0.5)
    exp_w_up = normal((L, N_EXPERTS, D, D_EXPERT), D ** -0.5)
    exp_w_down = normal((L, N_EXPERTS, D_EXPERT, D), D_EXPERT ** -0.5)
    norm_final = gain((D,))
    return {
        "x": x, "positions": positions, "norm_mix": norm_mix, "w_in": w_in, "w_out": w_out,
        "ret_log_decay": ret_log_decay, "ret_gn": ret_gn,
        "rwkv_mu_rkv": rwkv_mu_rkv, "rwkv_mu_w": rwkv_mu_w, "rwkv_mu_a": rwkv_mu_a,
        "rwkv_w0": rwkv_w0, "rwkv_w_up": rwkv_w_up, "rwkv_a0": rwkv_a0, "rwkv_a_up": rwkv_a_up,
        "rwkv_g_up": rwkv_g_up, "rwkv_k_k": rwkv_k_k, "rwkv_k_a": rwkv_k_a, "rwkv_r_k": rwkv_r_k,
        "rwkv_gn": rwkv_gn, "lru_conv_w": lru_conv_w, "lru_conv_b": lru_conv_b,
        "lru_gate_w": lru_gate_w, "lru_gate_b": lru_gate_b, "lru_lambda": lru_lambda,
        "gla_alpha_up": gla_alpha_up, "gla_alpha_b": gla_alpha_b, "gla_gn": gla_gn,
        "norm_ffn": norm_ffn, "router_w": router_w, "router_b": router_b,
        "exp_w_gate": exp_w_gate, "exp_w_up": exp_w_up, "exp_w_down": exp_w_down,
        "norm_final": norm_final,
    }


def reference(x, positions, norm_mix, w_in, w_out, ret_log_decay, ret_gn,
              rwkv_mu_rkv, rwkv_mu_w, rwkv_mu_a, rwkv_w0, rwkv_w_up, rwkv_a0, rwkv_a_up,
              rwkv_g_up, rwkv_k_k, rwkv_k_a, rwkv_r_k, rwkv_gn,
              lru_conv_w, lru_conv_b, lru_gate_w, lru_gate_b, lru_lambda,
              gla_alpha_up, gla_alpha_b, gla_gn,
              norm_ffn, router_w, router_b, exp_w_gate, exp_w_up, exp_w_down, norm_final):
    for l in range(DEPTH):
        h = _rmsnorm(x, norm_mix[l])
        proj = jnp.einsum('btd,dc->btc', h, w_in[l]).astype(jnp.float32)
        (rq, rk, rv, rg, wr, wk, wv, wxw, wxa, wxg, lx, lgate,
         gq, gk, gv, gog, gxa) = jnp.split(proj, SPLIT_POINTS, axis=-1)
        o_ret = _retention_group(rq, rk, rv, rg, positions, ret_log_decay[l], ret_gn[l])
        o_rwkv = _rwkv7_group(wr, wk, wv, wxw, wxa, wxg, rwkv_mu_rkv[l], rwkv_mu_w[l], rwkv_mu_a[l],
                              rwkv_w0[l], rwkv_w_up[l], rwkv_a0[l], rwkv_a_up[l], rwkv_g_up[l],
                              rwkv_k_k[l], rwkv_k_a[l], rwkv_r_k[l], rwkv_gn[l])
        o_lru = _rglru_group(lx, lgate, lru_conv_w[l], lru_conv_b[l], lru_gate_w[l],
                             lru_gate_b[l], lru_lambda[l])
        o_gla = _gla_group(gq, gk, gv, gog, gxa, gla_alpha_up[l], gla_alpha_b[l], gla_gn[l])
        mixed = jnp.concatenate([o_ret, o_rwkv, o_lru, o_gla], axis=-1).astype(x.dtype)
        x = x + jnp.einsum('btc,cd->btd', mixed, w_out[l])
        x = x + _expert_choice_ffn(_rmsnorm(x, norm_ffn[l]), router_w[l], router_b[l],
                                   exp_w_gate[l], exp_w_up[l], exp_w_down[l])
    return _rmsnorm(x, norm_final)
```

```python
import functools

import jax
import jax.numpy as jnp
from jax import lax
from jax.experimental import pallas as pl
from jax.experimental.pallas import tpu as pltpu

F32 = jnp.float32
BF16 = jnp.bfloat16

D_MODEL = 1024
GROUP_W = 256
N_HEADS = 4
HEAD_DIM = 64
ROPE_BASE = 10000.0
RWKV_GN_EPS = 64e-5
RGLRU_C = 8.0
GLA_KDIM = 128
GLA_HEAD_K = 32
GLA_TAU = 16.0
N_EXPERTS = 16
EC_CAPACITY_FACTOR = 2
NORM_EPS = 1e-6

LANE = 128
SUBLANE = 8
CHUNK = 64
VMEM_LIMIT = 56 * 1024 * 1024

COL_RET, COL_RWKV, COL_GLA, COL_LRU = 0, 1024, 2048, 3072
IN_COLS_PAD = 3584


def _tiles(T):
    return dict(
        tm_proj=min(1024, T), tn_proj=512,
        c_ret=min(256, T), tb_gla=min(512, T), tb_lru=min(512, T), tb_rwkv=min(256, T),
        tm_out=min(512, T), tm_comb=min(1024, T), tk_gather=min(1024, T),
    )


def _params(*sem):
    return pltpu.CompilerParams(dimension_semantics=sem, vmem_limit_bytes=VMEM_LIMIT)


def _bf(x):
    return x.astype(BF16)


def _dot(a, b):
    return jnp.dot(_bf(a), _bf(b), preferred_element_type=F32)


def _dot_nt(a, b):
    return lax.dot_general(_bf(a), _bf(b), (((1,), (1,)), ((), ())), preferred_element_type=F32)


def _dot_tn(a, b):
    return lax.dot_general(_bf(a), _bf(b), (((0,), (0,)), ((), ())), preferred_element_type=F32)


def _split2(a):
    hi = _bf(a)
    lo = _bf(a - hi.astype(F32))
    return hi, lo


def _split3(a):
    hi = _bf(a)
    r = a - hi.astype(F32)
    mid = _bf(r)
    lo = _bf(r - mid.astype(F32))
    return hi, mid, lo


def _dot_exact_lhs(m, a):
    hi, mid, lo = _split3(a)
    return (jnp.dot(m, hi, preferred_element_type=F32) + jnp.dot(m, mid, preferred_element_type=F32)
            + jnp.dot(m, lo, preferred_element_type=F32))


def _dot_exact_rhs(a, m):
    hi, lo = _split2(a)
    return jnp.dot(hi, m, preferred_element_type=F32) + jnp.dot(lo, m, preferred_element_type=F32)


def _dot_hp(a, b):
    ah, al = _split2(a)
    bh, bl = _split2(b)
    return (jnp.dot(ah, bh, preferred_element_type=F32) + jnp.dot(ah, bl, preferred_element_type=F32)
            + jnp.dot(al, bh, preferred_element_type=F32))


def _dot_hp_nt(a, b):
    ah, al = _split2(a)
    bh, bl = _split2(b)
    dn = (((1,), (1,)), ((), ()))
    return (lax.dot_general(ah, bh, dn, preferred_element_type=F32)
            + lax.dot_general(ah, bl, dn, preferred_element_type=F32)
            + lax.dot_general(al, bh, dn, preferred_element_type=F32))


def _sigmoid(x):
    return 1.0 / (1.0 + jnp.exp(-x))


def _softplus(x):
    return jnp.maximum(x, 0.0) + jnp.log(1.0 + jnp.exp(-jnp.abs(x)))


def _iota(shape, dim):
    return lax.broadcasted_iota(jnp.int32, shape, dim)


def _head_mask(h, width, hd):
    return (_iota((1, width), 1) // hd == h).astype(F32)


def _stack_heads(z, hd):
    w = z.shape[1]
    return jnp.concatenate([z * _head_mask(h, w, hd) for h in range(N_HEADS)], axis=0)


def _unstack_sum(z, c):
    return z[0:c] + z[c:2 * c] + z[2 * c:3 * c] + z[3 * c:4 * c]


def _tri_mask(n, lower, strict):
    i, j = _iota((n, n), 0), _iota((n, n), 1)
    if lower:
        return (j < i) if strict else (j <= i)
    return (j > i) if strict else (j >= i)


def _block_tri_mask(c, lower, strict):
    n = N_HEADS * c
    i, j = _iota((n, n), 0), _iota((n, n), 1)
    same = (i // c) == (j // c)
    ii, jj = i % c, j % c
    if lower:
        t = (jj < ii) if strict else (jj <= ii)
    else:
        t = (jj > ii) if strict else (jj >= ii)
    return same & t


def _head_stats(y, avg):
    return _dot_exact_rhs(y, avg)


def _shift_rows(z, edge, fwd):
    n = z.shape[0]
    row = _iota((n, 1), 0)
    if fwd:
        return jnp.where(row == 0, edge, pltpu.roll(z, 1, 0))
    return jnp.where(row == n - 1, edge, pltpu.roll(z, n - 1, 0))


def _in_proj_kernel(x_ref, g_ref, w_ref, o_ref, xn_ref):
    @pl.when(pl.program_id(1) == 0)
    def _():
        x = x_ref[...]
        ms = jnp.mean(x * x, axis=-1, keepdims=True)
        xn_ref[...] = _bf(x * lax.rsqrt(ms + NORM_EPS) * g_ref[...])

    o_ref[...] = jnp.dot(xn_ref[...], w_ref[...], preferred_element_type=F32)


def _in_proj(x2, gain, w_bf, tl):
    n, d = x2.shape
    tm, tn = tl["tm_proj"], tl["tn_proj"]
    nc = w_bf.shape[1]
    return pl.pallas_call(
        _in_proj_kernel,
        grid=(n // tm, nc // tn),
        in_specs=[pl.BlockSpec((tm, d), lambda i, j: (i, 0)),
                  pl.BlockSpec((1, d), lambda i, j: (0, 0)),
                  pl.BlockSpec((d, tn), lambda i, j: (0, j))],
        out_specs=pl.BlockSpec((tm, tn), lambda i, j: (i, j)),
        out_shape=jax.ShapeDtypeStruct((n, nc), F32),
        scratch_shapes=[pltpu.VMEM((tm, d), BF16)],
        compiler_params=_params("parallel", "arbitrary"),
        name="in_proj",
    )(x2, gain, w_bf)


def _rope_kernel(pos_ref, inv_ref, sgn_ref, cos_ref, sin_ref):
    ang = pos_ref[...].astype(F32) * inv_ref[...]
    cos_ref[...] = jnp.cos(ang)
    sin_ref[...] = jnp.sin(ang) * sgn_ref[...]


def _rope_tables(positions):
    b, t = positions.shape
    n = b * t
    tm = min(1024, n)
    lane = jnp.arange(LANE)
    inv = jnp.power(ROPE_BASE, -jnp.arange(0, HEAD_DIM, 2, dtype=F32) / HEAD_DIM)
    inv_l = inv[lane % (HEAD_DIM // 2)][None, :]
    sgn = jnp.where(lane % HEAD_DIM < HEAD_DIM // 2, -1.0, 1.0).astype(F32)[None, :]
    cos, sin = pl.pallas_call(
        _rope_kernel,
        grid=(n // tm,),
        in_specs=[pl.BlockSpec((tm, 1), lambda i: (i, 0)),
                  pl.BlockSpec((1, LANE), lambda i: (0, 0)),
                  pl.BlockSpec((1, LANE), lambda i: (0, 0))],
        out_specs=[pl.BlockSpec((tm, LANE), lambda i: (i, 0))] * 2,
        out_shape=[jax.ShapeDtypeStruct((n, LANE), F32)] * 2,
        compiler_params=_params("parallel"),
        name="rope_tables",
    )(positions.reshape(n, 1), inv_l, sgn)
    return cos.reshape(b, t, LANE), sin.reshape(b, t, LANE)


def _ret_kernel(lg_ref, x_ref, cos_ref, sin_ref, lgl_ref, gn_ref, avg_ref, *rest, C, fwd):
    if fwd:
        yo_ref, o_ref, s_ref, d_ref = rest
    else:
        o_ref, s_ref, d_ref = rest
    W = GROUP_W

    @pl.when(pl.program_id(1) == 0)
    def _init():
        s_ref[...] = jnp.zeros_like(s_ref)
        i, j = _iota((C, C), 0), _iota((C, C), 1)
        rel = ((i - j) if fwd else (j - i)).astype(F32)
        for h in range(N_HEADS):
            d_ref[h] = jnp.where(rel >= 0, jnp.exp(lg_ref[h] * jnp.maximum(rel, 0.0)), 0.0)

    x = x_ref[...]
    q, k, v, g = x[:, 0:W], x[:, W:2 * W], x[:, 2 * W:3 * W], x[:, 3 * W:4 * W]
    cos = jnp.concatenate([cos_ref[...]] * (W // LANE), axis=1)
    sin = jnp.concatenate([sin_ref[...]] * (W // LANE), axis=1)
    first_half = (_iota((1, W), 1) % HEAD_DIM) < (HEAD_DIM // 2)

    def rot(z):
        swapped = jnp.where(first_half, pltpu.roll(z, W - HEAD_DIM // 2, 1), pltpu.roll(z, HEAD_DIM // 2, 1))
        return z * cos + swapped * sin

    q = rot(q)
    k = rot(k) * (HEAD_DIM ** -0.5)
    lgl = lgl_ref[...]
    idx = _iota((C, 1), 0).astype(F32)
    if fwd:
        zeta = jnp.exp(lgl * (C - 1.0 - idx))
        xi = jnp.exp(lgl * (idx + 1.0))
    else:
        zeta = jnp.exp(lgl * idx)
        xi = jnp.exp(lgl * (C - idx))
    cd = jnp.exp(lgl * float(C))

    kb, vb = _bf(k), _bf(v)
    y = jnp.zeros((C, W), F32)
    for h in range(N_HEADS):
        mh = _head_mask(h, W, HEAD_DIM)
        s = _dot_nt(q * mh, kb)
        y = y + _dot(s * d_ref[h], vb) * mh
    state = s_ref[...]
    y = y + _dot(q * xi, state)
    bmask = ((_iota((W, W), 0) // HEAD_DIM) == (_iota((W, W), 1) // HEAD_DIM)).astype(F32)
    s_ref[...] = state * cd + _dot_tn(k * zeta, vb) * bmask

    if fwd:
        yt = y + yo_ref[...]
        avg = avg_ref[...]
        yc = yt - _head_stats(yt, avg)
        var = _head_stats(yc * yc, avg)
        o_ref[...] = yc * lax.rsqrt(var + NORM_EPS) * gn_ref[...] * (g * _sigmoid(g))
    else:
        o_ref[...] = y


def _retention(proj3, cos, sin, log_decay, gn_w, avg, tl):
    b, t, _ = proj3.shape
    c = tl["c_ret"]
    nc = t // c

    def call(fwd, lg, extra):
        def cm(i):
            return i if fwd else nc - 1 - i

        in_specs = [
            pl.BlockSpec(memory_space=pltpu.SMEM),
            pl.BlockSpec((None, c, 1024), lambda bi, i: (bi, cm(i), COL_RET // 1024)),
            pl.BlockSpec((None, c, LANE), lambda bi, i: (bi, cm(i), 0)),
            pl.BlockSpec((None, c, LANE), lambda bi, i: (bi, cm(i), 0)),
            pl.BlockSpec((1, GROUP_W), lambda bi, i: (0, 0)),
            pl.BlockSpec((1, GROUP_W), lambda bi, i: (0, 0)),
            pl.BlockSpec((GROUP_W, GROUP_W), lambda bi, i: (0, 0)),
        ]
        args = [lg, proj3, cos, sin, jnp.repeat(lg, HEAD_DIM)[None, :], gn_w[None, :], avg]
        if fwd:
            in_specs.append(pl.BlockSpec((None, c, GROUP_W), lambda bi, i: (bi, cm(i), 0)))
            args.append(extra)
        return pl.pallas_call(
            functools.partial(_ret_kernel, C=c, fwd=fwd),
            grid=(b, nc),
            in_specs=in_specs,
            out_specs=pl.BlockSpec((None, c, GROUP_W), lambda bi, i: (bi, cm(i), 0)),
            out_shape=jax.ShapeDtypeStruct((b, t, GROUP_W), F32),
            scratch_shapes=[pltpu.VMEM((GROUP_W, GROUP_W), F32), pltpu.VMEM((N_HEADS, c, c), F32)],
            compiler_params=_params("parallel", "arbitrary"),
            name="retention_fwd" if fwd else "retention_bwd",
        )(*args)

    y_b = call(False, log_decay[1], None)
    return call(True, log_decay[0], y_b)


def _gla_kernel(x_ref, aup_ref, ab_ref, gn_ref, avg_ref, *rest, TB, fwd):
    if fwd:
        yo_ref, o_ref, st_ref = rest
    else:
        o_ref, st_ref = rest
    W, KD, C = GROUP_W, GLA_KDIM, CHUNK

    @pl.when(pl.program_id(1) == 0)
    def _init():
        st_ref[...] = jnp.zeros_like(st_ref)

    x = x_ref[...]
    q = x[:, 0:KD] * (GLA_HEAD_K ** -0.5)
    k = x[:, KD:2 * KD]
    v = x[:, 2 * KD:2 * KD + W]
    og = x[:, 2 * KD + W:2 * KD + 2 * W]
    xa = x[:, 2 * KD + 2 * W:2 * KD + 2 * W + LANE]
    z = _dot_hp(xa, aup_ref[...]) + ab_ref[...]
    la = -_softplus(-z) * (1.0 / GLA_TAU)
    tri = jnp.where(_tri_mask(C, fwd, False), 1.0, 0.0).astype(BF16)
    tri4 = jnp.concatenate([_tri_mask(C, fwd, False)] * N_HEADS, axis=0)
    bmask_t = ((_iota((W, KD), 0) // HEAD_DIM) == (_iota((W, KD), 1) // GLA_HEAD_K)).astype(F32)

    n_chunks = TB // C
    order = range(n_chunks) if fwd else range(n_chunks - 1, -1, -1)
    for c in order:
        sl = slice(c * C, (c + 1) * C)
        bc = _dot_exact_lhs(tri, la[sl])
        bl = bc[C - 1:C] if fwd else bc[0:1]
        qd = q[sl] * jnp.exp(bc)
        kd = k[sl] * jnp.exp(-bc)
        kh = k[sl] * jnp.exp(bl - bc)
        vb = _bf(v[sl])
        sc = jnp.where(tri4, _dot_nt(_stack_heads(qd, GLA_HEAD_K), kd), 0.0)
        ys = _dot(sc, vb)
        y = sum(ys[h * C:(h + 1) * C] * _head_mask(h, W, HEAD_DIM) for h in range(N_HEADS))
        st = st_ref[...]
        y = y + _dot_nt(qd, st)
        st_ref[...] = st * jnp.exp(bl) + _dot_tn(vb, kh) * bmask_t
        o_ref[sl, :] = y

    if fwd:
        yt = o_ref[...] + yo_ref[...]
        var = _head_stats(yt * yt, avg_ref[...])
        o_ref[...] = yt * lax.rsqrt(var + NORM_EPS) * gn_ref[...] * (og * _sigmoid(og))


def _gla(proj3, alpha_up, alpha_b, gn_w, avg, tl):
    b, t, _ = proj3.shape
    tb = tl["tb_gla"]
    nb = t // tb

    def call(fwd, d, extra):
        def bm(i):
            return i if fwd else nb - 1 - i

        aup = jnp.zeros((LANE, GLA_KDIM), F32).at[:alpha_up.shape[1]].set(alpha_up[d])
        in_specs = [
            pl.BlockSpec((None, tb, 1024), lambda bi, i: (bi, bm(i), COL_GLA // 1024)),
            pl.BlockSpec((LANE, GLA_KDIM), lambda bi, i: (0, 0)),
            pl.BlockSpec((1, GLA_KDIM), lambda bi, i: (0, 0)),
            pl.BlockSpec((1, GROUP_W), lambda bi, i: (0, 0)),
            pl.BlockSpec((GROUP_W, GROUP_W), lambda bi, i: (0, 0)),
        ]
        args = [proj3, aup, alpha_b[d][None, :], gn_w[None, :], avg]
        if fwd:
            in_specs.append(pl.BlockSpec((None, tb, GROUP_W), lambda bi, i: (bi, bm(i), 0)))
            args.append(extra)
        return pl.pallas_call(
            functools.partial(_gla_kernel, TB=tb, fwd=fwd),
            grid=(b, nb),
            in_specs=in_specs,
            out_specs=pl.BlockSpec((None, tb, GROUP_W), lambda bi, i: (bi, bm(i), 0)),
            out_shape=jax.ShapeDtypeStruct((b, t, GROUP_W), F32),
            scratch_shapes=[pltpu.VMEM((GROUP_W, GLA_KDIM), F32)],
            compiler_params=_params("parallel", "arbitrary"),
            name="gla_fwd" if fwd else "gla_bwd",
        )(*args)

    y_b = call(False, 1, None)
    return call(True, 0, y_b)


def _lru_kernel(x_ref, hp_ref, hn_ref, cw_ref, cb_ref, gw_ref, gb_ref, lam_ref, *rest, TB, NB, fwd):
    if fwd:
        ho_ref, o_ref, h_ref = rest
    else:
        o_ref, h_ref = rest
    W = GROUP_W
    i = pl.program_id(1)
    blk = i if fwd else NB - 1 - i

    @pl.when(i == 0)
    def _init():
        h_ref[...] = jnp.zeros_like(h_ref)

    x = x_ref[...]
    xr, gate = x[:, 0:W], x[:, W:2 * W]
    has_prev = (blk > 0).astype(F32)
    has_next = (blk < NB - 1).astype(F32)
    prev = hp_ref[...][:, 0:W] * has_prev
    nxt = hn_ref[...][:, 0:W] * has_next
    row = _iota((TB, 1), 0)
    x_m1 = jnp.where(row == 0, prev[7:8], pltpu.roll(xr, 1, 0))
    x_m2 = jnp.where(row == 0, prev[6:7], jnp.where(row == 1, prev[7:8], pltpu.roll(xr, 2, 0)))
    x_p1 = jnp.where(row == TB - 1, nxt[0:1], pltpu.roll(xr, TB - 1, 0))
    cw = cw_ref[...]
    xc = x_m2 * cw[0:1] + x_m1 * cw[1:2] + xr * cw[2:3] + x_p1 * cw[3:4] + cb_ref[...]

    gx = _dot(xc, gw_ref[...]) + gb_ref[...]
    rec = _sigmoid(gx[:, 0:W])
    ing = _sigmoid(gx[:, W:2 * W])
    log_a = -RGLRU_C * rec * _softplus(-lam_ref[...])
    a = jnp.exp(log_a)
    bx = jnp.sqrt(-jnp.tanh(log_a) * (a * a + 1.0)) * (ing * xc)

    s = 1
    while s < TB:
        if fwd:
            keep = row >= s
            a_sh = jnp.where(keep, pltpu.roll(a, s, 0), 1.0)
            b_sh = jnp.where(keep, pltpu.roll(bx, s, 0), 0.0)
        else:
            keep = row < TB - s
            a_sh = jnp.where(keep, pltpu.roll(a, TB - s, 0), 1.0)
            b_sh = jnp.where(keep, pltpu.roll(bx, TB - s, 0), 0.0)
        bx = a * b_sh + bx
        a = a * a_sh
        s *= 2
    h = a * h_ref[...] + bx
    h_ref[...] = h[TB - 1:TB] if fwd else h[0:1]

    if fwd:
        o_ref[...] = (h + ho_ref[...]) * jax.nn.gelu(gate, approximate=True)
    else:
        o_ref[...] = h


def _block_diag_heads(w):
    h, d, e = w.shape
    eye = jnp.eye(h, dtype=w.dtype)
    return jnp.einsum('hde,hg->hdge', w, eye).reshape(h * d, h * e)


def _rglru(proj3, conv_w, conv_b, gate_w, gate_b, lam, tl):
    b, t, _ = proj3.shape
    tb = tl["tb_lru"]
    nb = t // tb
    r8 = tb // SUBLANE
    last8 = t // SUBLANE - 1
    colb = COL_LRU // 512

    def call(fwd, d, extra):
        def bm(i):
            return i if fwd else nb - 1 - i

        gw = jnp.concatenate([_block_diag_heads(gate_w[d, 0]), _block_diag_heads(gate_w[d, 1])], axis=1)
        gb = jnp.concatenate([gate_b[d, 0], gate_b[d, 1]])[None, :]
        in_specs = [
            pl.BlockSpec((None, tb, 512), lambda bi, i: (bi, bm(i), colb)),
            pl.BlockSpec((None, SUBLANE, 512), lambda bi, i: (bi, jnp.maximum(bm(i) * r8 - 1, 0), colb)),
            pl.BlockSpec((None, SUBLANE, 512), lambda bi, i: (bi, jnp.minimum((bm(i) + 1) * r8, last8), colb)),
            pl.BlockSpec((4, GROUP_W), lambda bi, i: (0, 0)),
            pl.BlockSpec((1, GROUP_W), lambda bi, i: (0, 0)),
            pl.BlockSpec((GROUP_W, 2 * GROUP_W), lambda bi, i: (0, 0)),
            pl.BlockSpec((1, 2 * GROUP_W), lambda bi, i: (0, 0)),
            pl.BlockSpec((1, GROUP_W), lambda bi, i: (0, 0)),
        ]
        args = [proj3, proj3, proj3, conv_w, conv_b[None, :], _bf(gw), gb, lam[d][None, :]]
        if fwd:
            in_specs.append(pl.BlockSpec((None, tb, GROUP_W), lambda bi, i: (bi, bm(i), 0)))
            args.append(extra)
        return pl.pallas_call(
            functools.partial(_lru_kernel, TB=tb, NB=nb, fwd=fwd),
            grid=(b, nb),
            in_specs=in_specs,
            out_specs=pl.BlockSpec((None, tb, GROUP_W), lambda bi, i: (bi, bm(i), 0)),
            out_shape=jax.ShapeDtypeStruct((b, t, GROUP_W), F32),
            scratch_shapes=[pltpu.VMEM((1, GROUP_W), F32)],
            compiler_params=_params("parallel", "arbitrary"),
            name="rglru_fwd" if fwd else "rglru_bwd",
        )(*args)

    h_b = call(False, 1, None)
    return call(True, 0, h_b)


def _rwkv_kernel(x_ref, halo_ref, mu_ref, lora_ref, w0_ref, a0_ref, kk_ref, ka_ref, rk_ref, ones_ref,
                 *rest, TB, NB, fwd):
    if fwd:
        gup_ref, gn_ref, avg_ref, yo_ref, bo_ref, o_ref, s_ref = rest
    else:
        y_ref, b_ref, s_ref = rest
    W, C = GROUP_W, CHUNK
    i = pl.program_id(1)
    blk = i if fwd else NB - 1 - i

    @pl.when(i == 0)
    def _init():
        s_ref[...] = jnp.zeros_like(s_ref)

    x = x_ref[...]
    NS = 3 * W + LANE
    zs = x[:, 0:NS]
    halo = halo_ref[...][:, 0:NS]
    if fwd:
        edge = halo[SUBLANE - 1:SUBLANE] * (blk > 0).astype(F32)
    else:
        edge = halo[0:1] * (blk < NB - 1).astype(F32)
    zs = zs + mu_ref[...] * (_shift_rows(zs, edge, fwd) - zs)
    r, k, v, xwa = zs[:, 0:W], zs[:, W:2 * W], zs[:, 2 * W:3 * W], zs[:, 3 * W:NS]
    lane_l = _iota((1, LANE), 1)
    xwa = jnp.where(lane_l < LANE // 2, jnp.tanh(xwa), xwa)
    lo = _dot_hp(xwa, lora_ref[...])
    w_log = -_softplus(-(w0_ref[...] + lo[:, 0:W])) - 0.5
    lw = -jnp.exp(w_log)
    a = _sigmoid(a0_ref[...] + lo[:, W:2 * W])
    ones_bd = ones_ref[...]
    kk = k * kk_ref[...]
    kk = kk / jnp.maximum(jnp.sqrt(_dot_exact_rhs(kk * kk, ones_bd)), 1e-12)
    k = k * (1.0 + (a - 1.0) * ka_ref[...])
    bonus = _dot_exact_rhs(r * k * rk_ref[...], ones_bd) * v
    am = -kk
    bm = kk * a

    tri = jnp.where(_tri_mask(C, fwd, False), 1.0, 0.0).astype(BF16)
    m_strict = _block_tri_mask(C, fwd, True)
    m_incl = _block_tri_mask(C, fwd, False)
    n4 = N_HEADS * C
    eye = (_iota((n4, n4), 0) == _iota((n4, n4), 1)).astype(F32)

    n_chunks = TB // C
    order = range(n_chunks) if fwd else range(n_chunks - 1, -1, -1)
    for c in order:
        sl = slice(c * C, (c + 1) * C)
        lwc = lw[sl]
        cum = _dot_exact_lhs(tri, lwc)
        tot = cum[C - 1:C] if fwd else cum[0:1]
        e_neg = jnp.exp(-cum)
        e_end = jnp.exp(tot - cum)
        a_t = _stack_heads(am[sl] * jnp.exp(cum - lwc), HEAD_DIM)
        r_t = _stack_heads(r[sl] * jnp.exp(cum), HEAD_DIM)
        b_t = _stack_heads(bm[sl] * e_neg, HEAD_DIM)
        k_t = _stack_heads(k[sl] * e_neg, HEAD_DIM)
        b_h = _stack_heads(bm[sl] * e_end, HEAD_DIM)
        k_h = _stack_heads(k[sl] * e_end, HEAD_DIM)
        v_s = _stack_heads(v[sl], HEAD_DIM)

        g = _dot_nt(jnp.concatenate([a_t, r_t], axis=0), jnp.concatenate([b_t, k_t], axis=0))
        a_ab = jnp.where(m_strict, g[0:n4, 0:n4], 0.0)
        a_ak = jnp.where(m_strict, g[0:n4, n4:2 * n4], 0.0)
        a_rb = jnp.where(m_incl, g[n4:2 * n4, 0:n4], 0.0)
        a_rk = jnp.where(m_incl, g[n4:2 * n4, n4:2 * n4], 0.0)

        tinv = eye + a_ab
        p = a_ab
        s = 1
        while 2 * s < C:
            p = _dot(p, p)
            tinv = tinv + _dot(tinv, p)
            s *= 2

        w1 = _dot(tinv, a_t)
        w2 = _dot(tinv, _dot(a_ak, v_s))
        qp = r_t + _dot(a_rb, w1)
        y0 = _dot(a_rk, v_s) + _dot(a_rb, w2)
        mt = _dot_tn(w1, b_h)
        nt = _dot_tn(v_s, k_h) + _dot_tn(w2, b_h)

        state = s_ref[...]
        y = _unstack_sum(_dot_nt(qp, state) + y0, C)
        s_ref[...] = state * jnp.exp(tot) + _dot(state, mt) + nt
        if fwd:
            o_ref[sl, :] = y
        else:
            y_ref[sl, :] = y

    if fwd:
        yt = o_ref[...] + yo_ref[...]
        avg = avg_ref[...]
        yc = yt - _head_stats(yt, avg)
        var = _head_stats(yc * yc, avg)
        o = yc * lax.rsqrt(var + RWKV_GN_EPS) * gn_ref[...] + bonus + bo_ref[...]
        xg = x[:, NS:NS + LANE]
        o_ref[...] = o * _dot_hp(_sigmoid(xg), gup_ref[...])
    else:
        b_ref[...] = bonus


def _rwkv7(proj3, mu_rkv, mu_w, mu_a, w0, w_up, a0, a_up, g_up, k_k, k_a, r_k, gn_w, avg, ones_bd, tl):
    b, t, _ = proj3.shape
    tb = tl["tb_rwkv"]
    nb = t // tb
    r8 = tb // SUBLANE
    last8 = t // SUBLANE - 1
    colb = COL_RWKV // 1024
    W = GROUP_W
    mu = jnp.concatenate([mu_rkv.reshape(-1), mu_w, mu_a])[None, :]
    nl = w_up.shape[1]

    def call(fwd, d, extra):
        def bm(i):
            return i if fwd else nb - 1 - i

        lora = jnp.zeros((LANE, 2 * W), F32).at[0:nl, 0:W].set(w_up[d]).at[nl:2 * nl, W:2 * W].set(a_up[d])
        if fwd:
            halo_map = lambda bi, i: (bi, jnp.maximum(bm(i) * r8 - 1, 0), colb)
        else:
            halo_map = lambda bi, i: (bi, jnp.minimum((bm(i) + 1) * r8, last8), colb)
        row = lambda n: pl.BlockSpec((1, n), lambda bi, i: (0, 0))
        in_specs = [
            pl.BlockSpec((None, tb, 1024), lambda bi, i: (bi, bm(i), colb)),
            pl.BlockSpec((None, SUBLANE, 1024), halo_map),
            row(3 * W + LANE),
            pl.BlockSpec((LANE, 2 * W), lambda bi, i: (0, 0)),
            row(W), row(W), row(W), row(W), row(W),
            pl.BlockSpec((W, W), lambda bi, i: (0, 0)),
        ]
        args = [proj3, proj3, mu, lora, w0[d][None, :], a0[d][None, :], k_k[None, :], k_a[None, :],
                r_k.reshape(1, W), ones_bd]
        blk = pl.BlockSpec((None, tb, W), lambda bi, i: (bi, bm(i), 0))
        if fwd:
            in_specs += [pl.BlockSpec((LANE, W), lambda bi, i: (0, 0)), row(W),
                         pl.BlockSpec((W, W), lambda bi, i: (0, 0)), blk, blk]
            args += [g_up, gn_w[None, :], avg, extra[0], extra[1]]
            out_specs = blk
            out_shape = jax.ShapeDtypeStruct((b, t, W), F32)
        else:
            out_specs = [blk, blk]
            out_shape = [jax.ShapeDtypeStruct((b, t, W), F32)] * 2
        return pl.pallas_call(
            functools.partial(_rwkv_kernel, TB=tb, NB=nb, fwd=fwd),
            grid=(b, nb),
            in_specs=in_specs,
            out_specs=out_specs,
            out_shape=out_shape,
            scratch_shapes=[pltpu.VMEM((W, W), F32)],
            compiler_params=_params("parallel", "arbitrary"),
            name="rwkv7_fwd" if fwd else "rwkv7_bwd",
        )(*args)

    y_b, bonus_b = call(False, 1, None)
    return call(True, 0, (y_b, bonus_b))


def _out_proj_kernel(x_ref, o1_ref, o2_ref, o3_ref, o4_ref, w_ref, g_ref, rw_ref, rb_ref, rwt_ref, rbt_ref,
                     xo_ref, xn_ref, aff_ref, afft_ref):
    W = GROUP_W
    w = w_ref[...]
    acc = x_ref[...]
    for gi, o_ref in enumerate((o1_ref, o2_ref, o3_ref, o4_ref)):
        acc = acc + jnp.dot(_bf(o_ref[...]), w[gi * W:(gi + 1) * W], preferred_element_type=F32)
    xo_ref[...] = acc
    ms = jnp.mean(acc * acc, axis=-1, keepdims=True)
    xn = acc * lax.rsqrt(ms + NORM_EPS) * g_ref[...]
    xn_ref[...] = _bf(xn)
    logits = _dot_hp(xn, rw_ref[...]) + rb_ref[...]
    logits = logits - jnp.max(logits, axis=-1, keepdims=True)
    e = jnp.exp(logits)
    aff_ref[...] = e / jnp.sum(e, axis=-1, keepdims=True)
    lt = _dot_hp_nt(rwt_ref[...], xn) + rbt_ref[...]
    lt = lt - jnp.max(lt, axis=0, keepdims=True)
    et = jnp.exp(lt)
    afft_ref[...] = et / jnp.sum(et, axis=0, keepdims=True)


def _out_proj(x2, outs, w_out_bf, gain, router_w, router_b, b, t, tl):
    n, d = x2.shape
    tm = tl["tm_out"]
    tpb = t // tm
    E = N_EXPERTS
    rw = jnp.zeros((d, LANE), F32).at[:, :E].set(router_w)
    rb = jnp.full((1, LANE), -1e30, F32).at[0, :E].set(router_b)
    full = lambda r, c: pl.BlockSpec((r, c), lambda i: (0, 0))
    tile = lambda c: pl.BlockSpec((tm, c), lambda i: (i, 0))
    return pl.pallas_call(
        _out_proj_kernel,
        grid=(n // tm,),
        in_specs=[tile(d)] + [tile(GROUP_W)] * 4 + [full(d, d), full(1, d), full(d, LANE), full(1, LANE),
                                                     full(E, d), full(E, 1)],
        out_specs=[tile(d), tile(d), tile(LANE),
                   pl.BlockSpec((None, E, tm), lambda i: (i // tpb, 0, i % tpb))],
        out_shape=[jax.ShapeDtypeStruct((n, d), F32), jax.ShapeDtypeStruct((n, d), BF16),
                   jax.ShapeDtypeStruct((n, LANE), F32), jax.ShapeDtypeStruct((b, E, t), F32)],
        compiler_params=_params("parallel"),
        name="out_proj_router",
    )(x2, *outs, w_out_bf, gain, rw, rb, router_w.T, router_b[:, None])


def _prefix_count(m, triu):
    e, t = m.shape
    nblk = t // LANE
    stacked = jnp.concatenate([m[:, j * LANE:(j + 1) * LANE] for j in range(nblk)], axis=0)
    incl = jnp.dot(_bf(stacked), triu, preferred_element_type=F32)
    pieces = []
    off = jnp.zeros((e, 1), F32)
    for j in range(nblk):
        blk = incl[j * e:(j + 1) * e]
        pieces.append(blk - stacked[j * e:(j + 1) * e] + off)
        off = off + blk[:, LANE - 1:LANE]
    return jnp.concatenate(pieces, axis=1)


def _select_kernel(afft_ref, rank_ref, rankc_ref, *, cap):
    aff = afft_ref[...]
    e, t = aff.shape
    bits = lax.bitcast_convert_type(aff, jnp.int32)

    def body(i, prefix):
        cand = prefix | jnp.left_shift(jnp.int32(1), 30 - i)
        cnt = jnp.sum((bits >= cand).astype(F32), axis=1, keepdims=True)
        return jnp.where(cnt >= cap, cand, prefix)

    thr = lax.fori_loop(0, 31, body, jnp.zeros((e, 1), jnp.int32))
    gt = (bits > thr).astype(F32)
    eq = (bits == thr).astype(F32)
    need = cap - jnp.sum(gt, axis=1, keepdims=True)
    triu = jnp.where(_tri_mask(LANE, False, False), 1.0, 0.0).astype(BF16)
    sel = gt + eq * (_prefix_count(eq, triu) < need).astype(F32)
    rank = jnp.where(sel > 0.5, _prefix_count(sel, triu), -1.0)
    rank_ref[...] = rank
    padded = jnp.concatenate([rank, jnp.full((LANE - e, t), -1.0, F32)], axis=0)
    rankc_ref[...] = padded.T


def _select(afft, cap):
    b, e, t = afft.shape
    return pl.pallas_call(
        functools.partial(_select_kernel, cap=cap),
        grid=(b,),
        in_specs=[pl.BlockSpec((None, e, t), lambda i: (i, 0, 0))],
        out_specs=[pl.BlockSpec((None, e, t), lambda i: (i, 0, 0)),
                   pl.BlockSpec((None, t, LANE), lambda i: (i, 0, 0))],
        out_shape=[jax.ShapeDtypeStruct((b, e, t), F32), jax.ShapeDtypeStruct((b, t, LANE), F32)],
        compiler_params=_params("parallel"),
        name="expert_choice_select",
    )(afft)


def _gather_kernel(rank_ref, xn_ref, xs_ref, *, cap, tk):
    e = pl.program_id(1)
    t = xn_ref.shape[0]
    r = rank_ref[pl.ds(e, 1), :]
    slot = _iota((cap, tk), 0).astype(F32)
    acc = jnp.zeros((cap, xn_ref.shape[1]), F32)
    for j in range(t // tk):
        onehot = jnp.where(r[:, j * tk:(j + 1) * tk] == slot, 1.0, 0.0).astype(BF16)
        acc = acc + jnp.dot(onehot, xn_ref[j * tk:(j + 1) * tk, :], preferred_element_type=F32)
    xs_ref[...] = _bf(acc)


def _gather(rank, xn3, cap, tl):
    b, e, t = rank.shape
    d = xn3.shape[2]
    return pl.pallas_call(
        functools.partial(_gather_kernel, cap=cap, tk=tl["tk_gather"]),
        grid=(b, e),
        in_specs=[pl.BlockSpec((None, e, t), lambda bi, ei: (bi, 0, 0)),
                  pl.BlockSpec((None, t, d), lambda bi, ei: (bi, 0, 0))],
        out_specs=pl.BlockSpec((None, None, cap, d), lambda bi, ei: (bi, ei, 0, 0)),
        out_shape=jax.ShapeDtypeStruct((b, e, cap, d), BF16),
        compiler_params=_params("parallel", "arbitrary"),
        name="moe_gather",
    )(rank, xn3)


def _ffn_kernel(xs_ref, wg_ref, wu_ref, wd_ref, o_ref, wgb_ref, wub_ref, wdb_ref):
    @pl.when(pl.program_id(1) == 0)
    def _():
        wgb_ref[...] = _bf(wg_ref[...])
        wub_ref[...] = _bf(wu_ref[...])
        wdb_ref[...] = _bf(wd_ref[...])

    xs = xs_ref[...]
    hg = jnp.dot(xs, wgb_ref[...], preferred_element_type=F32)
    hu = jnp.dot(xs, wub_ref[...], preferred_element_type=F32)
    hid = _bf(hg * _sigmoid(hg) * hu)
    o_ref[...] = _bf(jnp.dot(hid, wdb_ref[...], preferred_element_type=F32))


def _ffn(xs, w_gate, w_up, w_down):
    b, e, cap, d = xs.shape
    f = w_gate.shape[2]
    wspec = lambda r, c: pl.BlockSpec((None, r, c), lambda ei, bi: (ei, 0, 0))
    xspec = pl.BlockSpec((None, None, cap, d), lambda ei, bi: (bi, ei, 0, 0))
    return pl.pallas_call(
        _ffn_kernel,
        grid=(e, b),
        in_specs=[xspec, wspec(d, f), wspec(d, f), wspec(f, d)],
        out_specs=xspec,
        out_shape=jax.ShapeDtypeStruct((b, e, cap, d), BF16),
        scratch_shapes=[pltpu.VMEM((d, f), BF16), pltpu.VMEM((d, f), BF16), pltpu.VMEM((f, d), BF16)],
        compiler_params=_params("arbitrary", "arbitrary"),
        name="moe_ffn",
    )(xs, w_gate, w_up, w_down)


def _combine_kernel(x_ref, rankc_ref, aff_ref, o_ref, g_ref, out_ref, *, cap, final_norm):
    rankc = rankc_ref[...]
    aff = aff_ref[...]
    tm = x_ref.shape[0]
    slot = _iota((tm, cap), 1).astype(F32)
    acc = x_ref[...]
    for e in range(N_EXPERTS):
        pt = jnp.where(rankc[:, e:e + 1] == slot, aff[:, e:e + 1], 0.0)
        acc = acc + jnp.dot(_bf(pt), o_ref[e * cap:(e + 1) * cap, :], preferred_element_type=F32)
    if final_norm:
        ms = jnp.mean(acc * acc, axis=-1, keepdims=True)
        acc = acc * lax.rsqrt(ms + NORM_EPS) * g_ref[...]
    out_ref[...] = acc


def _combine(x3, rankc, aff3, o_flat, gain, cap, final_norm, tl):
    b, t, d = x3.shape
    tm = tl["tm_comb"]
    tile = lambda c: pl.BlockSpec((None, tm, c), lambda bi, i: (bi, i, 0))
    return pl.pallas_call(
        functools.partial(_combine_kernel, cap=cap, final_norm=final_norm),
        grid=(b, t // tm),
        in_specs=[tile(d), tile(LANE), tile(LANE),
                  pl.BlockSpec((None, N_EXPERTS * cap, d), lambda bi, i: (bi, 0, 0),
                               pipeline_mode=pl.Buffered(1)),
                  pl.BlockSpec((1, d), lambda bi, i: (0, 0))],
        out_specs=tile(d),
        out_shape=jax.ShapeDtypeStruct((b, t, d), F32),
        compiler_params=_params("parallel", "arbitrary"),
        name="moe_combine",
    )(x3, rankc, aff3, o_flat, gain)


def kernel(x, positions, norm_mix, w_in, w_out, ret_log_decay, ret_gn, rwkv_mu_rkv, rwkv_mu_w, rwkv_mu_a, rwkv_w0, rwkv_w_up, rwkv_a0, rwkv_a_up, rwkv_g_up, rwkv_k_k, rwkv_k_a, rwkv_r_k, rwkv_gn, lru_conv_w, lru_conv_b, lru_gate_w, lru_gate_b, lru_lambda, gla_alpha_up, gla_alpha_b, gla_gn, norm_ffn, router_w, router_b, exp_w_gate, exp_w_up, exp_w_down, norm_final):
    b, t, d = x.shape
    depth = w_in.shape[0]
    n = b * t
    tl = _tiles(t)
    cap = EC_CAPACITY_FACTOR * t // N_EXPERTS
    lane = jnp.arange(GROUP_W)
    same_head = (lane[:, None] // HEAD_DIM) == (lane[None, :] // HEAD_DIM)
    avg = jnp.where(same_head, 1.0 / HEAD_DIM, 0.0).astype(BF16)
    ones_bd = jnp.where(same_head, 1.0, 0.0).astype(BF16)
    cos, sin = _rope_tables(positions)

    x2 = x.reshape(n, d)
    for l in range(depth):
        w = w_in[l]
        split = 2048 + 512
        w_perm = jnp.concatenate(
            [w[:, :2048], w[:, split:], jnp.zeros((d, IN_COLS_PAD - w.shape[1]), F32), w[:, 2048:split]], axis=1)
        proj3 = _in_proj(x2, norm_mix[l][None, :], _bf(w_perm), tl).reshape(b, t, IN_COLS_PAD)

        o_ret = _retention(proj3, cos, sin, ret_log_decay[l], ret_gn[l], avg, tl)
        o_rwkv = _rwkv7(proj3, rwkv_mu_rkv[l], rwkv_mu_w[l], rwkv_mu_a[l], rwkv_w0[l], rwkv_w_up[l],
                        rwkv_a0[l], rwkv_a_up[l], rwkv_g_up[l], rwkv_k_k[l], rwkv_k_a[l], rwkv_r_k[l],
                        rwkv_gn[l], avg, ones_bd, tl)
        o_lru = _rglru(proj3, lru_conv_w[l], lru_conv_b[l], lru_gate_w[l], lru_gate_b[l], lru_lambda[l], tl)
        o_gla = _gla(proj3, gla_alpha_up[l], gla_alpha_b[l], gla_gn[l], avg, tl)
        outs = [o.reshape(n, GROUP_W) for o in (o_ret, o_rwkv, o_lru, o_gla)]

        x_mid, xn, aff, afft = _out_proj(x2, outs, _bf(w_out[l]), norm_ffn[l][None, :], router_w[l],
                                         router_b[l], b, t, tl)
        rank, rankc = _select(afft, cap)
        xs = _gather(rank, xn.reshape(b, t, d), cap, tl)
        o_exp = _ffn(xs, exp_w_gate[l], exp_w_up[l], exp_w_down[l])
        x3 = _combine(x_mid.reshape(b, t, d), rankc, aff.reshape(b, t, LANE),
                      o_exp.reshape(b, N_EXPERTS * cap, d), norm_final[None, :], cap, l == depth - 1, tl)
        x2 = x3.reshape(n, d)
    return x2.reshape(b, t, d)
```

```python
import functools

import jax
import jax.numpy as jnp
from jax import lax
from jax.experimental import pallas as pl
from jax.experimental.pallas import tpu as pltpu

F32 = jnp.float32
BF16 = jnp.bfloat16

D_MODEL = 1024
GROUP_W = 256
N_HEADS = 4
HEAD_DIM = 64
ROPE_BASE = 10000.0
RWKV_GN_EPS = 64e-5
RGLRU_C = 8.0
GLA_KDIM = 128
GLA_HEAD_K = 32
GLA_TAU = 16.0
N_EXPERTS = 16
EC_CAPACITY_FACTOR = 2
NORM_EPS = 1e-6

LANE = 128
SUBLANE = 8
CHUNK = 64
VMEM_LIMIT = 56 * 1024 * 1024

COL_RET, COL_RWKV, COL_GLA, COL_LRU = 0, 1024, 2048, 3072
IN_COLS_PAD = 3584


def _tiles(T):
    return dict(
        tm_proj=min(1024, T), tn_proj=512,
        c_ret=min(256, T), tb_gla=min(512, T), tb_lru=min(512, T), tb_rwkv=min(256, T),
        tm_out=min(512, T), tm_comb=min(1024, T), tk_gather=min(1024, T),
    )


def _params(*sem):
    return pltpu.CompilerParams(dimension_semantics=sem, vmem_limit_bytes=VMEM_LIMIT)


def _bf(x):
    return x.astype(BF16)


def _dot(a, b):
    return jnp.dot(_bf(a), _bf(b), preferred_element_type=F32)


def _dot_nt(a, b):
    return lax.dot_general(_bf(a), _bf(b), (((1,), (1,)), ((), ())), preferred_element_type=F32)


def _dot_tn(a, b):
    return lax.dot_general(_bf(a), _bf(b), (((0,), (0,)), ((), ())), preferred_element_type=F32)


def _split2(a):
    hi = _bf(a)
    lo = _bf(a - hi.astype(F32))
    return hi, lo


def _split3(a):
    hi = _bf(a)
    r = a - hi.astype(F32)
    mid = _bf(r)
    lo = _bf(r - mid.astype(F32))
    return hi, mid, lo


def _dot_exact_lhs(m, a):
    hi, mid, lo = _split3(a)
    return (jnp.dot(m, hi, preferred_element_type=F32) + jnp.dot(m, mid, preferred_element_type=F32)
            + jnp.dot(m, lo, preferred_element_type=F32))


def _dot_exact_rhs(a, m):
    hi, lo = _split2(a)
    return jnp.dot(hi, m, preferred_element_type=F32) + jnp.dot(lo, m, preferred_element_type=F32)


def _dot_hp(a, b):
    ah, al = _split2(a)
    bh, bl = _split2(b)
    return (jnp.dot(ah, bh, preferred_element_type=F32) + jnp.dot(ah, bl, preferred_element_type=F32)
            + jnp.dot(al, bh, preferred_element_type=F32))


def _dot_hp_nt(a, b):
    ah, al = _split2(a)
    bh, bl = _split2(b)
    dn = (((1,), (1,)), ((), ()))
    return (lax.dot_general(ah, bh, dn, preferred_element_type=F32)
            + lax.dot_general(ah, bl, dn, preferred_element_type=F32)
            + lax.dot_general(al, bh, dn, preferred_element_type=F32))


def _sigmoid(x):
    return 1.0 / (1.0 + jnp.exp(-x))


def _softplus(x):
    return jnp.maximum(x, 0.0) + jnp.log(1.0 + jnp.exp(-jnp.abs(x)))


def _iota(shape, dim):
    return lax.broadcasted_iota(jnp.int32, shape, dim)


def _head_mask(h, width, hd):
    return (_iota((1, width), 1) // hd == h).astype(F32)


def _stack_heads(z, hd):
    w = z.shape[1]
    return jnp.concatenate([z * _head_mask(h, w, hd) for h in range(N_HEADS)], axis=0)


def _block_diag_bf(z):
    w = z.shape[1]
    lane_head = _iota((1, w), 1) // (w // N_HEADS)
    zero = jnp.zeros_like(z)
    return jnp.concatenate([jnp.where(lane_head == h, z, zero) for h in range(N_HEADS)], axis=0)


def _unstack_sum(z, c):
    return z[0:c] + z[c:2 * c] + z[2 * c:3 * c] + z[3 * c:4 * c]


def _tri_mask(n, lower, strict):
    i, j = _iota((n, n), 0), _iota((n, n), 1)
    if lower:
        return (j < i) if strict else (j <= i)
    return (j > i) if strict else (j >= i)


def _block_tri_mask(c, lower, strict):
    n = N_HEADS * c
    i, j = _iota((n, n), 0), _iota((n, n), 1)
    same = (i // c) == (j // c)
    ii, jj = i % c, j % c
    if lower:
        t = (jj < ii) if strict else (jj <= ii)
    else:
        t = (jj > ii) if strict else (jj >= ii)
    return same & t


def _head_stats(y, avg):
    return _dot_exact_rhs(y, avg)


def _shift_rows(z, edge, fwd):
    n = z.shape[0]
    row = _iota((n, 1), 0)
    if fwd:
        return jnp.where(row == 0, edge, pltpu.roll(z, 1, 0))
    return jnp.where(row == n - 1, edge, pltpu.roll(z, n - 1, 0))


def _in_proj_kernel(x_ref, g_ref, w_ref, o_ref, xn_ref):
    @pl.when(pl.program_id(1) == 0)
    def _():
        x = x_ref[...]
        ms = jnp.mean(x * x, axis=-1, keepdims=True)
        xn_ref[...] = _bf(x * lax.rsqrt(ms + NORM_EPS) * g_ref[...])

    o_ref[...] = jnp.dot(xn_ref[...], w_ref[...], preferred_element_type=F32)


def _in_proj(x2, gain, w_bf, tl):
    n, d = x2.shape
    tm, tn = tl["tm_proj"], tl["tn_proj"]
    nc = w_bf.shape[1]
    return pl.pallas_call(
        _in_proj_kernel,
        grid=(n // tm, nc // tn),
        in_specs=[pl.BlockSpec((tm, d), lambda i, j: (i, 0)),
                  pl.BlockSpec((1, d), lambda i, j: (0, 0)),
                  pl.BlockSpec((d, tn), lambda i, j: (0, j))],
        out_specs=pl.BlockSpec((tm, tn), lambda i, j: (i, j)),
        out_shape=jax.ShapeDtypeStruct((n, nc), F32),
        scratch_shapes=[pltpu.VMEM((tm, d), BF16)],
        compiler_params=_params("parallel", "arbitrary"),
        name="in_proj",
    )(x2, gain, w_bf)


def _rope_kernel(pos_ref, inv_ref, sgn_ref, cos_ref, sin_ref):
    ang = pos_ref[...].astype(F32) * inv_ref[...]
    cos_ref[...] = jnp.cos(ang)
    sin_ref[...] = jnp.sin(ang) * sgn_ref[...]


def _rope_tables(positions):
    b, t = positions.shape
    n = b * t
    tm = min(1024, n)
    lane = jnp.arange(LANE)
    inv = jnp.power(ROPE_BASE, -jnp.arange(0, HEAD_DIM, 2, dtype=F32) / HEAD_DIM)
    inv_l = inv[lane % (HEAD_DIM // 2)][None, :]
    sgn = jnp.where(lane % HEAD_DIM < HEAD_DIM // 2, -1.0, 1.0).astype(F32)[None, :]
    cos, sin = pl.pallas_call(
        _rope_kernel,
        grid=(n // tm,),
        in_specs=[pl.BlockSpec((tm, 1), lambda i: (i, 0)),
                  pl.BlockSpec((1, LANE), lambda i: (0, 0)),
                  pl.BlockSpec((1, LANE), lambda i: (0, 0))],
        out_specs=[pl.BlockSpec((tm, LANE), lambda i: (i, 0))] * 2,
        out_shape=[jax.ShapeDtypeStruct((n, LANE), F32)] * 2,
        compiler_params=_params("parallel"),
        name="rope_tables",
    )(positions.reshape(n, 1), inv_l, sgn)
    return cos.reshape(b, t, LANE), sin.reshape(b, t, LANE)


def _ret_kernel(lg_ref, x_ref, cos_ref, sin_ref, lgl_ref, gn_ref, avg_ref, *rest, C, fwd):
    if fwd:
        yo_ref, o_ref, s_ref, d_ref = rest
    else:
        o_ref, s_ref, d_ref = rest
    W = GROUP_W

    @pl.when(pl.program_id(1) == 0)
    def _init():
        s_ref[...] = jnp.zeros_like(s_ref)
        i, j = _iota((C, C), 0), _iota((C, C), 1)
        rel = ((i - j) if fwd else (j - i)).astype(F32)
        for h in range(N_HEADS):
            d_ref[h] = jnp.where(rel >= 0, jnp.exp(lg_ref[h] * jnp.maximum(rel, 0.0)), 0.0)

    x = x_ref[...]
    q, k, v, g = x[:, 0:W], x[:, W:2 * W], x[:, 2 * W:3 * W], x[:, 3 * W:4 * W]
    cos = jnp.concatenate([cos_ref[...]] * (W // LANE), axis=1)
    sin = jnp.concatenate([sin_ref[...]] * (W // LANE), axis=1)
    first_half = (_iota((1, W), 1) % HEAD_DIM) < (HEAD_DIM // 2)

    def rot(z):
        swapped = jnp.where(first_half, pltpu.roll(z, W - HEAD_DIM // 2, 1), pltpu.roll(z, HEAD_DIM // 2, 1))
        return z * cos + swapped * sin

    q = rot(q)
    k = rot(k) * (HEAD_DIM ** -0.5)
    lgl = lgl_ref[...]
    idx = _iota((C, 1), 0).astype(F32)
    if fwd:
        zeta = jnp.exp(lgl * (C - 1.0 - idx))
        xi = jnp.exp(lgl * (idx + 1.0))
    else:
        zeta = jnp.exp(lgl * idx)
        xi = jnp.exp(lgl * (C - idx))
    cd = jnp.exp(lgl * float(C))

    kb, vb = _bf(k), _bf(v)
    y = jnp.zeros((C, W), F32)
    for h in range(N_HEADS):
        mh = _head_mask(h, W, HEAD_DIM)
        s = _dot_nt(q * mh, kb)
        y = y + _dot(s * d_ref[h], vb) * mh
    state = s_ref[...]
    y = y + _dot(q * xi, state)
    bmask = ((_iota((W, W), 0) // HEAD_DIM) == (_iota((W, W), 1) // HEAD_DIM)).astype(F32)
    s_ref[...] = state * cd + _dot_tn(k * zeta, vb) * bmask

    if fwd:
        yt = y + yo_ref[...]
        avg = avg_ref[...]
        yc = yt - _head_stats(yt, avg)
        var = _head_stats(yc * yc, avg)
        o_ref[...] = yc * lax.rsqrt(var + NORM_EPS) * gn_ref[...] * (g * _sigmoid(g))
    else:
        o_ref[...] = y


def _retention(proj3, cos, sin, log_decay, gn_w, avg, tl):
    b, t, _ = proj3.shape
    c = tl["c_ret"]
    nc = t // c

    def call(fwd, lg, extra):
        def cm(i):
            return i if fwd else nc - 1 - i

        in_specs = [
            pl.BlockSpec(memory_space=pltpu.SMEM),
            pl.BlockSpec((None, c, 1024), lambda bi, i: (bi, cm(i), COL_RET // 1024)),
            pl.BlockSpec((None, c, LANE), lambda bi, i: (bi, cm(i), 0)),
            pl.BlockSpec((None, c, LANE), lambda bi, i: (bi, cm(i), 0)),
            pl.BlockSpec((1, GROUP_W), lambda bi, i: (0, 0)),
            pl.BlockSpec((1, GROUP_W), lambda bi, i: (0, 0)),
            pl.BlockSpec((GROUP_W, GROUP_W), lambda bi, i: (0, 0)),
        ]
        args = [lg, proj3, cos, sin, jnp.repeat(lg, HEAD_DIM)[None, :], gn_w[None, :], avg]
        if fwd:
            in_specs.append(pl.BlockSpec((None, c, GROUP_W), lambda bi, i: (bi, cm(i), 0)))
            args.append(extra)
        return pl.pallas_call(
            functools.partial(_ret_kernel, C=c, fwd=fwd),
            grid=(b, nc),
            in_specs=in_specs,
            out_specs=pl.BlockSpec((None, c, GROUP_W), lambda bi, i: (bi, cm(i), 0)),
            out_shape=jax.ShapeDtypeStruct((b, t, GROUP_W), F32),
            scratch_shapes=[pltpu.VMEM((GROUP_W, GROUP_W), F32), pltpu.VMEM((N_HEADS, c, c), F32)],
            compiler_params=_params("parallel", "arbitrary"),
            name="retention_fwd" if fwd else "retention_bwd",
        )(*args)

    y_b = call(False, log_decay[1], None)
    return call(True, log_decay[0], y_b)


def _gla_kernel(x_ref, aup_ref, ab_ref, gn_ref, avg_ref, *rest, TB, fwd):
    if fwd:
        yo_ref, o_ref, st_ref = rest
    else:
        o_ref, st_ref = rest
    W, KD, C = GROUP_W, GLA_KDIM, CHUNK

    @pl.when(pl.program_id(1) == 0)
    def _init():
        st_ref[...] = jnp.zeros_like(st_ref)

    x = x_ref[...]
    q = x[:, 0:KD] * (GLA_HEAD_K ** -0.5)
    k = x[:, KD:2 * KD]
    v = x[:, 2 * KD:2 * KD + W]
    og = x[:, 2 * KD + W:2 * KD + 2 * W]
    xa = x[:, 2 * KD + 2 * W:2 * KD + 2 * W + LANE]
    z = _dot_hp(xa, aup_ref[...]) + ab_ref[...]
    la = -_softplus(-z) * (1.0 / GLA_TAU)
    tri = jnp.where(_tri_mask(C, fwd, False), 1.0, 0.0).astype(BF16)
    row_t = _iota((C, W), 0)
    lane_s = _iota((C, W), 1) % C
    m_incl = (lane_s <= row_t) if fwd else (lane_s >= row_t)
    bmask_t = ((_iota((W, KD), 0) // HEAD_DIM) == (_iota((W, KD), 1) // GLA_HEAD_K)).astype(F32)

    n_chunks = TB // C
    order = range(n_chunks) if fwd else range(n_chunks - 1, -1, -1)
    sls = [slice(c * C, (c + 1) * C) for c in order]
    bcs = [_dot_exact_lhs(tri, la[sl]) for sl in sls]
    bls = [bc[C - 1:C] if fwd else bc[0:1] for bc in bcs]
    qds = [q[sl] * jnp.exp(bc) for sl, bc in zip(sls, bcs)]
    kds = [_bf(k[sl] * jnp.exp(-bc)) for sl, bc in zip(sls, bcs)]
    khs = [k[sl] * jnp.exp(bl - bc) for sl, bc, bl in zip(sls, bcs, bls)]
    vbs = [_bf(v[sl]) for sl in sls]
    scs = [jnp.where(m_incl, _dot_nt(qd, _block_diag_bf(kd)), 0.0) for qd, kd in zip(qds, kds)]
    yis = [_dot(sc, _block_diag_bf(vb)) for sc, vb in zip(scs, vbs)]
    upds = [_dot_tn(vb, kh) * bmask_t for vb, kh in zip(vbs, khs)]
    st = st_ref[...]
    for j, sl in enumerate(sls):
        o_ref[sl, :] = yis[j] + _dot_nt(qds[j], st)
        st = st * jnp.exp(bls[j]) + upds[j]
    st_ref[...] = st

    if fwd:
        yt = o_ref[...] + yo_ref[...]
        var = _head_stats(yt * yt, avg_ref[...])
        o_ref[...] = yt * lax.rsqrt(var + NORM_EPS) * gn_ref[...] * (og * _sigmoid(og))


def _gla(proj3, alpha_up, alpha_b, gn_w, avg, tl):
    b, t, _ = proj3.shape
    tb = tl["tb_gla"]
    nb = t // tb

    def call(fwd, d, extra):
        def bm(i):
            return i if fwd else nb - 1 - i

        aup = jnp.zeros((LANE, GLA_KDIM), F32).at[:alpha_up.shape[1]].set(alpha_up[d])
        in_specs = [
            pl.BlockSpec((None, tb, 1024), lambda bi, i: (bi, bm(i), COL_GLA // 1024)),
            pl.BlockSpec((LANE, GLA_KDIM), lambda bi, i: (0, 0)),
            pl.BlockSpec((1, GLA_KDIM), lambda bi, i: (0, 0)),
            pl.BlockSpec((1, GROUP_W), lambda bi, i: (0, 0)),
            pl.BlockSpec((GROUP_W, GROUP_W), lambda bi, i: (0, 0)),
        ]
        args = [proj3, aup, alpha_b[d][None, :], gn_w[None, :], avg]
        if fwd:
            in_specs.append(pl.BlockSpec((None, tb, GROUP_W), lambda bi, i: (bi, bm(i), 0)))
            args.append(extra)
        return pl.pallas_call(
            functools.partial(_gla_kernel, TB=tb, fwd=fwd),
            grid=(b, nb),
            in_specs=in_specs,
            out_specs=pl.BlockSpec((None, tb, GROUP_W), lambda bi, i: (bi, bm(i), 0)),
            out_shape=jax.ShapeDtypeStruct((b, t, GROUP_W), F32),
            scratch_shapes=[pltpu.VMEM((GROUP_W, GLA_KDIM), F32)],
            compiler_params=_params("parallel", "arbitrary"),
            name="gla_fwd" if fwd else "gla_bwd",
        )(*args)

    y_b = call(False, 1, None)
    return call(True, 0, y_b)


def _lru_kernel(x_ref, hp_ref, hn_ref, cw_ref, cb_ref, gw_ref, gb_ref, lam_ref, *rest, TB, NB, fwd):
    if fwd:
        ho_ref, o_ref, h_ref = rest
    else:
        o_ref, h_ref = rest
    W = GROUP_W
    i = pl.program_id(1)
    blk = i if fwd else NB - 1 - i

    @pl.when(i == 0)
    def _init():
        h_ref[...] = jnp.zeros_like(h_ref)

    x = x_ref[...]
    xr, gate = x[:, 0:W], x[:, W:2 * W]
    has_prev = (blk > 0).astype(F32)
    has_next = (blk < NB - 1).astype(F32)
    prev = hp_ref[...][:, 0:W] * has_prev
    nxt = hn_ref[...][:, 0:W] * has_next
    row = _iota((TB, 1), 0)
    x_m1 = jnp.where(row == 0, prev[7:8], pltpu.roll(xr, 1, 0))
    x_m2 = jnp.where(row == 0, prev[6:7], jnp.where(row == 1, prev[7:8], pltpu.roll(xr, 2, 0)))
    x_p1 = jnp.where(row == TB - 1, nxt[0:1], pltpu.roll(xr, TB - 1, 0))
    cw = cw_ref[...]
    xc = x_m2 * cw[0:1] + x_m1 * cw[1:2] + xr * cw[2:3] + x_p1 * cw[3:4] + cb_ref[...]

    gx = _dot(xc, gw_ref[...]) + gb_ref[...]
    rec = _sigmoid(gx[:, 0:W])
    ing = _sigmoid(gx[:, W:2 * W])
    log_a = -RGLRU_C * rec * _softplus(-lam_ref[...])
    a = jnp.exp(log_a)
    bx = jnp.sqrt(-jnp.tanh(log_a) * (a * a + 1.0)) * (ing * xc)

    s = 1
    while s < TB:
        if fwd:
            keep = row >= s
            a_sh = jnp.where(keep, pltpu.roll(a, s, 0), 1.0)
            b_sh = jnp.where(keep, pltpu.roll(bx, s, 0), 0.0)
        else:
            keep = row < TB - s
            a_sh = jnp.where(keep, pltpu.roll(a, TB - s, 0), 1.0)
            b_sh = jnp.where(keep, pltpu.roll(bx, TB - s, 0), 0.0)
        bx = a * b_sh + bx
        a = a * a_sh
        s *= 2
    h = a * h_ref[...] + bx
    h_ref[...] = h[TB - 1:TB] if fwd else h[0:1]

    if fwd:
        o_ref[...] = (h + ho_ref[...]) * jax.nn.gelu(gate, approximate=True)
    else:
        o_ref[...] = h


def _block_diag_heads(w):
    h, d, e = w.shape
    eye = jnp.eye(h, dtype=w.dtype)
    return jnp.einsum('hde,hg->hdge', w, eye).reshape(h * d, h * e)


def _rglru(proj3, conv_w, conv_b, gate_w, gate_b, lam, tl):
    b, t, _ = proj3.shape
    tb = tl["tb_lru"]
    nb = t // tb
    r8 = tb // SUBLANE
    last8 = t // SUBLANE - 1
    colb = COL_LRU // 512

    def call(fwd, d, extra):
        def bm(i):
            return i if fwd else nb - 1 - i

        gw = jnp.concatenate([_block_diag_heads(gate_w[d, 0]), _block_diag_heads(gate_w[d, 1])], axis=1)
        gb = jnp.concatenate([gate_b[d, 0], gate_b[d, 1]])[None, :]
        in_specs = [
            pl.BlockSpec((None, tb, 512), lambda bi, i: (bi, bm(i), colb)),
            pl.BlockSpec((None, SUBLANE, 512), lambda bi, i: (bi, jnp.maximum(bm(i) * r8 - 1, 0), colb)),
            pl.BlockSpec((None, SUBLANE, 512), lambda bi, i: (bi, jnp.minimum((bm(i) + 1) * r8, last8), colb)),
            pl.BlockSpec((4, GROUP_W), lambda bi, i: (0, 0)),
            pl.BlockSpec((1, GROUP_W), lambda bi, i: (0, 0)),
            pl.BlockSpec((GROUP_W, 2 * GROUP_W), lambda bi, i: (0, 0)),
            pl.BlockSpec((1, 2 * GROUP_W), lambda bi, i: (0, 0)),
            pl.BlockSpec((1, GROUP_W), lambda bi, i: (0, 0)),
        ]
        args = [proj3, proj3, proj3, conv_w, conv_b[None, :], _bf(gw), gb, lam[d][None, :]]
        if fwd:
            in_specs.append(pl.BlockSpec((None, tb, GROUP_W), lambda bi, i: (bi, bm(i), 0)))
            args.append(extra)
        return pl.pallas_call(
            functools.partial(_lru_kernel, TB=tb, NB=nb, fwd=fwd),
            grid=(b, nb),
            in_specs=in_specs,
            out_specs=pl.BlockSpec((None, tb, GROUP_W), lambda bi, i: (bi, bm(i), 0)),
            out_shape=jax.ShapeDtypeStruct((b, t, GROUP_W), F32),
            scratch_shapes=[pltpu.VMEM((1, GROUP_W), F32)],
            compiler_params=_params("parallel", "arbitrary"),
            name="rglru_fwd" if fwd else "rglru_bwd",
        )(*args)

    h_b = call(False, 1, None)
    return call(True, 0, h_b)


def _rwkv_kernel(x_ref, halo_ref, mu_ref, lora_ref, w0_ref, a0_ref, kk_ref, ka_ref, rk_ref, ones_ref,
                 *rest, TB, NB, fwd):
    if fwd:
        gup_ref, gn_ref, avg_ref, yo_ref, bo_ref, o_ref, s_ref = rest
    else:
        y_ref, b_ref, s_ref = rest
    W, C = GROUP_W, CHUNK
    i = pl.program_id(1)
    blk = i if fwd else NB - 1 - i

    @pl.when(i == 0)
    def _init():
        s_ref[...] = jnp.zeros_like(s_ref)

    x = x_ref[...]
    NS = 3 * W + LANE
    zs = x[:, 0:NS]
    halo = halo_ref[...][:, 0:NS]
    if fwd:
        edge = halo[SUBLANE - 1:SUBLANE] * (blk > 0).astype(F32)
    else:
        edge = halo[0:1] * (blk < NB - 1).astype(F32)
    zs = zs + mu_ref[...] * (_shift_rows(zs, edge, fwd) - zs)
    r, k, v, xwa = zs[:, 0:W], zs[:, W:2 * W], zs[:, 2 * W:3 * W], zs[:, 3 * W:NS]
    lane_l = _iota((1, LANE), 1)
    xwa = jnp.where(lane_l < LANE // 2, jnp.tanh(xwa), xwa)
    lo = _dot_hp(xwa, lora_ref[...])
    w_log = -_softplus(-(w0_ref[...] + lo[:, 0:W])) - 0.5
    lw = -jnp.exp(w_log)
    a = _sigmoid(a0_ref[...] + lo[:, W:2 * W])
    ones_bd = ones_ref[...]
    kk = k * kk_ref[...]
    kk = kk / jnp.maximum(jnp.sqrt(_dot_exact_rhs(kk * kk, ones_bd)), 1e-12)
    k = k * (1.0 + (a - 1.0) * ka_ref[...])
    bonus = _dot_exact_rhs(r * k * rk_ref[...], ones_bd) * v
    am = -kk
    bm = kk * a

    tri = jnp.where(_tri_mask(C, fwd, False), 1.0, 0.0).astype(BF16)
    row_t = _iota((C, W), 0)
    lane_s = _iota((C, W), 1) % C
    m_strict = (lane_s < row_t) if fwd else (lane_s > row_t)
    m_incl = (lane_s <= row_t) if fwd else (lane_s >= row_t)
    eye_c = (lane_s == row_t).astype(F32)
    bmask = (_iota((W, W), 0) // HEAD_DIM) == (_iota((W, W), 1) // HEAD_DIM)

    def bd(z):
        return _block_diag_bf(_bf(z))

    n_chunks = TB // C
    order = range(n_chunks) if fwd else range(n_chunks - 1, -1, -1)
    sls = [slice(c * C, (c + 1) * C) for c in order]
    cums = [_dot_exact_lhs(tri, lw[sl]) for sl in sls]
    tots = [cum[C - 1:C] if fwd else cum[0:1] for cum in cums]
    e_negs = [jnp.exp(-cum) for cum in cums]
    e_ends = [jnp.exp(tot - cum) for tot, cum in zip(tots, cums)]
    a_ts = [am[sl] * jnp.exp(cum - lw[sl]) for sl, cum in zip(sls, cums)]
    r_ts = [r[sl] * jnp.exp(cum) for sl, cum in zip(sls, cums)]
    gs = [_dot_nt(jnp.concatenate([a_t, r_t], axis=0),
                  jnp.concatenate([bd(bm[sl] * e_neg), bd(k[sl] * e_neg)], axis=0))
          for sl, a_t, r_t, e_neg in zip(sls, a_ts, r_ts, e_negs)]
    a_abs = [jnp.where(m_strict, g[0:C, 0:W], 0.0) for g in gs]
    a_aks = [jnp.where(m_strict, g[0:C, W:2 * W], 0.0) for g in gs]
    a_rbs = [jnp.where(m_incl, g[C:2 * C, 0:W], 0.0) for g in gs]
    a_rks = [jnp.where(m_incl, g[C:2 * C, W:2 * W], 0.0) for g in gs]

    tinvs = [eye_c + a_ab for a_ab in a_abs]
    ps = [_dot(a_ab, bd(a_ab)) for a_ab in a_abs]
    avs = [_dot(jnp.concatenate([a_ak, a_rk], axis=0), bd(v[sl])) for a_ak, a_rk, sl in zip(a_aks, a_rks, sls)]
    n_terms = 2
    while 2 * n_terms < C:
        xps = [_dot(p, jnp.concatenate([bd(tinv), bd(p)], axis=1)) for p, tinv in zip(ps, tinvs)]
        tinvs = [tinv + xp[:, 0:W] for tinv, xp in zip(tinvs, xps)]
        ps = [xp[:, W:2 * W] for xp in xps]
        n_terms *= 2
    tinvs = [tinv + _dot(p, bd(tinv)) for p, tinv in zip(ps, tinvs)]

    w12s = [_dot(tinv, jnp.concatenate([bd(a_t), bd(av[0:C])], axis=1))
            for tinv, a_t, av in zip(tinvs, a_ts, avs)]
    qys = [_dot(a_rb, jnp.concatenate([bd(w12[:, 0:W]), bd(w12[:, W:2 * W])], axis=1))
           for a_rb, w12 in zip(a_rbs, w12s)]

    for j, sl in enumerate(sls):
        w1, w2 = w12s[j][:, 0:W], w12s[j][:, W:2 * W]
        qp = r_ts[j] + qys[j][:, 0:W]
        y0 = avs[j][C:2 * C] + qys[j][:, W:2 * W]
        state = s_ref[...]
        uy = _dot_nt(jnp.concatenate([w1, qp], axis=0), state)
        u = uy[0:C] + w2
        y = uy[C:2 * C] + y0
        upd = _dot_tn(jnp.concatenate([v[sl], u], axis=0),
                      jnp.concatenate([k[sl] * e_ends[j], bm[sl] * e_ends[j]], axis=0))
        s_ref[...] = state * jnp.exp(tots[j]) + jnp.where(bmask, upd, 0.0)
        if fwd:
            o_ref[sl, :] = y
        else:
            y_ref[sl, :] = y

    if fwd:
        yt = o_ref[...] + yo_ref[...]
        avg = avg_ref[...]
        yc = yt - _head_stats(yt, avg)
        var = _head_stats(yc * yc, avg)
        o = yc * lax.rsqrt(var + RWKV_GN_EPS) * gn_ref[...] + bonus + bo_ref[...]
        xg = x[:, NS:NS + LANE]
        o_ref[...] = o * _dot_hp(_sigmoid(xg), gup_ref[...])
    else:
        b_ref[...] = bonus


def _rwkv7(proj3, mu_rkv, mu_w, mu_a, w0, w_up, a0, a_up, g_up, k_k, k_a, r_k, gn_w, avg, ones_bd, tl):
    b, t, _ = proj3.shape
    tb = tl["tb_rwkv"]
    nb = t // tb
    r8 = tb // SUBLANE
    last8 = t // SUBLANE - 1
    colb = COL_RWKV // 1024
    W = GROUP_W
    mu = jnp.concatenate([mu_rkv.reshape(-1), mu_w, mu_a])[None, :]
    nl = w_up.shape[1]

    def call(fwd, d, extra):
        def bm(i):
            return i if fwd else nb - 1 - i

        lora = jnp.zeros((LANE, 2 * W), F32).at[0:nl, 0:W].set(w_up[d]).at[nl:2 * nl, W:2 * W].set(a_up[d])
        if fwd:
            halo_map = lambda bi, i: (bi, jnp.maximum(bm(i) * r8 - 1, 0), colb)
        else:
            halo_map = lambda bi, i: (bi, jnp.minimum((bm(i) + 1) * r8, last8), colb)
        row = lambda n: pl.BlockSpec((1, n), lambda bi, i: (0, 0))
        in_specs = [
            pl.BlockSpec((None, tb, 1024), lambda bi, i: (bi, bm(i), colb)),
            pl.BlockSpec((None, SUBLANE, 1024), halo_map),
            row(3 * W + LANE),
            pl.BlockSpec((LANE, 2 * W), lambda bi, i: (0, 0)),
            row(W), row(W), row(W), row(W), row(W),
            pl.BlockSpec((W, W), lambda bi, i: (0, 0)),
        ]
        args = [proj3, proj3, mu, lora, w0[d][None, :], a0[d][None, :], k_k[None, :], k_a[None, :],
                r_k.reshape(1, W), ones_bd]
        blk = pl.BlockSpec((None, tb, W), lambda bi, i: (bi, bm(i), 0))
        if fwd:
            in_specs += [pl.BlockSpec((LANE, W), lambda bi, i: (0, 0)), row(W),
                         pl.BlockSpec((W, W), lambda bi, i: (0, 0)), blk, blk]
            args += [g_up, gn_w[None, :], avg, extra[0], extra[1]]
            out_specs = blk
            out_shape = jax.ShapeDtypeStruct((b, t, W), F32)
        else:
            out_specs = [blk, blk]
            out_shape = [jax.ShapeDtypeStruct((b, t, W), F32)] * 2
        return pl.pallas_call(
            functools.partial(_rwkv_kernel, TB=tb, NB=nb, fwd=fwd),
            grid=(b, nb),
            in_specs=in_specs,
            out_specs=out_specs,
            out_shape=out_shape,
            scratch_shapes=[pltpu.VMEM((W, W), F32)],
            compiler_params=_params("parallel", "arbitrary"),
            name="rwkv7_fwd" if fwd else "rwkv7_bwd",
        )(*args)

    y_b, bonus_b = call(False, 1, None)
    return call(True, 0, (y_b, bonus_b))


def _out_proj_kernel(x_ref, o1_ref, o2_ref, o3_ref, o4_ref, w_ref, g_ref, rw_ref, rb_ref, rwt_ref, rbt_ref,
                     xo_ref, xn_ref, aff_ref, afft_ref):
    W = GROUP_W
    w = w_ref[...]
    acc = x_ref[...]
    for gi, o_ref in enumerate((o1_ref, o2_ref, o3_ref, o4_ref)):
        acc = acc + jnp.dot(_bf(o_ref[...]), w[gi * W:(gi + 1) * W], preferred_element_type=F32)
    xo_ref[...] = acc
    ms = jnp.mean(acc * acc, axis=-1, keepdims=True)
    xn = acc * lax.rsqrt(ms + NORM_EPS) * g_ref[...]
    xn_ref[...] = _bf(xn)
    logits = _dot_hp(xn, rw_ref[...]) + rb_ref[...]
    logits = logits - jnp.max(logits, axis=-1, keepdims=True)
    e = jnp.exp(logits)
    aff_ref[...] = e / jnp.sum(e, axis=-1, keepdims=True)
    lt = _dot_hp_nt(rwt_ref[...], xn) + rbt_ref[...]
    lt = lt - jnp.max(lt, axis=0, keepdims=True)
    et = jnp.exp(lt)
    afft_ref[...] = et / jnp.sum(et, axis=0, keepdims=True)


def _out_proj(x2, outs, w_out_bf, gain, router_w, router_b, b, t, tl):
    n, d = x2.shape
    tm = tl["tm_out"]
    tpb = t // tm
    E = N_EXPERTS
    rw = jnp.zeros((d, LANE), F32).at[:, :E].set(router_w)
    rb = jnp.full((1, LANE), -1e30, F32).at[0, :E].set(router_b)
    full = lambda r, c: pl.BlockSpec((r, c), lambda i: (0, 0))
    tile = lambda c: pl.BlockSpec((tm, c), lambda i: (i, 0))
    return pl.pallas_call(
        _out_proj_kernel,
        grid=(n // tm,),
        in_specs=[tile(d)] + [tile(GROUP_W)] * 4 + [full(d, d), full(1, d), full(d, LANE), full(1, LANE),
                                                     full(E, d), full(E, 1)],
        out_specs=[tile(d), tile(d), tile(LANE),
                   pl.BlockSpec((None, E, tm), lambda i: (i // tpb, 0, i % tpb))],
        out_shape=[jax.ShapeDtypeStruct((n, d), F32), jax.ShapeDtypeStruct((n, d), BF16),
                   jax.ShapeDtypeStruct((n, LANE), F32), jax.ShapeDtypeStruct((b, E, t), F32)],
        compiler_params=_params("parallel"),
        name="out_proj_router",
    )(x2, *outs, w_out_bf, gain, rw, rb, router_w.T, router_b[:, None])


def _prefix_count(m, triu):
    e, t = m.shape
    nblk = t // LANE
    stacked = jnp.concatenate([m[:, j * LANE:(j + 1) * LANE] for j in range(nblk)], axis=0)
    incl = jnp.dot(_bf(stacked), triu, preferred_element_type=F32)
    pieces = []
    off = jnp.zeros((e, 1), F32)
    for j in range(nblk):
        blk = incl[j * e:(j + 1) * e]
        pieces.append(blk - stacked[j * e:(j + 1) * e] + off)
        off = off + blk[:, LANE - 1:LANE]
    return jnp.concatenate(pieces, axis=1)


def _select_kernel(afft_ref, rank_ref, rankc_ref, *, cap):
    aff = afft_ref[...]
    e, t = aff.shape
    bits = lax.bitcast_convert_type(aff, jnp.int32)

    def body(i, prefix):
        cand = prefix | jnp.left_shift(jnp.int32(1), 30 - i)
        cnt = jnp.sum((bits >= cand).astype(F32), axis=1, keepdims=True)
        return jnp.where(cnt >= cap, cand, prefix)

    thr = lax.fori_loop(0, 31, body, jnp.zeros((e, 1), jnp.int32))
    gt = (bits > thr).astype(F32)
    eq = (bits == thr).astype(F32)
    need = cap - jnp.sum(gt, axis=1, keepdims=True)
    triu = jnp.where(_tri_mask(LANE, False, False), 1.0, 0.0).astype(BF16)
    sel = gt + eq * (_prefix_count(eq, triu) < need).astype(F32)
    rank = jnp.where(sel > 0.5, _prefix_count(sel, triu), -1.0)
    rank_ref[...] = rank
    padded = jnp.concatenate([rank, jnp.full((LANE - e, t), -1.0, F32)], axis=0)
    rankc_ref[...] = padded.T


def _select(afft, cap):
    b, e, t = afft.shape
    return pl.pallas_call(
        functools.partial(_select_kernel, cap=cap),
        grid=(b,),
        in_specs=[pl.BlockSpec((None, e, t), lambda i: (i, 0, 0))],
        out_specs=[pl.BlockSpec((None, e, t), lambda i: (i, 0, 0)),
                   pl.BlockSpec((None, t, LANE), lambda i: (i, 0, 0))],
        out_shape=[jax.ShapeDtypeStruct((b, e, t), F32), jax.ShapeDtypeStruct((b, t, LANE), F32)],
        compiler_params=_params("parallel"),
        name="expert_choice_select",
    )(afft)


def _gather_kernel(rank_ref, xn_ref, xs_ref, *, cap, tk):
    e = pl.program_id(1)
    t = xn_ref.shape[0]
    r = rank_ref[pl.ds(e, 1), :]
    slot = _iota((cap, tk), 0).astype(F32)
    acc = jnp.zeros((cap, xn_ref.shape[1]), F32)
    for j in range(t // tk):
        onehot = jnp.where(r[:, j * tk:(j + 1) * tk] == slot, 1.0, 0.0).astype(BF16)
        acc = acc + jnp.dot(onehot, xn_ref[j * tk:(j + 1) * tk, :], preferred_element_type=F32)
    xs_ref[...] = _bf(acc)


def _gather(rank, xn3, cap, tl):
    b, e, t = rank.shape
    d = xn3.shape[2]
    return pl.pallas_call(
        functools.partial(_gather_kernel, cap=cap, tk=tl["tk_gather"]),
        grid=(b, e),
        in_specs=[pl.BlockSpec((None, e, t), lambda bi, ei: (bi, 0, 0)),
                  pl.BlockSpec((None, t, d), lambda bi, ei: (bi, 0, 0))],
        out_specs=pl.BlockSpec((None, None, cap, d), lambda bi, ei: (bi, ei, 0, 0)),
        out_shape=jax.ShapeDtypeStruct((b, e, cap, d), BF16),
        compiler_params=_params("parallel", "arbitrary"),
        name="moe_gather",
    )(rank, xn3)


def _ffn_kernel(xs_ref, wg_ref, wu_ref, wd_ref, o_ref, wgb_ref, wub_ref, wdb_ref):
    @pl.when(pl.program_id(1) == 0)
    def _():
        wgb_ref[...] = _bf(wg_ref[...])
        wub_ref[...] = _bf(wu_ref[...])
        wdb_ref[...] = _bf(wd_ref[...])

    xs = xs_ref[...]
    hg = jnp.dot(xs, wgb_ref[...], preferred_element_type=F32)
    hu = jnp.dot(xs, wub_ref[...], preferred_element_type=F32)
    hid = _bf(hg * _sigmoid(hg) * hu)
    o_ref[...] = _bf(jnp.dot(hid, wdb_ref[...], preferred_element_type=F32))


def _ffn(xs, w_gate, w_up, w_down, layer):
    b, e, cap, d = xs.shape
    f = w_gate.shape[3]
    wspec = lambda r, c: pl.BlockSpec((None, None, r, c), lambda ei, bi: (layer, ei, 0, 0))
    xspec = pl.BlockSpec((None, None, cap, d), lambda ei, bi: (bi, ei, 0, 0))
    return pl.pallas_call(
        _ffn_kernel,
        grid=(e, b),
        in_specs=[xspec, wspec(d, f), wspec(d, f), wspec(f, d)],
        out_specs=xspec,
        out_shape=jax.ShapeDtypeStruct((b, e, cap, d), BF16),
        scratch_shapes=[pltpu.VMEM((d, f), BF16), pltpu.VMEM((d, f), BF16), pltpu.VMEM((f, d), BF16)],
        compiler_params=_params("arbitrary", "arbitrary"),
        name="moe_ffn",
    )(xs, w_gate, w_up, w_down)


def _combine_kernel(x_ref, rankc_ref, aff_ref, o_ref, g_ref, out_ref, *, cap, final_norm):
    rankc = rankc_ref[...]
    aff = aff_ref[...]
    tm = x_ref.shape[0]
    slot = _iota((tm, cap), 1).astype(F32)
    acc = x_ref[...]
    for e in range(N_EXPERTS):
        pt = jnp.where(rankc[:, e:e + 1] == slot, aff[:, e:e + 1], 0.0)
        acc = acc + jnp.dot(_bf(pt), o_ref[e * cap:(e + 1) * cap, :], preferred_element_type=F32)
    if final_norm:
        ms = jnp.mean(acc * acc, axis=-1, keepdims=True)
        acc = acc * lax.rsqrt(ms + NORM_EPS) * g_ref[...]
    out_ref[...] = acc


def _combine(x3, rankc, aff3, o_flat, gain, cap, final_norm, tl):
    b, t, d = x3.shape
    tm = tl["tm_comb"]
    tile = lambda c: pl.BlockSpec((None, tm, c), lambda bi, i: (bi, i, 0))
    return pl.pallas_call(
        functools.partial(_combine_kernel, cap=cap, final_norm=final_norm),
        grid=(b, t // tm),
        in_specs=[tile(d), tile(LANE), tile(LANE),
                  pl.BlockSpec((None, N_EXPERTS * cap, d), lambda bi, i: (bi, 0, 0),
                               pipeline_mode=pl.Buffered(1)),
                  pl.BlockSpec((1, d), lambda bi, i: (0, 0))],
        out_specs=tile(d),
        out_shape=jax.ShapeDtypeStruct((b, t, d), F32),
        compiler_params=_params("parallel", "arbitrary"),
        name="moe_combine",
    )(x3, rankc, aff3, o_flat, gain)


def kernel(x, positions, norm_mix, w_in, w_out, ret_log_decay, ret_gn, rwkv_mu_rkv, rwkv_mu_w, rwkv_mu_a, rwkv_w0, rwkv_w_up, rwkv_a0, rwkv_a_up, rwkv_g_up, rwkv_k_k, rwkv_k_a, rwkv_r_k, rwkv_gn, lru_conv_w, lru_conv_b, lru_gate_w, lru_gate_b, lru_lambda, gla_alpha_up, gla_alpha_b, gla_gn, norm_ffn, router_w, router_b, exp_w_gate, exp_w_up, exp_w_down, norm_final):
    b, t, d = x.shape
    depth = w_in.shape[0]
    n = b * t
    tl = _tiles(t)
    cap = EC_CAPACITY_FACTOR * t // N_EXPERTS
    lane = jnp.arange(GROUP_W)
    same_head = (lane[:, None] // HEAD_DIM) == (lane[None, :] // HEAD_DIM)
    avg = jnp.where(same_head, 1.0 / HEAD_DIM, 0.0).astype(BF16)
    ones_bd = jnp.where(same_head, 1.0, 0.0).astype(BF16)
    cos, sin = _rope_tables(positions)

    x2 = x.reshape(n, d)
    for l in range(depth):
        w = w_in[l]
        split = 2048 + 512
        w_perm = jnp.concatenate(
            [w[:, :2048], w[:, split:], jnp.zeros((d, IN_COLS_PAD - w.shape[1]), F32), w[:, 2048:split]], axis=1)
        proj3 = _in_proj(x2, norm_mix[l][None, :], _bf(w_perm), tl).reshape(b, t, IN_COLS_PAD)

        o_ret = _retention(proj3, cos, sin, ret_log_decay[l], ret_gn[l], avg, tl)
        o_rwkv = _rwkv7(proj3, rwkv_mu_rkv[l], rwkv_mu_w[l], rwkv_mu_a[l], rwkv_w0[l], rwkv_w_up[l],
                        rwkv_a0[l], rwkv_a_up[l], rwkv_g_up[l], rwkv_k_k[l], rwkv_k_a[l], rwkv_r_k[l],
                        rwkv_gn[l], avg, ones_bd, tl)
        o_lru = _rglru(proj3, lru_conv_w[l], lru_conv_b[l], lru_gate_w[l], lru_gate_b[l], lru_lambda[l], tl)
        o_gla = _gla(proj3, gla_alpha_up[l], gla_alpha_b[l], gla_gn[l], avg, tl)
        outs = [o.reshape(n, GROUP_W) for o in (o_ret, o_rwkv, o_lru, o_gla)]

        x_mid, xn, aff, afft = _out_proj(x2, outs, _bf(w_out[l]), norm_ffn[l][None, :], router_w[l],
                                         router_b[l], b, t, tl)
        rank, rankc = _select(afft, cap)
        xs = _gather(rank, xn.reshape(b, t, d), cap, tl)
        o_exp = _ffn(xs, exp_w_gate, exp_w_up, exp_w_down, l)
        x3 = _combine(x_mid.reshape(b, t, d), rankc, aff.reshape(b, t, LANE),
                      o_exp.reshape(b, N_EXPERTS * cap, d), norm_final[None, :], cap, l == depth - 1, tl)
        x2 = x3.reshape(n, d)
    return x2.reshape(b, t, d)
```

```python
import functools

import jax
import jax.numpy as jnp
from jax import lax
from jax.experimental import pallas as pl
from jax.experimental.pallas import tpu as pltpu

F32 = jnp.float32
BF16 = jnp.bfloat16

D_MODEL = 1024
GROUP_W = 256
N_HEADS = 4
HEAD_DIM = 64
ROPE_BASE = 10000.0
RWKV_GN_EPS = 64e-5
RGLRU_C = 8.0
GLA_KDIM = 128
GLA_HEAD_K = 32
GLA_TAU = 16.0
N_EXPERTS = 16
EC_CAPACITY_FACTOR = 2
NORM_EPS = 1e-6

LANE = 128
SUBLANE = 8
CHUNK = 64
VMEM_LIMIT = 56 * 1024 * 1024

COL_RET, COL_RWKV, COL_GLA, COL_LRU = 0, 1024, 2048, 3072
IN_COLS_PAD = 3584


def _tiles(T):
    return dict(
        tm_proj=min(512, T), tn_proj=512,
        tb_ret=min(512, T), tb_gla=min(512, T), tb_lru=min(512, T), tb_rwkv=min(512, T),
        tm_out=min(512, T), tk_moe=min(256, T), sb_moe=256,
    )


def _params(*sem):
    return pltpu.CompilerParams(dimension_semantics=sem, vmem_limit_bytes=VMEM_LIMIT)


def _bf(x):
    return x.astype(BF16)


def _dot(a, b):
    return jnp.dot(_bf(a), _bf(b), preferred_element_type=F32)


def _dot_nt(a, b):
    return lax.dot_general(_bf(a), _bf(b), (((1,), (1,)), ((), ())), preferred_element_type=F32)


def _dot_tn(a, b):
    return lax.dot_general(_bf(a), _bf(b), (((0,), (0,)), ((), ())), preferred_element_type=F32)


def _split2(a):
    hi = _bf(a)
    lo = _bf(a - hi.astype(F32))
    return hi, lo


def _split3(a):
    hi = _bf(a)
    r = a - hi.astype(F32)
    mid = _bf(r)
    lo = _bf(r - mid.astype(F32))
    return hi, mid, lo


def _dot_exact_lhs(m, a):
    hi, mid, lo = _split3(a)
    return (jnp.dot(m, hi, preferred_element_type=F32) + jnp.dot(m, mid, preferred_element_type=F32)
            + jnp.dot(m, lo, preferred_element_type=F32))


def _dot_exact_rhs(a, m):
    hi, lo = _split2(a)
    return jnp.dot(hi, m, preferred_element_type=F32) + jnp.dot(lo, m, preferred_element_type=F32)


def _dot_hp(a, b):
    ah, al = _split2(a)
    bh, bl = _split2(b)
    return (jnp.dot(ah, bh, preferred_element_type=F32) + jnp.dot(ah, bl, preferred_element_type=F32)
            + jnp.dot(al, bh, preferred_element_type=F32))


def _dot_hp_nt(a, b):
    ah, al = _split2(a)
    bh, bl = _split2(b)
    dn = (((1,), (1,)), ((), ()))
    return (lax.dot_general(ah, bh, dn, preferred_element_type=F32)
            + lax.dot_general(ah, bl, dn, preferred_element_type=F32)
            + lax.dot_general(al, bh, dn, preferred_element_type=F32))


def _sigmoid(x):
    return 1.0 / (1.0 + jnp.exp(-x))


def _softplus(x):
    return jnp.maximum(x, 0.0) + jnp.log(1.0 + jnp.exp(-jnp.abs(x)))


def _iota(shape, dim):
    return lax.broadcasted_iota(jnp.int32, shape, dim)


def _head_mask(h, width, hd):
    return (_iota((1, width), 1) // hd == h).astype(F32)


def _stack_heads(z, hd):
    w = z.shape[1]
    return jnp.concatenate([z * _head_mask(h, w, hd) for h in range(N_HEADS)], axis=0)


def _block_diag_bf(z):
    w = z.shape[1]
    lane_head = _iota((1, w), 1) // (w // N_HEADS)
    zero = jnp.zeros_like(z)
    return jnp.concatenate([jnp.where(lane_head == h, z, zero) for h in range(N_HEADS)], axis=0)


def _unstack_sum(z, c):
    return z[0:c] + z[c:2 * c] + z[2 * c:3 * c] + z[3 * c:4 * c]


def _tri_mask(n, lower, strict):
    i, j = _iota((n, n), 0), _iota((n, n), 1)
    if lower:
        return (j < i) if strict else (j <= i)
    return (j > i) if strict else (j >= i)


def _block_tri_mask(c, lower, strict):
    n = N_HEADS * c
    i, j = _iota((n, n), 0), _iota((n, n), 1)
    same = (i // c) == (j // c)
    ii, jj = i % c, j % c
    if lower:
        t = (jj < ii) if strict else (jj <= ii)
    else:
        t = (jj > ii) if strict else (jj >= ii)
    return same & t


def _head_stats(y, avg):
    return _dot_exact_rhs(y, avg)


def _shift_rows(z, edge, fwd):
    n = z.shape[0]
    row = _iota((n, 1), 0)
    if fwd:
        return jnp.where(row == 0, edge, pltpu.roll(z, 1, 0))
    return jnp.where(row == n - 1, edge, pltpu.roll(z, n - 1, 0))


def _in_proj_kernel(x_ref, g_ref, w_ref, o_ref, *, tn):
    x = x_ref[...]
    ms = jnp.mean(x * x, axis=-1, keepdims=True)
    xn = _bf(x * lax.rsqrt(ms + NORM_EPS) * g_ref[...])
    for c in range(o_ref.shape[1] // tn):
        o_ref[:, c * tn:(c + 1) * tn] = jnp.dot(xn, w_ref[:, c * tn:(c + 1) * tn], preferred_element_type=F32)


def _in_proj(x2, gain, w_bf, tl):
    n, d = x2.shape
    tm, tn = tl["tm_proj"], tl["tn_proj"]
    nc = w_bf.shape[1]
    return pl.pallas_call(
        functools.partial(_in_proj_kernel, tn=tn),
        grid=(n // tm,),
        in_specs=[pl.BlockSpec((tm, d), lambda i: (i, 0)),
                  pl.BlockSpec((1, d), lambda i: (0, 0)),
                  pl.BlockSpec((d, nc), lambda i: (0, 0), pipeline_mode=pl.Buffered(1))],
        out_specs=pl.BlockSpec((tm, nc), lambda i: (i, 0)),
        out_shape=jax.ShapeDtypeStruct((n, nc), F32),
        compiler_params=_params("parallel"),
        name="in_proj",
    )(x2, gain, w_bf)


def _rope_kernel(pos_ref, inv_ref, sgn_ref, cos_ref, sin_ref):
    ang = pos_ref[...].astype(F32) * inv_ref[...]
    cos_ref[...] = jnp.cos(ang)
    sin_ref[...] = jnp.sin(ang) * sgn_ref[...]


def _rope_tables(positions):
    b, t = positions.shape
    n = b * t
    tm = min(1024, n)
    lane = jnp.arange(LANE)
    inv = jnp.power(ROPE_BASE, -jnp.arange(0, HEAD_DIM, 2, dtype=F32) / HEAD_DIM)
    inv_l = inv[lane % (HEAD_DIM // 2)][None, :]
    sgn = jnp.where(lane % HEAD_DIM < HEAD_DIM // 2, -1.0, 1.0).astype(F32)[None, :]
    cos, sin = pl.pallas_call(
        _rope_kernel,
        grid=(n // tm,),
        in_specs=[pl.BlockSpec((tm, 1), lambda i: (i, 0)),
                  pl.BlockSpec((1, LANE), lambda i: (0, 0)),
                  pl.BlockSpec((1, LANE), lambda i: (0, 0))],
        out_specs=[pl.BlockSpec((tm, LANE), lambda i: (i, 0))] * 2,
        out_shape=[jax.ShapeDtypeStruct((n, LANE), F32)] * 2,
        compiler_params=_params("parallel"),
        name="rope_tables",
    )(positions.reshape(n, 1), inv_l, sgn)
    return cos.reshape(b, t, LANE), sin.reshape(b, t, LANE)


def _ret_kernel(x_ref, cos_ref, sin_ref, lgl_ref, gn_ref, avg_ref, *rest, TB, fwd):
    if fwd:
        yo_ref, o_ref, s_ref = rest
    else:
        o_ref, s_ref = rest
    W, C = GROUP_W, CHUNK

    @pl.when(pl.program_id(1) == 0)
    def _init():
        s_ref[...] = jnp.zeros_like(s_ref)

    x = x_ref[...]
    q, k, v, g = x[:, 0:W], x[:, W:2 * W], x[:, 2 * W:3 * W], x[:, 3 * W:4 * W]
    cos = jnp.concatenate([cos_ref[...]] * (W // LANE), axis=1)
    sin = jnp.concatenate([sin_ref[...]] * (W // LANE), axis=1)
    first_half = (_iota((1, W), 1) % HEAD_DIM) < (HEAD_DIM // 2)

    def rot(z):
        swapped = jnp.where(first_half, pltpu.roll(z, W - HEAD_DIM // 2, 1), pltpu.roll(z, HEAD_DIM // 2, 1))
        return z * cos + swapped * sin

    q = rot(q)
    k = rot(k) * (HEAD_DIM ** -0.5)
    lgl = lgl_ref[...]
    row_t = _iota((C, W), 0)
    lane_s = _iota((C, W), 1) % C
    rel = ((row_t - lane_s) if fwd else (lane_s - row_t)).astype(F32)
    dmask = jnp.where(rel >= 0, jnp.exp(lgl * jnp.maximum(rel, 0.0)), 0.0)
    idx = _iota((C, 1), 0).astype(F32)
    if fwd:
        zeta = jnp.exp(lgl * (C - 1.0 - idx))
        xi = jnp.exp(lgl * (idx + 1.0))
    else:
        zeta = jnp.exp(lgl * idx)
        xi = jnp.exp(lgl * (C - idx))
    cd = jnp.exp(lgl * float(C))
    bmask = ((_iota((W, W), 0) // HEAD_DIM) == (_iota((W, W), 1) // HEAD_DIM)).astype(F32)

    n_chunks = TB // C
    order = range(n_chunks) if fwd else range(n_chunks - 1, -1, -1)
    sls = [slice(c * C, (c + 1) * C) for c in order]
    kbs = [_bf(k[sl]) for sl in sls]
    vbs = [_bf(v[sl]) for sl in sls]
    scs = [_dot_nt(q[sl], _block_diag_bf(kb)) * dmask for sl, kb in zip(sls, kbs)]
    yis = [_dot(sc, _block_diag_bf(vb)) for sc, vb in zip(scs, vbs)]
    upds = [_dot_tn(k[sl] * zeta, vb) * bmask for sl, vb in zip(sls, vbs)]
    state = s_ref[...]
    for j, sl in enumerate(sls):
        o_ref[sl, :] = yis[j] + _dot(q[sl] * xi, state)
        state = state * cd + upds[j]
    s_ref[...] = state

    if fwd:
        yt = o_ref[...] + yo_ref[...]
        avg = avg_ref[...]
        yc = yt - _head_stats(yt, avg)
        var = _head_stats(yc * yc, avg)
        o_ref[...] = yc * lax.rsqrt(var + NORM_EPS) * gn_ref[...] * (g * _sigmoid(g))


def _retention(proj3, cos, sin, log_decay, gn_w, avg, tl):
    b, t, _ = proj3.shape
    tb = tl["tb_ret"]
    nb = t // tb

    def call(fwd, lg, extra):
        def cm(i):
            return i if fwd else nb - 1 - i

        in_specs = [
            pl.BlockSpec((None, tb, 1024), lambda bi, i: (bi, cm(i), COL_RET // 1024)),
            pl.BlockSpec((None, tb, LANE), lambda bi, i: (bi, cm(i), 0)),
            pl.BlockSpec((None, tb, LANE), lambda bi, i: (bi, cm(i), 0)),
            pl.BlockSpec((1, GROUP_W), lambda bi, i: (0, 0)),
            pl.BlockSpec((1, GROUP_W), lambda bi, i: (0, 0)),
            pl.BlockSpec((GROUP_W, GROUP_W), lambda bi, i: (0, 0)),
        ]
        args = [proj3, cos, sin, jnp.repeat(lg, HEAD_DIM)[None, :], gn_w[None, :], avg]
        if fwd:
            in_specs.append(pl.BlockSpec((None, tb, GROUP_W), lambda bi, i: (bi, cm(i), 0)))
            args.append(extra)
        return pl.pallas_call(
            functools.partial(_ret_kernel, TB=tb, fwd=fwd),
            grid=(b, nb),
            in_specs=in_specs,
            out_specs=pl.BlockSpec((None, tb, GROUP_W), lambda bi, i: (bi, cm(i), 0)),
            out_shape=jax.ShapeDtypeStruct((b, t, GROUP_W), F32),
            scratch_shapes=[pltpu.VMEM((GROUP_W, GROUP_W), F32)],
            compiler_params=_params("parallel", "arbitrary"),
            name="retention_fwd" if fwd else "retention_bwd",
        )(*args)

    y_b = call(False, log_decay[1], None)
    return call(True, log_decay[0], y_b)


def _gla_kernel(x_ref, aup_ref, ab_ref, gn_ref, avg_ref, *rest, TB, fwd):
    if fwd:
        yo_ref, o_ref, st_ref = rest
    else:
        o_ref, st_ref = rest
    W, KD, C = GROUP_W, GLA_KDIM, CHUNK

    @pl.when(pl.program_id(1) == 0)
    def _init():
        st_ref[...] = jnp.zeros_like(st_ref)

    x = x_ref[...]
    q = x[:, 0:KD] * (GLA_HEAD_K ** -0.5)
    k = x[:, KD:2 * KD]
    v = x[:, 2 * KD:2 * KD + W]
    og = x[:, 2 * KD + W:2 * KD + 2 * W]
    xa = x[:, 2 * KD + 2 * W:2 * KD + 2 * W + LANE]
    z = _dot_hp(xa, aup_ref[...]) + ab_ref[...]
    la = -_softplus(-z) * (1.0 / GLA_TAU)
    tri = jnp.where(_tri_mask(C, fwd, False), 1.0, 0.0).astype(BF16)
    row_t = _iota((C, W), 0)
    lane_s = _iota((C, W), 1) % C
    m_incl = (lane_s <= row_t) if fwd else (lane_s >= row_t)
    bmask_t = ((_iota((W, KD), 0) // HEAD_DIM) == (_iota((W, KD), 1) // GLA_HEAD_K)).astype(F32)

    n_chunks = TB // C
    order = range(n_chunks) if fwd else range(n_chunks - 1, -1, -1)
    sls = [slice(c * C, (c + 1) * C) for c in order]
    bcs = [_dot_exact_lhs(tri, la[sl]) for sl in sls]
    bls = [bc[C - 1:C] if fwd else bc[0:1] for bc in bcs]
    qds = [q[sl] * jnp.exp(bc) for sl, bc in zip(sls, bcs)]
    kds = [_bf(k[sl] * jnp.exp(-bc)) for sl, bc in zip(sls, bcs)]
    khs = [k[sl] * jnp.exp(bl - bc) for sl, bc, bl in zip(sls, bcs, bls)]
    vbs = [_bf(v[sl]) for sl in sls]
    scs = [jnp.where(m_incl, _dot_nt(qd, _block_diag_bf(kd)), 0.0) for qd, kd in zip(qds, kds)]
    yis = [_dot(sc, _block_diag_bf(vb)) for sc, vb in zip(scs, vbs)]
    upds = [_dot_tn(vb, kh) * bmask_t for vb, kh in zip(vbs, khs)]
    st = st_ref[...]
    for j, sl in enumerate(sls):
        o_ref[sl, :] = yis[j] + _dot_nt(qds[j], st)
        st = st * jnp.exp(bls[j]) + upds[j]
    st_ref[...] = st

    if fwd:
        yt = o_ref[...] + yo_ref[...]
        var = _head_stats(yt * yt, avg_ref[...])
        o_ref[...] = yt * lax.rsqrt(var + NORM_EPS) * gn_ref[...] * (og * _sigmoid(og))


def _gla(proj3, alpha_up, alpha_b, gn_w, avg, tl):
    b, t, _ = proj3.shape
    tb = tl["tb_gla"]
    nb = t // tb

    def call(fwd, d, extra):
        def bm(i):
            return i if fwd else nb - 1 - i

        aup = jnp.zeros((LANE, GLA_KDIM), F32).at[:alpha_up.shape[1]].set(alpha_up[d])
        in_specs = [
            pl.BlockSpec((None, tb, 1024), lambda bi, i: (bi, bm(i), COL_GLA // 1024)),
            pl.BlockSpec((LANE, GLA_KDIM), lambda bi, i: (0, 0)),
            pl.BlockSpec((1, GLA_KDIM), lambda bi, i: (0, 0)),
            pl.BlockSpec((1, GROUP_W), lambda bi, i: (0, 0)),
            pl.BlockSpec((GROUP_W, GROUP_W), lambda bi, i: (0, 0)),
        ]
        args = [proj3, aup, alpha_b[d][None, :], gn_w[None, :], avg]
        if fwd:
            in_specs.append(pl.BlockSpec((None, tb, GROUP_W), lambda bi, i: (bi, bm(i), 0)))
            args.append(extra)
        return pl.pallas_call(
            functools.partial(_gla_kernel, TB=tb, fwd=fwd),
            grid=(b, nb),
            in_specs=in_specs,
            out_specs=pl.BlockSpec((None, tb, GROUP_W), lambda bi, i: (bi, bm(i), 0)),
            out_shape=jax.ShapeDtypeStruct((b, t, GROUP_W), F32),
            scratch_shapes=[pltpu.VMEM((GROUP_W, GLA_KDIM), F32)],
            compiler_params=_params("parallel", "arbitrary"),
            name="gla_fwd" if fwd else "gla_bwd",
        )(*args)

    y_b = call(False, 1, None)
    return call(True, 0, y_b)


def _lru_kernel(x_ref, hp_ref, hn_ref, cw_ref, cb_ref, gw_ref, gb_ref, lam_ref, *rest, TB, NB, fwd):
    if fwd:
        ho_ref, o_ref, h_ref = rest
    else:
        o_ref, h_ref = rest
    W = GROUP_W
    i = pl.program_id(1)
    blk = i if fwd else NB - 1 - i

    @pl.when(i == 0)
    def _init():
        h_ref[...] = jnp.zeros_like(h_ref)

    x = x_ref[...]
    xr, gate = x[:, 0:W], x[:, W:2 * W]
    has_prev = (blk > 0).astype(F32)
    has_next = (blk < NB - 1).astype(F32)
    prev = hp_ref[...][:, 0:W] * has_prev
    nxt = hn_ref[...][:, 0:W] * has_next
    row = _iota((TB, 1), 0)
    x_m1 = jnp.where(row == 0, prev[7:8], pltpu.roll(xr, 1, 0))
    x_m2 = jnp.where(row == 0, prev[6:7], jnp.where(row == 1, prev[7:8], pltpu.roll(xr, 2, 0)))
    x_p1 = jnp.where(row == TB - 1, nxt[0:1], pltpu.roll(xr, TB - 1, 0))
    cw = cw_ref[...]
    xc = x_m2 * cw[0:1] + x_m1 * cw[1:2] + xr * cw[2:3] + x_p1 * cw[3:4] + cb_ref[...]

    gx = _dot(xc, gw_ref[...]) + gb_ref[...]
    rec = _sigmoid(gx[:, 0:W])
    ing = _sigmoid(gx[:, W:2 * W])
    log_a = -RGLRU_C * rec * _softplus(-lam_ref[...])
    a = jnp.exp(log_a)
    bx = jnp.sqrt(-jnp.tanh(log_a) * (a * a + 1.0)) * (ing * xc)

    s = 1
    while s < TB:
        if fwd:
            keep = row >= s
            a_sh = jnp.where(keep, pltpu.roll(a, s, 0), 1.0)
            b_sh = jnp.where(keep, pltpu.roll(bx, s, 0), 0.0)
        else:
            keep = row < TB - s
            a_sh = jnp.where(keep, pltpu.roll(a, TB - s, 0), 1.0)
            b_sh = jnp.where(keep, pltpu.roll(bx, TB - s, 0), 0.0)
        bx = a * b_sh + bx
        a = a * a_sh
        s *= 2
    h = a * h_ref[...] + bx
    h_ref[...] = h[TB - 1:TB] if fwd else h[0:1]

    if fwd:
        o_ref[...] = (h + ho_ref[...]) * jax.nn.gelu(gate, approximate=True)
    else:
        o_ref[...] = h


def _block_diag_heads(w):
    h, d, e = w.shape
    eye = jnp.eye(h, dtype=w.dtype)
    return jnp.einsum('hde,hg->hdge', w, eye).reshape(h * d, h * e)


def _rglru(proj3, conv_w, conv_b, gate_w, gate_b, lam, tl):
    b, t, _ = proj3.shape
    tb = tl["tb_lru"]
    nb = t // tb
    r8 = tb // SUBLANE
    last8 = t // SUBLANE - 1
    colb = COL_LRU // 512

    def call(fwd, d, extra):
        def bm(i):
            return i if fwd else nb - 1 - i

        gw = jnp.concatenate([_block_diag_heads(gate_w[d, 0]), _block_diag_heads(gate_w[d, 1])], axis=1)
        gb = jnp.concatenate([gate_b[d, 0], gate_b[d, 1]])[None, :]
        in_specs = [
            pl.BlockSpec((None, tb, 512), lambda bi, i: (bi, bm(i), colb)),
            pl.BlockSpec((None, SUBLANE, 512), lambda bi, i: (bi, jnp.maximum(bm(i) * r8 - 1, 0), colb)),
            pl.BlockSpec((None, SUBLANE, 512), lambda bi, i: (bi, jnp.minimum((bm(i) + 1) * r8, last8), colb)),
            pl.BlockSpec((4, GROUP_W), lambda bi, i: (0, 0)),
            pl.BlockSpec((1, GROUP_W), lambda bi, i: (0, 0)),
            pl.BlockSpec((GROUP_W, 2 * GROUP_W), lambda bi, i: (0, 0)),
            pl.BlockSpec((1, 2 * GROUP_W), lambda bi, i: (0, 0)),
            pl.BlockSpec((1, GROUP_W), lambda bi, i: (0, 0)),
        ]
        args = [proj3, proj3, proj3, conv_w, conv_b[None, :], _bf(gw), gb, lam[d][None, :]]
        if fwd:
            in_specs.append(pl.BlockSpec((None, tb, GROUP_W), lambda bi, i: (bi, bm(i), 0)))
            args.append(extra)
        return pl.pallas_call(
            functools.partial(_lru_kernel, TB=tb, NB=nb, fwd=fwd),
            grid=(b, nb),
            in_specs=in_specs,
            out_specs=pl.BlockSpec((None, tb, GROUP_W), lambda bi, i: (bi, bm(i), 0)),
            out_shape=jax.ShapeDtypeStruct((b, t, GROUP_W), F32),
            scratch_shapes=[pltpu.VMEM((1, GROUP_W), F32)],
            compiler_params=_params("parallel", "arbitrary"),
            name="rglru_fwd" if fwd else "rglru_bwd",
        )(*args)

    h_b = call(False, 1, None)
    return call(True, 0, h_b)


def _rwkv_kernel(x_ref, halo_ref, mu_ref, lora_ref, w0_ref, a0_ref, kk_ref, ka_ref, rk_ref, ones_ref,
                 *rest, TB, NB, fwd):
    if fwd:
        gup_ref, gn_ref, avg_ref, yo_ref, bo_ref, o_ref, s_ref = rest
    else:
        y_ref, b_ref, s_ref = rest
    W, C = GROUP_W, CHUNK
    i = pl.program_id(1)
    blk = i if fwd else NB - 1 - i

    @pl.when(i == 0)
    def _init():
        s_ref[...] = jnp.zeros_like(s_ref)

    x = x_ref[...]
    NS = 3 * W + LANE
    zs = x[:, 0:NS]
    halo = halo_ref[...][:, 0:NS]
    if fwd:
        edge = halo[SUBLANE - 1:SUBLANE] * (blk > 0).astype(F32)
    else:
        edge = halo[0:1] * (blk < NB - 1).astype(F32)
    zs = zs + mu_ref[...] * (_shift_rows(zs, edge, fwd) - zs)
    r, k, v, xwa = zs[:, 0:W], zs[:, W:2 * W], zs[:, 2 * W:3 * W], zs[:, 3 * W:NS]
    lane_l = _iota((1, LANE), 1)
    xwa = jnp.where(lane_l < LANE // 2, jnp.tanh(xwa), xwa)
    lo = _dot_hp(xwa, lora_ref[...])
    w_log = -_softplus(-(w0_ref[...] + lo[:, 0:W])) - 0.5
    lw = -jnp.exp(w_log)
    a = _sigmoid(a0_ref[...] + lo[:, W:2 * W])
    ones_bd = ones_ref[...]
    kk = k * kk_ref[...]
    kk = kk / jnp.maximum(jnp.sqrt(_dot_exact_rhs(kk * kk, ones_bd)), 1e-12)
    k = k * (1.0 + (a - 1.0) * ka_ref[...])
    bonus = _dot_exact_rhs(r * k * rk_ref[...], ones_bd) * v
    am = -kk
    bm = kk * a

    tri = jnp.where(_tri_mask(C, fwd, False), 1.0, 0.0).astype(BF16)
    row_t = _iota((C, W), 0)
    lane_s = _iota((C, W), 1) % C
    m_strict = (lane_s < row_t) if fwd else (lane_s > row_t)
    m_incl = (lane_s <= row_t) if fwd else (lane_s >= row_t)
    eye_c = (lane_s == row_t).astype(F32)
    bmask = (_iota((W, W), 0) // HEAD_DIM) == (_iota((W, W), 1) // HEAD_DIM)

    def bd(z):
        return _block_diag_bf(_bf(z))

    n_chunks = TB // C
    order = range(n_chunks) if fwd else range(n_chunks - 1, -1, -1)
    sls = [slice(c * C, (c + 1) * C) for c in order]
    cums = [_dot_exact_lhs(tri, lw[sl]) for sl in sls]
    tots = [cum[C - 1:C] if fwd else cum[0:1] for cum in cums]
    e_negs = [jnp.exp(-cum) for cum in cums]
    e_ends = [jnp.exp(tot - cum) for tot, cum in zip(tots, cums)]
    a_ts = [am[sl] * jnp.exp(cum - lw[sl]) for sl, cum in zip(sls, cums)]
    r_ts = [r[sl] * jnp.exp(cum) for sl, cum in zip(sls, cums)]
    gs = [_dot_nt(jnp.concatenate([a_t, r_t], axis=0),
                  jnp.concatenate([bd(bm[sl] * e_neg), bd(k[sl] * e_neg)], axis=0))
          for sl, a_t, r_t, e_neg in zip(sls, a_ts, r_ts, e_negs)]
    a_abs = [jnp.where(m_strict, g[0:C, 0:W], 0.0) for g in gs]
    a_aks = [jnp.where(m_strict, g[0:C, W:2 * W], 0.0) for g in gs]
    a_rbs = [jnp.where(m_incl, g[C:2 * C, 0:W], 0.0) for g in gs]
    a_rks = [jnp.where(m_incl, g[C:2 * C, W:2 * W], 0.0) for g in gs]

    tinvs = [eye_c + a_ab for a_ab in a_abs]
    ps = [_dot(a_ab, bd(a_ab)) for a_ab in a_abs]
    avs = [_dot(jnp.concatenate([a_ak, a_rk], axis=0), bd(v[sl])) for a_ak, a_rk, sl in zip(a_aks, a_rks, sls)]
    n_terms = 2
    while 2 * n_terms < C:
        xps = [_dot(p, jnp.concatenate([bd(tinv), bd(p)], axis=1)) for p, tinv in zip(ps, tinvs)]
        tinvs = [tinv + xp[:, 0:W] for tinv, xp in zip(tinvs, xps)]
        ps = [xp[:, W:2 * W] for xp in xps]
        n_terms *= 2
    tinvs = [tinv + _dot(p, bd(tinv)) for p, tinv in zip(ps, tinvs)]

    w12s = [_dot(tinv, jnp.concatenate([bd(a_t), bd(av[0:C])], axis=1))
            for tinv, a_t, av in zip(tinvs, a_ts, avs)]
    qys = [_dot(a_rb, jnp.concatenate([bd(w12[:, 0:W]), bd(w12[:, W:2 * W])], axis=1))
           for a_rb, w12 in zip(a_rbs, w12s)]

    for j, sl in enumerate(sls):
        w1, w2 = w12s[j][:, 0:W], w12s[j][:, W:2 * W]
        qp = r_ts[j] + qys[j][:, 0:W]
        y0 = avs[j][C:2 * C] + qys[j][:, W:2 * W]
        state = s_ref[...]
        uy = _dot_nt(jnp.concatenate([w1, qp], axis=0), state)
        u = uy[0:C] + w2
        y = uy[C:2 * C] + y0
        upd = _dot_tn(jnp.concatenate([v[sl], u], axis=0),
                      jnp.concatenate([k[sl] * e_ends[j], bm[sl] * e_ends[j]], axis=0))
        s_ref[...] = state * jnp.exp(tots[j]) + jnp.where(bmask, upd, 0.0)
        if fwd:
            o_ref[sl, :] = y
        else:
            y_ref[sl, :] = y

    if fwd:
        yt = o_ref[...] + yo_ref[...]
        avg = avg_ref[...]
        yc = yt - _head_stats(yt, avg)
        var = _head_stats(yc * yc, avg)
        o = yc * lax.rsqrt(var + RWKV_GN_EPS) * gn_ref[...] + bonus + bo_ref[...]
        xg = x[:, NS:NS + LANE]
        o_ref[...] = o * _dot_hp(_sigmoid(xg), gup_ref[...])
    else:
        b_ref[...] = bonus


def _rwkv7(proj3, mu_rkv, mu_w, mu_a, w0, w_up, a0, a_up, g_up, k_k, k_a, r_k, gn_w, avg, ones_bd, tl):
    b, t, _ = proj3.shape
    tb = tl["tb_rwkv"]
    nb = t // tb
    r8 = tb // SUBLANE
    last8 = t // SUBLANE - 1
    colb = COL_RWKV // 1024
    W = GROUP_W
    mu = jnp.concatenate([mu_rkv.reshape(-1), mu_w, mu_a])[None, :]
    nl = w_up.shape[1]

    def call(fwd, d, extra):
        def bm(i):
            return i if fwd else nb - 1 - i

        lora = jnp.zeros((LANE, 2 * W), F32).at[0:nl, 0:W].set(w_up[d]).at[nl:2 * nl, W:2 * W].set(a_up[d])
        if fwd:
            halo_map = lambda bi, i: (bi, jnp.maximum(bm(i) * r8 - 1, 0), colb)
        else:
            halo_map = lambda bi, i: (bi, jnp.minimum((bm(i) + 1) * r8, last8), colb)
        row = lambda n: pl.BlockSpec((1, n), lambda bi, i: (0, 0))
        in_specs = [
            pl.BlockSpec((None, tb, 1024), lambda bi, i: (bi, bm(i), colb)),
            pl.BlockSpec((None, SUBLANE, 1024), halo_map),
            row(3 * W + LANE),
            pl.BlockSpec((LANE, 2 * W), lambda bi, i: (0, 0)),
            row(W), row(W), row(W), row(W), row(W),
            pl.BlockSpec((W, W), lambda bi, i: (0, 0)),
        ]
        args = [proj3, proj3, mu, lora, w0[d][None, :], a0[d][None, :], k_k[None, :], k_a[None, :],
                r_k.reshape(1, W), ones_bd]
        blk = pl.BlockSpec((None, tb, W), lambda bi, i: (bi, bm(i), 0))
        if fwd:
            in_specs += [pl.BlockSpec((LANE, W), lambda bi, i: (0, 0)), row(W),
                         pl.BlockSpec((W, W), lambda bi, i: (0, 0)), blk, blk]
            args += [g_up, gn_w[None, :], avg, extra[0], extra[1]]
            out_specs = blk
            out_shape = jax.ShapeDtypeStruct((b, t, W), F32)
        else:
            out_specs = [blk, blk]
            out_shape = [jax.ShapeDtypeStruct((b, t, W), F32)] * 2
        return pl.pallas_call(
            functools.partial(_rwkv_kernel, TB=tb, NB=nb, fwd=fwd),
            grid=(b, nb),
            in_specs=in_specs,
            out_specs=out_specs,
            out_shape=out_shape,
            scratch_shapes=[pltpu.VMEM((W, W), F32)],
            compiler_params=_params("parallel", "arbitrary"),
            name="rwkv7_fwd" if fwd else "rwkv7_bwd",
        )(*args)

    y_b, bonus_b = call(False, 1, None)
    return call(True, 0, (y_b, bonus_b))


def _out_proj_kernel(x_ref, o1_ref, o2_ref, o3_ref, o4_ref, w_ref, g_ref, rw_ref, rb_ref, rwt_ref, rbt_ref,
                     xo_ref, xn_ref, aff_ref, afft_ref):
    W = GROUP_W
    w = w_ref[...]
    acc = x_ref[...]
    for gi, o_ref in enumerate((o1_ref, o2_ref, o3_ref, o4_ref)):
        acc = acc + jnp.dot(_bf(o_ref[...]), w[gi * W:(gi + 1) * W], preferred_element_type=F32)
    xo_ref[...] = acc
    ms = jnp.mean(acc * acc, axis=-1, keepdims=True)
    xn = acc * lax.rsqrt(ms + NORM_EPS) * g_ref[...]
    xn_ref[...] = _bf(xn)
    logits = _dot_hp(xn, rw_ref[...]) + rb_ref[...]
    logits = logits - jnp.max(logits, axis=-1, keepdims=True)
    e = jnp.exp(logits)
    aff_ref[...] = e / jnp.sum(e, axis=-1, keepdims=True)
    lt = _dot_hp_nt(rwt_ref[...], xn) + rbt_ref[...]
    lt = lt - jnp.max(lt, axis=0, keepdims=True)
    et = jnp.exp(lt)
    afft_ref[...] = et / jnp.sum(et, axis=0, keepdims=True)


def _out_proj(x2, outs, w_out_bf, gain, router_w, router_b, b, t, tl):
    n, d = x2.shape
    tm = tl["tm_out"]
    tpb = t // tm
    E = N_EXPERTS
    rw = jnp.zeros((d, LANE), F32).at[:, :E].set(router_w)
    rb = jnp.full((1, LANE), -1e30, F32).at[0, :E].set(router_b)
    full = lambda r, c: pl.BlockSpec((r, c), lambda i: (0, 0))
    tile = lambda c: pl.BlockSpec((tm, c), lambda i: (i, 0))
    return pl.pallas_call(
        _out_proj_kernel,
        grid=(n // tm,),
        in_specs=[tile(d)] + [tile(GROUP_W)] * 4 + [full(d, d), full(1, d), full(d, LANE), full(1, LANE),
                                                     full(E, d), full(E, 1)],
        out_specs=[tile(d), tile(d), tile(LANE),
                   pl.BlockSpec((None, E, tm), lambda i: (i // tpb, 0, i % tpb))],
        out_shape=[jax.ShapeDtypeStruct((n, d), F32), jax.ShapeDtypeStruct((n, d), BF16),
                   jax.ShapeDtypeStruct((n, LANE), F32), jax.ShapeDtypeStruct((b, E, t), F32)],
        compiler_params=_params("parallel"),
        name="out_proj_router",
    )(x2, *outs, w_out_bf, gain, rw, rb, router_w.T, router_b[:, None])


def _prefix_count(m, triu):
    e, t = m.shape
    nblk = t // LANE
    stacked = jnp.concatenate([m[:, j * LANE:(j + 1) * LANE] for j in range(nblk)], axis=0)
    incl = jnp.dot(_bf(stacked), triu, preferred_element_type=F32)
    pieces = []
    off = jnp.zeros((e, 1), F32)
    for j in range(nblk):
        blk = incl[j * e:(j + 1) * e]
        pieces.append(blk - stacked[j * e:(j + 1) * e] + off)
        off = off + blk[:, LANE - 1:LANE]
    return jnp.concatenate(pieces, axis=1)


def _select_kernel(afft_ref, rank_ref, rankc_ref, hit_ref, *, cap, sb_size, tk):
    aff = afft_ref[...]
    e, t = aff.shape
    bits = lax.bitcast_convert_type(aff, jnp.int32)

    def body(i, prefix):
        cand = prefix | jnp.left_shift(jnp.int32(1), 30 - i)
        cnt = jnp.sum((bits >= cand).astype(F32), axis=1, keepdims=True)
        return jnp.where(cnt >= cap, cand, prefix)

    thr = lax.fori_loop(0, 31, body, jnp.zeros((e, 1), jnp.int32))
    gt = (bits > thr).astype(F32)
    eq = (bits == thr).astype(F32)
    need = cap - jnp.sum(gt, axis=1, keepdims=True)
    triu = jnp.where(_tri_mask(LANE, False, False), 1.0, 0.0).astype(BF16)
    sel = gt + eq * (_prefix_count(eq, triu) < need).astype(F32)
    rank = jnp.where(sel > 0.5, _prefix_count(sel, triu), -1.0)
    rank_ref[...] = rank
    padded = jnp.concatenate([rank, jnp.full((LANE - e, t), -1.0, F32)], axis=0)
    rankc_ref[...] = padded.T
    tile_of = (_iota((t, LANE), 0) // tk == _iota((t, LANE), 1))
    ind = jnp.where(tile_of, 1.0, 0.0).astype(BF16)
    for sb in range(cap // sb_size):
        in_sb = (rank >= float(sb * sb_size)) & (rank < float((sb + 1) * sb_size))
        cnt = jnp.dot(jnp.where(in_sb, 1.0, 0.0).astype(BF16), ind, preferred_element_type=F32)
        hit_ref[sb] = (cnt > 0.5).astype(jnp.int32)


def _select(afft, cap, tl):
    b, e, t = afft.shape
    sb_size, tk = min(tl["sb_moe"], cap), tl["tk_moe"]
    nsb, nt = cap // sb_size, t // tk
    rank, rankc, hit = pl.pallas_call(
        functools.partial(_select_kernel, cap=cap, sb_size=sb_size, tk=tk),
        grid=(b,),
        in_specs=[pl.BlockSpec((None, e, t), lambda i: (i, 0, 0))],
        out_specs=[pl.BlockSpec((None, e, t), lambda i: (i, 0, 0)),
                   pl.BlockSpec((None, t, LANE), lambda i: (i, 0, 0)),
                   pl.BlockSpec((None, nsb, e, LANE), lambda i: (i, 0, 0, 0))],
        out_shape=[jax.ShapeDtypeStruct((b, e, t), F32), jax.ShapeDtypeStruct((b, t, LANE), F32),
                   jax.ShapeDtypeStruct((b, nsb, e, LANE), jnp.int32)],
        compiler_params=_params("parallel"),
        name="expert_choice_select",
    )(afft)
    hit_flat = jnp.transpose(hit[..., :nt], (0, 2, 1, 3)).reshape(-1)
    return rank, rankc, hit_flat


def _gather_kernel(hit_ref, rank_ref, xn_ref, xs_ref, acc_ref, *, cap, sb_size):
    b, e = pl.program_id(0), pl.program_id(1)
    ne = pl.num_programs(1)
    nt, tk = rank_ref.shape[1], rank_ref.shape[2]
    nsb = cap // sb_size
    slot = _iota((sb_size, tk), 0).astype(F32)
    for sb in range(nsb):
        acc_ref[...] = jnp.zeros_like(acc_ref)
        base = ((b * ne + e) * nsb + sb) * nt

        def body(j, carry):
            @pl.when(hit_ref[base + j] > 0)
            def _():
                r = rank_ref[e, pl.ds(j, 1), :] - float(sb * sb_size)
                onehot = jnp.where(r == slot, 1.0, 0.0).astype(BF16)
                rows = xn_ref[pl.ds(pl.multiple_of(j * tk, tk), tk), :]
                acc_ref[...] += jnp.dot(onehot, rows, preferred_element_type=F32)

            return carry

        lax.fori_loop(0, nt, body, 0)
        xs_ref[sb * sb_size:(sb + 1) * sb_size, :] = _bf(acc_ref[...])


def _gather(hit, rank, xn3, cap, tl):
    b, e, t = rank.shape
    d = xn3.shape[2]
    sb_size, tk = min(tl["sb_moe"], cap), tl["tk_moe"]
    return pl.pallas_call(
        functools.partial(_gather_kernel, cap=cap, sb_size=sb_size),
        grid_spec=pltpu.PrefetchScalarGridSpec(
            num_scalar_prefetch=1,
            grid=(b, e),
            in_specs=[pl.BlockSpec((None, e, t // tk, tk), lambda bi, ei, h: (bi, 0, 0, 0)),
                      pl.BlockSpec((None, t, d), lambda bi, ei, h: (bi, 0, 0))],
            out_specs=pl.BlockSpec((None, None, cap, d), lambda bi, ei, h: (bi, ei, 0, 0)),
            scratch_shapes=[pltpu.VMEM((sb_size, d), F32)]),
        out_shape=jax.ShapeDtypeStruct((b, e, cap, d), BF16),
        compiler_params=_params("parallel", "arbitrary"),
        name="moe_gather",
    )(hit, rank.reshape(b, e, t // tk, tk), xn3)


def _ffn_kernel(xs_ref, wg_ref, wu_ref, wd_ref, o_ref, wgb_ref, wub_ref, wdb_ref):
    @pl.when(pl.program_id(1) == 0)
    def _():
        wgb_ref[...] = _bf(wg_ref[...])
        wub_ref[...] = _bf(wu_ref[...])
        wdb_ref[...] = _bf(wd_ref[...])

    xs = xs_ref[...]
    hg = jnp.dot(xs, wgb_ref[...], preferred_element_type=F32)
    hu = jnp.dot(xs, wub_ref[...], preferred_element_type=F32)
    hid = _bf(hg * _sigmoid(hg) * hu)
    o_ref[...] = _bf(jnp.dot(hid, wdb_ref[...], preferred_element_type=F32))


def _ffn(xs, w_gate, w_up, w_down, layer):
    b, e, cap, d = xs.shape
    f = w_gate.shape[3]
    wspec = lambda r, c: pl.BlockSpec((None, None, r, c), lambda ei, bi: (layer, ei, 0, 0))
    xspec = pl.BlockSpec((None, None, cap, d), lambda ei, bi: (bi, ei, 0, 0))
    return pl.pallas_call(
        _ffn_kernel,
        grid=(e, b),
        in_specs=[xspec, wspec(d, f), wspec(d, f), wspec(f, d)],
        out_specs=xspec,
        out_shape=jax.ShapeDtypeStruct((b, e, cap, d), BF16),
        scratch_shapes=[pltpu.VMEM((d, f), BF16), pltpu.VMEM((d, f), BF16), pltpu.VMEM((f, d), BF16)],
        compiler_params=_params("arbitrary", "arbitrary"),
        name="moe_ffn",
    )(xs, w_gate, w_up, w_down)


def _combine_kernel(hit_ref, x_ref, rankc_ref, aff_ref, o_ref, g_ref, out_ref, *, cap, sb_size, final_norm):
    b, j = pl.program_id(0), pl.program_id(1)
    nt = pl.num_programs(1)
    nsb = cap // sb_size
    tm = x_ref.shape[0]
    slot = _iota((tm, sb_size), 1).astype(F32)
    out_ref[...] = x_ref[...]
    for e in range(N_EXPERTS):
        for sb in range(nsb):
            @pl.when(hit_ref[((b * N_EXPERTS + e) * nsb + sb) * nt + j] > 0)
            def _():
                rc = rankc_ref[:, e:e + 1] - float(sb * sb_size)
                pt = jnp.where(rc == slot, aff_ref[:, e:e + 1], 0.0)
                rows = o_ref[e * cap + sb * sb_size:e * cap + (sb + 1) * sb_size, :]
                out_ref[...] += jnp.dot(_bf(pt), rows, preferred_element_type=F32)
    if final_norm:
        acc = out_ref[...]
        ms = jnp.mean(acc * acc, axis=-1, keepdims=True)
        out_ref[...] = acc * lax.rsqrt(ms + NORM_EPS) * g_ref[...]


def _combine(hit, x3, rankc, aff3, o_flat, gain, cap, final_norm, tl):
    b, t, d = x3.shape
    tm = tl["tk_moe"]
    sb_size = min(tl["sb_moe"], cap)
    tile = lambda c: pl.BlockSpec((None, tm, c), lambda bi, i, h: (bi, i, 0))
    return pl.pallas_call(
        functools.partial(_combine_kernel, cap=cap, sb_size=sb_size, final_norm=final_norm),
        grid_spec=pltpu.PrefetchScalarGridSpec(
            num_scalar_prefetch=1,
            grid=(b, t // tm),
            in_specs=[tile(d), tile(LANE), tile(LANE),
                      pl.BlockSpec((None, N_EXPERTS * cap, d), lambda bi, i, h: (bi, 0, 0),
                                   pipeline_mode=pl.Buffered(1)),
                      pl.BlockSpec((1, d), lambda bi, i, h: (0, 0))],
            out_specs=tile(d)),
        out_shape=jax.ShapeDtypeStruct((b, t, d), F32),
        compiler_params=_params("parallel", "arbitrary"),
        name="moe_combine",
    )(hit, x3, rankc, aff3, o_flat, gain)


def kernel(x, positions, norm_mix, w_in, w_out, ret_log_decay, ret_gn, rwkv_mu_rkv, rwkv_mu_w, rwkv_mu_a, rwkv_w0, rwkv_w_up, rwkv_a0, rwkv_a_up, rwkv_g_up, rwkv_k_k, rwkv_k_a, rwkv_r_k, rwkv_gn, lru_conv_w, lru_conv_b, lru_gate_w, lru_gate_b, lru_lambda, gla_alpha_up, gla_alpha_b, gla_gn, norm_ffn, router_w, router_b, exp_w_gate, exp_w_up, exp_w_down, norm_final):
    b, t, d = x.shape
    depth = w_in.shape[0]
    n = b * t
    tl = _tiles(t)
    cap = EC_CAPACITY_FACTOR * t // N_EXPERTS
    lane = jnp.arange(GROUP_W)
    same_head = (lane[:, None] // HEAD_DIM) == (lane[None, :] // HEAD_DIM)
    avg = jnp.where(same_head, 1.0 / HEAD_DIM, 0.0).astype(BF16)
    ones_bd = jnp.where(same_head, 1.0, 0.0).astype(BF16)
    cos, sin = _rope_tables(positions)

    x2 = x.reshape(n, d)
    for l in range(depth):
        w = w_in[l]
        split = 2048 + 512
        w_perm = jnp.concatenate(
            [w[:, :2048], w[:, split:], jnp.zeros((d, IN_COLS_PAD - w.shape[1]), F32), w[:, 2048:split]], axis=1)
        proj3 = _in_proj(x2, norm_mix[l][None, :], _bf(w_perm), tl).reshape(b, t, IN_COLS_PAD)

        o_ret = _retention(proj3, cos, sin, ret_log_decay[l], ret_gn[l], avg, tl)
        o_rwkv = _rwkv7(proj3, rwkv_mu_rkv[l], rwkv_mu_w[l], rwkv_mu_a[l], rwkv_w0[l], rwkv_w_up[l],
                        rwkv_a0[l], rwkv_a_up[l], rwkv_g_up[l], rwkv_k_k[l], rwkv_k_a[l], rwkv_r_k[l],
                        rwkv_gn[l], avg, ones_bd, tl)
        o_lru = _rglru(proj3, lru_conv_w[l], lru_conv_b[l], lru_gate_w[l], lru_gate_b[l], lru_lambda[l], tl)
        o_gla = _gla(proj3, gla_alpha_up[l], gla_alpha_b[l], gla_gn[l], avg, tl)
        outs = [o.reshape(n, GROUP_W) for o in (o_ret, o_rwkv, o_lru, o_gla)]

        x_mid, xn, aff, afft = _out_proj(x2, outs, _bf(w_out[l]), norm_ffn[l][None, :], router_w[l],
                                         router_b[l], b, t, tl)
        rank, rankc, hit = _select(afft, cap, tl)
        xs = _gather(hit, rank, xn.reshape(b, t, d), cap, tl)
        o_exp = _ffn(xs, exp_w_gate, exp_w_up, exp_w_down, l)
        x3 = _combine(hit, x_mid.reshape(b, t, d), rankc, aff.reshape(b, t, LANE),
                      o_exp.reshape(b, N_EXPERTS * cap, d), norm_final[None, :], cap, l == depth - 1, tl)
        x2 = x3.reshape(n, d)
    return x2.reshape(b, t, d)
```

```python
import functools

import jax
import jax.numpy as jnp
from jax import lax
from jax.experimental import pallas as pl
from jax.experimental.pallas import tpu as pltpu

F32 = jnp.float32
BF16 = jnp.bfloat16

D_MODEL = 1024
GROUP_W = 256
N_HEADS = 4
HEAD_DIM = 64
ROPE_BASE = 10000.0
RWKV_GN_EPS = 64e-5
RGLRU_C = 8.0
GLA_KDIM = 128
GLA_HEAD_K = 32
GLA_TAU = 16.0
N_EXPERTS = 16
EC_CAPACITY_FACTOR = 2
NORM_EPS = 1e-6

LANE = 128
SUBLANE = 8
CHUNK = 64
VMEM_LIMIT = 56 * 1024 * 1024

COL_RET, COL_RWKV, COL_GLA, COL_LRU = 0, 1024, 2048, 3072
IN_COLS_PAD = 3584


def _tiles(T):
    return dict(
        tm_proj=min(512, T), tn_proj=512,
        tb_ret=min(512, T), tb_gla=min(512, T), tb_lru=min(512, T), tb_rwkv=min(512, T),
        tm_out=min(512, T), tm_comb=min(1024, T), tk_gather=min(1024, T), gather_group=4,
    )


def _params(*sem):
    return pltpu.CompilerParams(dimension_semantics=sem, vmem_limit_bytes=VMEM_LIMIT)


def _bf(x):
    return x.astype(BF16)


def _dot(a, b):
    return jnp.dot(_bf(a), _bf(b), preferred_element_type=F32)


def _dot_nt(a, b):
    return lax.dot_general(_bf(a), _bf(b), (((1,), (1,)), ((), ())), preferred_element_type=F32)


def _dot_tn(a, b):
    return lax.dot_general(_bf(a), _bf(b), (((0,), (0,)), ((), ())), preferred_element_type=F32)


def _split2(a):
    hi = _bf(a)
    lo = _bf(a - hi.astype(F32))
    return hi, lo


def _split3(a):
    hi = _bf(a)
    r = a - hi.astype(F32)
    mid = _bf(r)
    lo = _bf(r - mid.astype(F32))
    return hi, mid, lo


def _dot_exact_lhs(m, a):
    hi, mid, lo = _split3(a)
    return (jnp.dot(m, hi, preferred_element_type=F32) + jnp.dot(m, mid, preferred_element_type=F32)
            + jnp.dot(m, lo, preferred_element_type=F32))


def _dot_exact_rhs(a, m):
    hi, lo = _split2(a)
    return jnp.dot(hi, m, preferred_element_type=F32) + jnp.dot(lo, m, preferred_element_type=F32)


def _dot_hp(a, b):
    ah, al = _split2(a)
    bh, bl = _split2(b)
    return (jnp.dot(ah, bh, preferred_element_type=F32) + jnp.dot(ah, bl, preferred_element_type=F32)
            + jnp.dot(al, bh, preferred_element_type=F32))


def _dot_hp_nt(a, b):
    ah, al = _split2(a)
    bh, bl = _split2(b)
    dn = (((1,), (1,)), ((), ()))
    return (lax.dot_general(ah, bh, dn, preferred_element_type=F32)
            + lax.dot_general(ah, bl, dn, preferred_element_type=F32)
            + lax.dot_general(al, bh, dn, preferred_element_type=F32))


def _sigmoid(x):
    return 1.0 / (1.0 + jnp.exp(-x))


def _softplus(x):
    return jnp.maximum(x, 0.0) + jnp.log(1.0 + jnp.exp(-jnp.abs(x)))


def _iota(shape, dim):
    return lax.broadcasted_iota(jnp.int32, shape, dim)


def _head_mask(h, width, hd):
    return (_iota((1, width), 1) // hd == h).astype(F32)


def _stack_heads(z, hd):
    w = z.shape[1]
    return jnp.concatenate([z * _head_mask(h, w, hd) for h in range(N_HEADS)], axis=0)


def _block_diag_bf(z):
    w = z.shape[1]
    lane_head = _iota((1, w), 1) // (w // N_HEADS)
    zero = jnp.zeros_like(z)
    return jnp.concatenate([jnp.where(lane_head == h, z, zero) for h in range(N_HEADS)], axis=0)


def _unstack_sum(z, c):
    return z[0:c] + z[c:2 * c] + z[2 * c:3 * c] + z[3 * c:4 * c]


def _tri_mask(n, lower, strict):
    i, j = _iota((n, n), 0), _iota((n, n), 1)
    if lower:
        return (j < i) if strict else (j <= i)
    return (j > i) if strict else (j >= i)


def _block_tri_mask(c, lower, strict):
    n = N_HEADS * c
    i, j = _iota((n, n), 0), _iota((n, n), 1)
    same = (i // c) == (j // c)
    ii, jj = i % c, j % c
    if lower:
        t = (jj < ii) if strict else (jj <= ii)
    else:
        t = (jj > ii) if strict else (jj >= ii)
    return same & t


def _head_stats(y, avg):
    return _dot_exact_rhs(y, avg)


def _shift_rows(z, edge, fwd):
    n = z.shape[0]
    row = _iota((n, 1), 0)
    if fwd:
        return jnp.where(row == 0, edge, pltpu.roll(z, 1, 0))
    return jnp.where(row == n - 1, edge, pltpu.roll(z, n - 1, 0))


def _in_proj_kernel(x_ref, g_ref, w_ref, o_ref, *, tn):
    x = x_ref[...]
    ms = jnp.mean(x * x, axis=-1, keepdims=True)
    xn = _bf(x * lax.rsqrt(ms + NORM_EPS) * g_ref[...])
    for c in range(o_ref.shape[1] // tn):
        o_ref[:, c * tn:(c + 1) * tn] = jnp.dot(xn, w_ref[:, c * tn:(c + 1) * tn], preferred_element_type=F32)


def _in_proj(x2, gain, w_bf, tl):
    n, d = x2.shape
    tm, tn = tl["tm_proj"], tl["tn_proj"]
    nc = w_bf.shape[1]
    return pl.pallas_call(
        functools.partial(_in_proj_kernel, tn=tn),
        grid=(n // tm,),
        in_specs=[pl.BlockSpec((tm, d), lambda i: (i, 0)),
                  pl.BlockSpec((1, d), lambda i: (0, 0)),
                  pl.BlockSpec((d, nc), lambda i: (0, 0), pipeline_mode=pl.Buffered(1))],
        out_specs=pl.BlockSpec((tm, nc), lambda i: (i, 0)),
        out_shape=jax.ShapeDtypeStruct((n, nc), F32),
        compiler_params=_params("parallel"),
        name="in_proj",
    )(x2, gain, w_bf)


def _rope_kernel(pos_ref, inv_ref, sgn_ref, cos_ref, sin_ref):
    ang = pos_ref[...].astype(F32) * inv_ref[...]
    cos_ref[...] = jnp.cos(ang)
    sin_ref[...] = jnp.sin(ang) * sgn_ref[...]


def _rope_tables(positions):
    b, t = positions.shape
    n = b * t
    tm = min(1024, n)
    lane = jnp.arange(LANE)
    inv = jnp.power(ROPE_BASE, -jnp.arange(0, HEAD_DIM, 2, dtype=F32) / HEAD_DIM)
    inv_l = inv[lane % (HEAD_DIM // 2)][None, :]
    sgn = jnp.where(lane % HEAD_DIM < HEAD_DIM // 2, -1.0, 1.0).astype(F32)[None, :]
    cos, sin = pl.pallas_call(
        _rope_kernel,
        grid=(n // tm,),
        in_specs=[pl.BlockSpec((tm, 1), lambda i: (i, 0)),
                  pl.BlockSpec((1, LANE), lambda i: (0, 0)),
                  pl.BlockSpec((1, LANE), lambda i: (0, 0))],
        out_specs=[pl.BlockSpec((tm, LANE), lambda i: (i, 0))] * 2,
        out_shape=[jax.ShapeDtypeStruct((n, LANE), F32)] * 2,
        compiler_params=_params("parallel"),
        name="rope_tables",
    )(positions.reshape(n, 1), inv_l, sgn)
    return cos.reshape(b, t, LANE), sin.reshape(b, t, LANE)


def _ret_kernel(x_ref, cos_ref, sin_ref, lgl_ref, gn_ref, avg_ref, *rest, TB, fwd):
    if fwd:
        yo_ref, o_ref, s_ref = rest
    else:
        o_ref, s_ref = rest
    W, C = GROUP_W, CHUNK

    @pl.when(pl.program_id(1) == 0)
    def _init():
        s_ref[...] = jnp.zeros_like(s_ref)

    x = x_ref[...]
    q, k, v, g = x[:, 0:W], x[:, W:2 * W], x[:, 2 * W:3 * W], x[:, 3 * W:4 * W]
    cos = jnp.concatenate([cos_ref[...]] * (W // LANE), axis=1)
    sin = jnp.concatenate([sin_ref[...]] * (W // LANE), axis=1)
    first_half = (_iota((1, W), 1) % HEAD_DIM) < (HEAD_DIM // 2)

    def rot(z):
        swapped = jnp.where(first_half, pltpu.roll(z, W - HEAD_DIM // 2, 1), pltpu.roll(z, HEAD_DIM // 2, 1))
        return z * cos + swapped * sin

    q = rot(q)
    k = rot(k) * (HEAD_DIM ** -0.5)
    lgl = lgl_ref[...]
    row_t = _iota((C, W), 0)
    lane_s = _iota((C, W), 1) % C
    rel = ((row_t - lane_s) if fwd else (lane_s - row_t)).astype(F32)
    dmask = jnp.where(rel >= 0, jnp.exp(lgl * jnp.maximum(rel, 0.0)), 0.0)
    idx = _iota((C, 1), 0).astype(F32)
    if fwd:
        zeta = jnp.exp(lgl * (C - 1.0 - idx))
        xi = jnp.exp(lgl * (idx + 1.0))
    else:
        zeta = jnp.exp(lgl * idx)
        xi = jnp.exp(lgl * (C - idx))
    cd = jnp.exp(lgl * float(C))
    bmask = ((_iota((W, W), 0) // HEAD_DIM) == (_iota((W, W), 1) // HEAD_DIM)).astype(F32)

    n_chunks = TB // C
    order = range(n_chunks) if fwd else range(n_chunks - 1, -1, -1)
    sls = [slice(c * C, (c + 1) * C) for c in order]
    kbs = [_bf(k[sl]) for sl in sls]
    vbs = [_bf(v[sl]) for sl in sls]
    scs = [_dot_nt(q[sl], _block_diag_bf(kb)) * dmask for sl, kb in zip(sls, kbs)]
    yis = [_dot(sc, _block_diag_bf(vb)) for sc, vb in zip(scs, vbs)]
    upds = [_dot_tn(k[sl] * zeta, vb) * bmask for sl, vb in zip(sls, vbs)]
    state = s_ref[...]
    for j, sl in enumerate(sls):
        o_ref[sl, :] = yis[j] + _dot(q[sl] * xi, state)
        state = state * cd + upds[j]
    s_ref[...] = state

    if fwd:
        yt = o_ref[...] + yo_ref[...]
        avg = avg_ref[...]
        yc = yt - _head_stats(yt, avg)
        var = _head_stats(yc * yc, avg)
        o_ref[...] = yc * lax.rsqrt(var + NORM_EPS) * gn_ref[...] * (g * _sigmoid(g))


def _retention(proj3, cos, sin, log_decay, gn_w, avg, tl):
    b, t, _ = proj3.shape
    tb = tl["tb_ret"]
    nb = t // tb

    def call(fwd, lg, extra):
        def cm(i):
            return i if fwd else nb - 1 - i

        in_specs = [
            pl.BlockSpec((None, tb, 1024), lambda bi, i: (bi, cm(i), COL_RET // 1024)),
            pl.BlockSpec((None, tb, LANE), lambda bi, i: (bi, cm(i), 0)),
            pl.BlockSpec((None, tb, LANE), lambda bi, i: (bi, cm(i), 0)),
            pl.BlockSpec((1, GROUP_W), lambda bi, i: (0, 0)),
            pl.BlockSpec((1, GROUP_W), lambda bi, i: (0, 0)),
            pl.BlockSpec((GROUP_W, GROUP_W), lambda bi, i: (0, 0)),
        ]
        args = [proj3, cos, sin, jnp.repeat(lg, HEAD_DIM)[None, :], gn_w[None, :], avg]
        if fwd:
            in_specs.append(pl.BlockSpec((None, tb, GROUP_W), lambda bi, i: (bi, cm(i), 0)))
            args.append(extra)
        return pl.pallas_call(
            functools.partial(_ret_kernel, TB=tb, fwd=fwd),
            grid=(b, nb),
            in_specs=in_specs,
            out_specs=pl.BlockSpec((None, tb, GROUP_W), lambda bi, i: (bi, cm(i), 0)),
            out_shape=jax.ShapeDtypeStruct((b, t, GROUP_W), F32),
            scratch_shapes=[pltpu.VMEM((GROUP_W, GROUP_W), F32)],
            compiler_params=_params("parallel", "arbitrary"),
            name="retention_fwd" if fwd else "retention_bwd",
        )(*args)

    y_b = call(False, log_decay[1], None)
    return call(True, log_decay[0], y_b)


def _gla_kernel(x_ref, aup_ref, ab_ref, gn_ref, avg_ref, *rest, TB, fwd):
    if fwd:
        yo_ref, o_ref, st_ref = rest
    else:
        o_ref, st_ref = rest
    W, KD, C = GROUP_W, GLA_KDIM, CHUNK

    @pl.when(pl.program_id(1) == 0)
    def _init():
        st_ref[...] = jnp.zeros_like(st_ref)

    x = x_ref[...]
    q = x[:, 0:KD] * (GLA_HEAD_K ** -0.5)
    k = x[:, KD:2 * KD]
    v = x[:, 2 * KD:2 * KD + W]
    og = x[:, 2 * KD + W:2 * KD + 2 * W]
    xa = x[:, 2 * KD + 2 * W:2 * KD + 2 * W + LANE]
    z = _dot_hp(xa, aup_ref[...]) + ab_ref[...]
    la = -_softplus(-z) * (1.0 / GLA_TAU)
    tri = jnp.where(_tri_mask(C, fwd, False), 1.0, 0.0).astype(BF16)
    row_t = _iota((C, W), 0)
    lane_s = _iota((C, W), 1) % C
    m_incl = (lane_s <= row_t) if fwd else (lane_s >= row_t)
    bmask_t = ((_iota((W, KD), 0) // HEAD_DIM) == (_iota((W, KD), 1) // GLA_HEAD_K)).astype(F32)

    n_chunks = TB // C
    order = range(n_chunks) if fwd else range(n_chunks - 1, -1, -1)
    sls = [slice(c * C, (c + 1) * C) for c in order]
    bcs = [_dot_exact_lhs(tri, la[sl]) for sl in sls]
    bls = [bc[C - 1:C] if fwd else bc[0:1] for bc in bcs]
    qds = [q[sl] * jnp.exp(bc) for sl, bc in zip(sls, bcs)]
    kds = [_bf(k[sl] * jnp.exp(-bc)) for sl, bc in zip(sls, bcs)]
    khs = [k[sl] * jnp.exp(bl - bc) for sl, bc, bl in zip(sls, bcs, bls)]
    vbs = [_bf(v[sl]) for sl in sls]
    scs = [jnp.where(m_incl, _dot_nt(qd, _block_diag_bf(kd)), 0.0) for qd, kd in zip(qds, kds)]
    yis = [_dot(sc, _block_diag_bf(vb)) for sc, vb in zip(scs, vbs)]
    upds = [_dot_tn(vb, kh) * bmask_t for vb, kh in zip(vbs, khs)]
    st = st_ref[...]
    for j, sl in enumerate(sls):
        o_ref[sl, :] = yis[j] + _dot_nt(qds[j], st)
        st = st * jnp.exp(bls[j]) + upds[j]
    st_ref[...] = st

    if fwd:
        yt = o_ref[...] + yo_ref[...]
        var = _head_stats(yt * yt, avg_ref[...])
        o_ref[...] = yt * lax.rsqrt(var + NORM_EPS) * gn_ref[...] * (og * _sigmoid(og))


def _gla(proj3, alpha_up, alpha_b, gn_w, avg, tl):
    b, t, _ = proj3.shape
    tb = tl["tb_gla"]
    nb = t // tb

    def call(fwd, d, extra):
        def bm(i):
            return i if fwd else nb - 1 - i

        aup = jnp.zeros((LANE, GLA_KDIM), F32).at[:alpha_up.shape[1]].set(alpha_up[d])
        in_specs = [
            pl.BlockSpec((None, tb, 1024), lambda bi, i: (bi, bm(i), COL_GLA // 1024)),
            pl.BlockSpec((LANE, GLA_KDIM), lambda bi, i: (0, 0)),
            pl.BlockSpec((1, GLA_KDIM), lambda bi, i: (0, 0)),
            pl.BlockSpec((1, GROUP_W), lambda bi, i: (0, 0)),
            pl.BlockSpec((GROUP_W, GROUP_W), lambda bi, i: (0, 0)),
        ]
        args = [proj3, aup, alpha_b[d][None, :], gn_w[None, :], avg]
        if fwd:
            in_specs.append(pl.BlockSpec((None, tb, GROUP_W), lambda bi, i: (bi, bm(i), 0)))
            args.append(extra)
        return pl.pallas_call(
            functools.partial(_gla_kernel, TB=tb, fwd=fwd),
            grid=(b, nb),
            in_specs=in_specs,
            out_specs=pl.BlockSpec((None, tb, GROUP_W), lambda bi, i: (bi, bm(i), 0)),
            out_shape=jax.ShapeDtypeStruct((b, t, GROUP_W), F32),
            scratch_shapes=[pltpu.VMEM((GROUP_W, GLA_KDIM), F32)],
            compiler_params=_params("parallel", "arbitrary"),
            name="gla_fwd" if fwd else "gla_bwd",
        )(*args)

    y_b = call(False, 1, None)
    return call(True, 0, y_b)


def _lru_kernel(x_ref, hp_ref, hn_ref, cw_ref, cb_ref, gw_ref, gb_ref, lam_ref, *rest, TB, NB, fwd):
    if fwd:
        ho_ref, o_ref, h_ref = rest
    else:
        o_ref, h_ref = rest
    W = GROUP_W
    i = pl.program_id(1)
    blk = i if fwd else NB - 1 - i

    @pl.when(i == 0)
    def _init():
        h_ref[...] = jnp.zeros_like(h_ref)

    x = x_ref[...]
    xr, gate = x[:, 0:W], x[:, W:2 * W]
    has_prev = (blk > 0).astype(F32)
    has_next = (blk < NB - 1).astype(F32)
    prev = hp_ref[...][:, 0:W] * has_prev
    nxt = hn_ref[...][:, 0:W] * has_next
    row = _iota((TB, 1), 0)
    x_m1 = jnp.where(row == 0, prev[7:8], pltpu.roll(xr, 1, 0))
    x_m2 = jnp.where(row == 0, prev[6:7], jnp.where(row == 1, prev[7:8], pltpu.roll(xr, 2, 0)))
    x_p1 = jnp.where(row == TB - 1, nxt[0:1], pltpu.roll(xr, TB - 1, 0))
    cw = cw_ref[...]
    xc = x_m2 * cw[0:1] + x_m1 * cw[1:2] + xr * cw[2:3] + x_p1 * cw[3:4] + cb_ref[...]

    gx = _dot(xc, gw_ref[...]) + gb_ref[...]
    rec = _sigmoid(gx[:, 0:W])
    ing = _sigmoid(gx[:, W:2 * W])
    log_a = -RGLRU_C * rec * _softplus(-lam_ref[...])
    a = jnp.exp(log_a)
    bx = jnp.sqrt(-jnp.tanh(log_a) * (a * a + 1.0)) * (ing * xc)

    s = 1
    while s < TB:
        if s % SUBLANE == 0:
            ones, zeros = jnp.ones((s, W), F32), jnp.zeros((s, W), F32)
            if fwd:
                a_sh = jnp.concatenate([ones, a[0:TB - s]], axis=0)
                b_sh = jnp.concatenate([zeros, bx[0:TB - s]], axis=0)
            else:
                a_sh = jnp.concatenate([a[s:TB], ones], axis=0)
                b_sh = jnp.concatenate([bx[s:TB], zeros], axis=0)
        elif fwd:
            keep = row >= s
            a_sh = jnp.where(keep, pltpu.roll(a, s, 0), 1.0)
            b_sh = jnp.where(keep, pltpu.roll(bx, s, 0), 0.0)
        else:
            keep = row < TB - s
            a_sh = jnp.where(keep, pltpu.roll(a, TB - s, 0), 1.0)
            b_sh = jnp.where(keep, pltpu.roll(bx, TB - s, 0), 0.0)
        bx = a * b_sh + bx
        a = a * a_sh
        s *= 2
    h = a * h_ref[...] + bx
    h_ref[...] = h[TB - 1:TB] if fwd else h[0:1]

    if fwd:
        o_ref[...] = (h + ho_ref[...]) * jax.nn.gelu(gate, approximate=True)
    else:
        o_ref[...] = h


def _block_diag_heads(w):
    h, d, e = w.shape
    eye = jnp.eye(h, dtype=w.dtype)
    return jnp.einsum('hde,hg->hdge', w, eye).reshape(h * d, h * e)


def _rglru(proj3, conv_w, conv_b, gate_w, gate_b, lam, tl):
    b, t, _ = proj3.shape
    tb = tl["tb_lru"]
    nb = t // tb
    r8 = tb // SUBLANE
    last8 = t // SUBLANE - 1
    colb = COL_LRU // 512

    def call(fwd, d, extra):
        def bm(i):
            return i if fwd else nb - 1 - i

        gw = jnp.concatenate([_block_diag_heads(gate_w[d, 0]), _block_diag_heads(gate_w[d, 1])], axis=1)
        gb = jnp.concatenate([gate_b[d, 0], gate_b[d, 1]])[None, :]
        in_specs = [
            pl.BlockSpec((None, tb, 512), lambda bi, i: (bi, bm(i), colb)),
            pl.BlockSpec((None, SUBLANE, 512), lambda bi, i: (bi, jnp.maximum(bm(i) * r8 - 1, 0), colb)),
            pl.BlockSpec((None, SUBLANE, 512), lambda bi, i: (bi, jnp.minimum((bm(i) + 1) * r8, last8), colb)),
            pl.BlockSpec((4, GROUP_W), lambda bi, i: (0, 0)),
            pl.BlockSpec((1, GROUP_W), lambda bi, i: (0, 0)),
            pl.BlockSpec((GROUP_W, 2 * GROUP_W), lambda bi, i: (0, 0)),
            pl.BlockSpec((1, 2 * GROUP_W), lambda bi, i: (0, 0)),
            pl.BlockSpec((1, GROUP_W), lambda bi, i: (0, 0)),
        ]
        args = [proj3, proj3, proj3, conv_w, conv_b[None, :], _bf(gw), gb, lam[d][None, :]]
        if fwd:
            in_specs.append(pl.BlockSpec((None, tb, GROUP_W), lambda bi, i: (bi, bm(i), 0)))
            args.append(extra)
        return pl.pallas_call(
            functools.partial(_lru_kernel, TB=tb, NB=nb, fwd=fwd),
            grid=(b, nb),
            in_specs=in_specs,
            out_specs=pl.BlockSpec((None, tb, GROUP_W), lambda bi, i: (bi, bm(i), 0)),
            out_shape=jax.ShapeDtypeStruct((b, t, GROUP_W), F32),
            scratch_shapes=[pltpu.VMEM((1, GROUP_W), F32)],
            compiler_params=_params("parallel", "arbitrary"),
            name="rglru_fwd" if fwd else "rglru_bwd",
        )(*args)

    h_b = call(False, 1, None)
    return call(True, 0, h_b)


def _rwkv_kernel(x_ref, halo_ref, mu_ref, lora_ref, w0_ref, a0_ref, kk_ref, ka_ref, rk_ref, ones_ref,
                 *rest, TB, NB, fwd):
    if fwd:
        gup_ref, gn_ref, avg_ref, yo_ref, bo_ref, o_ref, s_ref = rest
    else:
        y_ref, b_ref, s_ref = rest
    W, C = GROUP_W, CHUNK
    i = pl.program_id(1)
    blk = i if fwd else NB - 1 - i

    @pl.when(i == 0)
    def _init():
        s_ref[...] = jnp.zeros_like(s_ref)

    x = x_ref[...]
    NS = 3 * W + LANE
    zs = x[:, 0:NS]
    halo = halo_ref[...][:, 0:NS]
    if fwd:
        edge = halo[SUBLANE - 1:SUBLANE] * (blk > 0).astype(F32)
    else:
        edge = halo[0:1] * (blk < NB - 1).astype(F32)
    zs = zs + mu_ref[...] * (_shift_rows(zs, edge, fwd) - zs)
    r, k, v, xwa = zs[:, 0:W], zs[:, W:2 * W], zs[:, 2 * W:3 * W], zs[:, 3 * W:NS]
    lane_l = _iota((1, LANE), 1)
    xwa = jnp.where(lane_l < LANE // 2, jnp.tanh(xwa), xwa)
    lo = _dot_hp(xwa, lora_ref[...])
    w_log = -_softplus(-(w0_ref[...] + lo[:, 0:W])) - 0.5
    lw = -jnp.exp(w_log)
    a = _sigmoid(a0_ref[...] + lo[:, W:2 * W])
    ones_bd = ones_ref[...]
    kk = k * kk_ref[...]
    kk = kk / jnp.maximum(jnp.sqrt(_dot_exact_rhs(kk * kk, ones_bd)), 1e-12)
    k = k * (1.0 + (a - 1.0) * ka_ref[...])
    bonus = _dot_exact_rhs(r * k * rk_ref[...], ones_bd) * v
    am = -kk
    bm = kk * a

    tri = jnp.where(_tri_mask(C, fwd, False), 1.0, 0.0).astype(BF16)
    row_t = _iota((C, W), 0)
    lane_s = _iota((C, W), 1) % C
    m_strict = (lane_s < row_t) if fwd else (lane_s > row_t)
    m_incl = (lane_s <= row_t) if fwd else (lane_s >= row_t)
    eye_c = (lane_s == row_t).astype(F32)
    bmask = (_iota((W, W), 0) // HEAD_DIM) == (_iota((W, W), 1) // HEAD_DIM)

    def bd(z):
        return _block_diag_bf(_bf(z))

    n_chunks = TB // C
    order = range(n_chunks) if fwd else range(n_chunks - 1, -1, -1)
    sls = [slice(c * C, (c + 1) * C) for c in order]
    cums = [_dot_exact_lhs(tri, lw[sl]) for sl in sls]
    tots = [cum[C - 1:C] if fwd else cum[0:1] for cum in cums]
    e_negs = [jnp.exp(-cum) for cum in cums]
    e_ends = [jnp.exp(tot - cum) for tot, cum in zip(tots, cums)]
    a_ts = [am[sl] * jnp.exp(cum - lw[sl]) for sl, cum in zip(sls, cums)]
    r_ts = [r[sl] * jnp.exp(cum) for sl, cum in zip(sls, cums)]
    gs = [_dot_nt(jnp.concatenate([a_t, r_t], axis=0),
                  jnp.concatenate([bd(bm[sl] * e_neg), bd(k[sl] * e_neg)], axis=0))
          for sl, a_t, r_t, e_neg in zip(sls, a_ts, r_ts, e_negs)]
    a_abs = [jnp.where(m_strict, g[0:C, 0:W], 0.0) for g in gs]
    a_aks = [jnp.where(m_strict, g[0:C, W:2 * W], 0.0) for g in gs]
    a_rbs = [jnp.where(m_incl, g[C:2 * C, 0:W], 0.0) for g in gs]
    a_rks = [jnp.where(m_incl, g[C:2 * C, W:2 * W], 0.0) for g in gs]

    tinvs = [eye_c + a_ab for a_ab in a_abs]
    ps = [_dot(a_ab, bd(a_ab)) for a_ab in a_abs]
    avs = [_dot(jnp.concatenate([a_ak, a_rk], axis=0), bd(v[sl])) for a_ak, a_rk, sl in zip(a_aks, a_rks, sls)]
    n_terms = 2
    while 2 * n_terms < C:
        xps = [_dot(jnp.concatenate([tinv, p], axis=0), bd(p)) for p, tinv in zip(ps, tinvs)]
        tinvs = [tinv + xp[0:C] for tinv, xp in zip(tinvs, xps)]
        ps = [xp[C:2 * C] for xp in xps]
        n_terms *= 2
    tinvs = [tinv + _dot(tinv, bd(p)) for p, tinv in zip(ps, tinvs)]

    w12s = [_dot(tinv, jnp.concatenate([bd(a_t), bd(av[0:C])], axis=1))
            for tinv, a_t, av in zip(tinvs, a_ts, avs)]
    qys = [_dot(a_rb, jnp.concatenate([bd(w12[:, 0:W]), bd(w12[:, W:2 * W])], axis=1))
           for a_rb, w12 in zip(a_rbs, w12s)]

    for j, sl in enumerate(sls):
        w1, w2 = w12s[j][:, 0:W], w12s[j][:, W:2 * W]
        qp = r_ts[j] + qys[j][:, 0:W]
        y0 = avs[j][C:2 * C] + qys[j][:, W:2 * W]
        state = s_ref[...]
        uy = _dot_nt(jnp.concatenate([w1, qp], axis=0), state)
        u = uy[0:C] + w2
        y = uy[C:2 * C] + y0
        upd = _dot_tn(jnp.concatenate([v[sl], u], axis=0),
                      jnp.concatenate([k[sl] * e_ends[j], bm[sl] * e_ends[j]], axis=0))
        s_ref[...] = state * jnp.exp(tots[j]) + jnp.where(bmask, upd, 0.0)
        if fwd:
            o_ref[sl, :] = y
        else:
            y_ref[sl, :] = y

    if fwd:
        yt = o_ref[...] + yo_ref[...]
        avg = avg_ref[...]
        yc = yt - _head_stats(yt, avg)
        var = _head_stats(yc * yc, avg)
        o = yc * lax.rsqrt(var + RWKV_GN_EPS) * gn_ref[...] + bonus + bo_ref[...]
        xg = x[:, NS:NS + LANE]
        o_ref[...] = o * _dot_hp(_sigmoid(xg), gup_ref[...])
    else:
        b_ref[...] = bonus


def _rwkv7(proj3, mu_rkv, mu_w, mu_a, w0, w_up, a0, a_up, g_up, k_k, k_a, r_k, gn_w, avg, ones_bd, tl):
    b, t, _ = proj3.shape
    tb = tl["tb_rwkv"]
    nb = t // tb
    r8 = tb // SUBLANE
    last8 = t // SUBLANE - 1
    colb = COL_RWKV // 1024
    W = GROUP_W
    mu = jnp.concatenate([mu_rkv.reshape(-1), mu_w, mu_a])[None, :]
    nl = w_up.shape[1]

    def call(fwd, d, extra):
        def bm(i):
            return i if fwd else nb - 1 - i

        lora = jnp.zeros((LANE, 2 * W), F32).at[0:nl, 0:W].set(w_up[d]).at[nl:2 * nl, W:2 * W].set(a_up[d])
        if fwd:
            halo_map = lambda bi, i: (bi, jnp.maximum(bm(i) * r8 - 1, 0), colb)
        else:
            halo_map = lambda bi, i: (bi, jnp.minimum((bm(i) + 1) * r8, last8), colb)
        row = lambda n: pl.BlockSpec((1, n), lambda bi, i: (0, 0))
        in_specs = [
            pl.BlockSpec((None, tb, 1024), lambda bi, i: (bi, bm(i), colb)),
            pl.BlockSpec((None, SUBLANE, 1024), halo_map),
            row(3 * W + LANE),
            pl.BlockSpec((LANE, 2 * W), lambda bi, i: (0, 0)),
            row(W), row(W), row(W), row(W), row(W),
            pl.BlockSpec((W, W), lambda bi, i: (0, 0)),
        ]
        args = [proj3, proj3, mu, lora, w0[d][None, :], a0[d][None, :], k_k[None, :], k_a[None, :],
                r_k.reshape(1, W), ones_bd]
        blk = pl.BlockSpec((None, tb, W), lambda bi, i: (bi, bm(i), 0))
        if fwd:
            in_specs += [pl.BlockSpec((LANE, W), lambda bi, i: (0, 0)), row(W),
                         pl.BlockSpec((W, W), lambda bi, i: (0, 0)), blk, blk]
            args += [g_up, gn_w[None, :], avg, extra[0], extra[1]]
            out_specs = blk
            out_shape = jax.ShapeDtypeStruct((b, t, W), F32)
        else:
            out_specs = [blk, blk]
            out_shape = [jax.ShapeDtypeStruct((b, t, W), F32)] * 2
        return pl.pallas_call(
            functools.partial(_rwkv_kernel, TB=tb, NB=nb, fwd=fwd),
            grid=(b, nb),
            in_specs=in_specs,
            out_specs=out_specs,
            out_shape=out_shape,
            scratch_shapes=[pltpu.VMEM((W, W), F32)],
            compiler_params=_params("parallel", "arbitrary"),
            name="rwkv7_fwd" if fwd else "rwkv7_bwd",
        )(*args)

    y_b, bonus_b = call(False, 1, None)
    return call(True, 0, (y_b, bonus_b))


def _out_proj_kernel(x_ref, o1_ref, o2_ref, o3_ref, o4_ref, w_ref, g_ref, rw_ref, rb_ref, rwt_ref, rbt_ref,
                     xo_ref, xn_ref, aff_ref, afft_ref):
    W = GROUP_W
    w = w_ref[...]
    acc = x_ref[...]
    for gi, o_ref in enumerate((o1_ref, o2_ref, o3_ref, o4_ref)):
        acc = acc + jnp.dot(_bf(o_ref[...]), w[gi * W:(gi + 1) * W], preferred_element_type=F32)
    xo_ref[...] = acc
    ms = jnp.mean(acc * acc, axis=-1, keepdims=True)
    xn = acc * lax.rsqrt(ms + NORM_EPS) * g_ref[...]
    xn_ref[...] = _bf(xn)
    logits = _dot_hp(xn, rw_ref[...]) + rb_ref[...]
    logits = logits - jnp.max(logits, axis=-1, keepdims=True)
    e = jnp.exp(logits)
    aff_ref[...] = e / jnp.sum(e, axis=-1, keepdims=True)
    lt = _dot_hp_nt(rwt_ref[...], xn) + rbt_ref[...]
    lt = lt - jnp.max(lt, axis=0, keepdims=True)
    et = jnp.exp(lt)
    afft_ref[...] = et / jnp.sum(et, axis=0, keepdims=True)


def _out_proj(x2, outs, w_out_bf, gain, router_w, router_b, b, t, tl):
    n, d = x2.shape
    tm = tl["tm_out"]
    tpb = t // tm
    E = N_EXPERTS
    rw = jnp.zeros((d, LANE), F32).at[:, :E].set(router_w)
    rb = jnp.full((1, LANE), -1e30, F32).at[0, :E].set(router_b)
    full = lambda r, c: pl.BlockSpec((r, c), lambda i: (0, 0))
    tile = lambda c: pl.BlockSpec((tm, c), lambda i: (i, 0))
    return pl.pallas_call(
        _out_proj_kernel,
        grid=(n // tm,),
        in_specs=[tile(d)] + [tile(GROUP_W)] * 4 + [full(d, d), full(1, d), full(d, LANE), full(1, LANE),
                                                     full(E, d), full(E, 1)],
        out_specs=[tile(d), tile(d), tile(LANE),
                   pl.BlockSpec((None, E, tm), lambda i: (i // tpb, 0, i % tpb))],
        out_shape=[jax.ShapeDtypeStruct((n, d), F32), jax.ShapeDtypeStruct((n, d), BF16),
                   jax.ShapeDtypeStruct((n, LANE), F32), jax.ShapeDtypeStruct((b, E, t), F32)],
        compiler_params=_params("parallel"),
        name="out_proj_router",
    )(x2, *outs, w_out_bf, gain, rw, rb, router_w.T, router_b[:, None])


def _prefix_count(m, triu):
    e, t = m.shape
    nblk = t // LANE
    stacked = jnp.concatenate([m[:, j * LANE:(j + 1) * LANE] for j in range(nblk)], axis=0)
    incl = jnp.dot(_bf(stacked), triu, preferred_element_type=F32)
    pieces = []
    off = jnp.zeros((e, 1), F32)
    for j in range(nblk):
        blk = incl[j * e:(j + 1) * e]
        pieces.append(blk - stacked[j * e:(j + 1) * e] + off)
        off = off + blk[:, LANE - 1:LANE]
    return jnp.concatenate(pieces, axis=1)


def _select_kernel(afft_ref, rank_ref, rankc_ref, *, cap):
    aff = afft_ref[...]
    e, t = aff.shape
    bits = lax.bitcast_convert_type(aff, jnp.int32)

    def body(i, prefix):
        cand = prefix | jnp.left_shift(jnp.int32(1), 30 - i)
        cnt = jnp.sum((bits >= cand).astype(F32), axis=1, keepdims=True)
        return jnp.where(cnt >= cap, cand, prefix)

    thr = lax.fori_loop(0, 31, body, jnp.zeros((e, 1), jnp.int32))
    gt = (bits > thr).astype(F32)
    eq = (bits == thr).astype(F32)
    need = cap - jnp.sum(gt, axis=1, keepdims=True)
    triu = jnp.where(_tri_mask(LANE, False, False), 1.0, 0.0).astype(BF16)
    sel = gt + eq * (_prefix_count(eq, triu) < need).astype(F32)
    rank = jnp.where(sel > 0.5, _prefix_count(sel, triu), -1.0)
    rank_ref[...] = rank
    padded = jnp.concatenate([rank, jnp.full((LANE - e, t), -1.0, F32)], axis=0)
    rankc_ref[...] = padded.T


def _select(afft, cap):
    b, e, t = afft.shape
    return pl.pallas_call(
        functools.partial(_select_kernel, cap=cap),
        grid=(b,),
        in_specs=[pl.BlockSpec((None, e, t), lambda i: (i, 0, 0))],
        out_specs=[pl.BlockSpec((None, e, t), lambda i: (i, 0, 0)),
                   pl.BlockSpec((None, t, LANE), lambda i: (i, 0, 0))],
        out_shape=[jax.ShapeDtypeStruct((b, e, t), F32), jax.ShapeDtypeStruct((b, t, LANE), F32)],
        compiler_params=_params("parallel"),
        name="expert_choice_select",
    )(afft)


def _gather_kernel(rank_ref, xn_ref, xs_ref, acc_ref, *, cap, tk, group):
    g0 = pl.program_id(1) * group
    t = xn_ref.shape[0]
    slot = _iota((cap, tk), 0).astype(F32)
    for j in range(t // tk):
        onehot = jnp.concatenate(
            [jnp.where(rank_ref[pl.ds(g0 + g, 1), j * tk:(j + 1) * tk] == slot, 1.0, 0.0).astype(BF16)
             for g in range(group)], axis=0)
        part = jnp.dot(onehot, xn_ref[j * tk:(j + 1) * tk, :], preferred_element_type=F32)
        if j == 0:
            acc_ref[...] = part
        else:
            acc_ref[...] += part
    for g in range(group):
        xs_ref[g] = _bf(acc_ref[g * cap:(g + 1) * cap, :])


def _gather(rank, xn3, cap, tl):
    b, e, t = rank.shape
    d = xn3.shape[2]
    group = tl["gather_group"]
    return pl.pallas_call(
        functools.partial(_gather_kernel, cap=cap, tk=tl["tk_gather"], group=group),
        grid=(b, e // group),
        in_specs=[pl.BlockSpec((None, e, t), lambda bi, gi: (bi, 0, 0)),
                  pl.BlockSpec((None, t, d), lambda bi, gi: (bi, 0, 0))],
        out_specs=pl.BlockSpec((None, group, cap, d), lambda bi, gi: (bi, gi, 0, 0)),
        out_shape=jax.ShapeDtypeStruct((b, e, cap, d), BF16),
        scratch_shapes=[pltpu.VMEM((group * cap, d), F32)],
        compiler_params=_params("parallel", "arbitrary"),
        name="moe_gather",
    )(rank, xn3)


def _ffn_kernel(xs_ref, wg_ref, wu_ref, wd_ref, o_ref, wgb_ref, wub_ref, wdb_ref):
    @pl.when(pl.program_id(1) == 0)
    def _():
        wgb_ref[...] = _bf(wg_ref[...])
        wub_ref[...] = _bf(wu_ref[...])
        wdb_ref[...] = _bf(wd_ref[...])

    xs = xs_ref[...]
    hg = jnp.dot(xs, wgb_ref[...], preferred_element_type=F32)
    hu = jnp.dot(xs, wub_ref[...], preferred_element_type=F32)
    hid = _bf(hg * _sigmoid(hg) * hu)
    o_ref[...] = _bf(jnp.dot(hid, wdb_ref[...], preferred_element_type=F32))


def _ffn(xs, w_gate, w_up, w_down, layer):
    b, e, cap, d = xs.shape
    f = w_gate.shape[3]
    wspec = lambda r, c: pl.BlockSpec((None, None, r, c), lambda ei, bi: (layer, ei, 0, 0))
    xspec = pl.BlockSpec((None, None, cap, d), lambda ei, bi: (bi, ei, 0, 0))
    return pl.pallas_call(
        _ffn_kernel,
        grid=(e, b),
        in_specs=[xspec, wspec(d, f), wspec(d, f), wspec(f, d)],
        out_specs=xspec,
        out_shape=jax.ShapeDtypeStruct((b, e, cap, d), BF16),
        scratch_shapes=[pltpu.VMEM((d, f), BF16), pltpu.VMEM((d, f), BF16), pltpu.VMEM((f, d), BF16)],
        compiler_params=_params("arbitrary", "arbitrary"),
        name="moe_ffn",
    )(xs, w_gate, w_up, w_down)


def _combine_kernel(x_ref, rankc_ref, aff_ref, o_ref, g_ref, out_ref, *, cap, final_norm):
    rankc = rankc_ref[...]
    aff = aff_ref[...]
    tm = x_ref.shape[0]
    slot = _iota((tm, cap), 1).astype(F32)
    acc = x_ref[...]
    for e in range(N_EXPERTS):
        pt = jnp.where(rankc[:, e:e + 1] == slot, aff[:, e:e + 1], 0.0)
        acc = acc + jnp.dot(_bf(pt), o_ref[e * cap:(e + 1) * cap, :], preferred_element_type=F32)
    if final_norm:
        ms = jnp.mean(acc * acc, axis=-1, keepdims=True)
        acc = acc * lax.rsqrt(ms + NORM_EPS) * g_ref[...]
    out_ref[...] = acc


def _combine(x3, rankc, aff3, o_flat, gain, cap, final_norm, tl):
    b, t, d = x3.shape
    tm = tl["tm_comb"]
    tile = lambda c: pl.BlockSpec((None, tm, c), lambda bi, i: (bi, i, 0))
    return pl.pallas_call(
        functools.partial(_combine_kernel, cap=cap, final_norm=final_norm),
        grid=(b, t // tm),
        in_specs=[tile(d), tile(LANE), tile(LANE),
                  pl.BlockSpec((None, N_EXPERTS * cap, d), lambda bi, i: (bi, 0, 0),
                               pipeline_mode=pl.Buffered(1)),
                  pl.BlockSpec((1, d), lambda bi, i: (0, 0))],
        out_specs=tile(d),
        out_shape=jax.ShapeDtypeStruct((b, t, d), F32),
        compiler_params=_params("parallel", "arbitrary"),
        name="moe_combine",
    )(x3, rankc, aff3, o_flat, gain)


def kernel(x, positions, norm_mix, w_in, w_out, ret_log_decay, ret_gn, rwkv_mu_rkv, rwkv_mu_w, rwkv_mu_a, rwkv_w0, rwkv_w_up, rwkv_a0, rwkv_a_up, rwkv_g_up, rwkv_k_k, rwkv_k_a, rwkv_r_k, rwkv_gn, lru_conv_w, lru_conv_b, lru_gate_w, lru_gate_b, lru_lambda, gla_alpha_up, gla_alpha_b, gla_gn, norm_ffn, router_w, router_b, exp_w_gate, exp_w_up, exp_w_down, norm_final):
    b, t, d = x.shape
    depth = w_in.shape[0]
    n = b * t
    tl = _tiles(t)
    cap = EC_CAPACITY_FACTOR * t // N_EXPERTS
    lane = jnp.arange(GROUP_W)
    same_head = (lane[:, None] // HEAD_DIM) == (lane[None, :] // HEAD_DIM)
    avg = jnp.where(same_head, 1.0 / HEAD_DIM, 0.0).astype(BF16)
    ones_bd = jnp.where(same_head, 1.0, 0.0).astype(BF16)
    cos, sin = _rope_tables(positions)

    x2 = x.reshape(n, d)
    for l in range(depth):
        w = w_in[l]
        split = 2048 + 512
        w_perm = jnp.concatenate(
            [w[:, :2048], w[:, split:], jnp.zeros((d, IN_COLS_PAD - w.shape[1]), F32), w[:, 2048:split]], axis=1)
        proj3 = _in_proj(x2, norm_mix[l][None, :], _bf(w_perm), tl).reshape(b, t, IN_COLS_PAD)

        o_ret = _retention(proj3, cos, sin, ret_log_decay[l], ret_gn[l], avg, tl)
        o_rwkv = _rwkv7(proj3, rwkv_mu_rkv[l], rwkv_mu_w[l], rwkv_mu_a[l], rwkv_w0[l], rwkv_w_up[l],
                        rwkv_a0[l], rwkv_a_up[l], rwkv_g_up[l], rwkv_k_k[l], rwkv_k_a[l], rwkv_r_k[l],
                        rwkv_gn[l], avg, ones_bd, tl)
        o_lru = _rglru(proj3, lru_conv_w[l], lru_conv_b[l], lru_gate_w[l], lru_gate_b[l], lru_lambda[l], tl)
        o_gla = _gla(proj3, gla_alpha_up[l], gla_alpha_b[l], gla_gn[l], avg, tl)
        outs = [o.reshape(n, GROUP_W) for o in (o_ret, o_rwkv, o_lru, o_gla)]

        x_mid, xn, aff, afft = _out_proj(x2, outs, _bf(w_out[l]), norm_ffn[l][None, :], router_w[l],
                                         router_b[l], b, t, tl)
        rank, rankc = _select(afft, cap)
        xs = _gather(rank, xn.reshape(b, t, d), cap, tl)
        o_exp = _ffn(xs, exp_w_gate, exp_w_up, exp_w_down, l)
        x3 = _combine(x_mid.reshape(b, t, d), rankc, aff.reshape(b, t, LANE),
                      o_exp.reshape(b, N_EXPERTS * cap, d), norm_final[None, :], cap, l == depth - 1, tl)
        x2 = x3.reshape(n, d)
    return x2.reshape(b, t, d)
```

```python
import functools

import jax
import jax.numpy as jnp
from jax import lax
from jax.experimental import pallas as pl
from jax.experimental.pallas import tpu as pltpu

F32 = jnp.float32
BF16 = jnp.bfloat16

D_MODEL = 1024
GROUP_W = 256
N_HEADS = 4
HEAD_DIM = 64
ROPE_BASE = 10000.0
RWKV_GN_EPS = 64e-5
RGLRU_C = 8.0
GLA_KDIM = 128
GLA_HEAD_K = 32
GLA_TAU = 16.0
N_EXPERTS = 16
EC_CAPACITY_FACTOR = 2
NORM_EPS = 1e-6

LANE = 128
SUBLANE = 8
CHUNK = 64
VMEM_LIMIT = 56 * 1024 * 1024

COL_RET, COL_RWKV, COL_GLA, COL_LRU = 0, 1024, 2048, 3072
IN_COLS_PAD = 3584


def _tiles(T):
    return dict(
        tm_proj=min(512, T), tn_proj=512,
        tb_ret=min(512, T), tb_gla=min(512, T), tb_lru=min(512, T), tb_rwkv=min(1024, T),
        tm_out=min(512, T), tm_comb=min(1024, T), tk_gather=min(1024, T), gather_group=4, ffn_seqs=2, rwkv_group=8,
    )


def _params(*sem):
    return pltpu.CompilerParams(dimension_semantics=sem, vmem_limit_bytes=VMEM_LIMIT)


def _bf(x):
    return x.astype(BF16)


def _dot(a, b):
    return jnp.dot(_bf(a), _bf(b), preferred_element_type=F32)


def _dot_nt(a, b):
    return lax.dot_general(_bf(a), _bf(b), (((1,), (1,)), ((), ())), preferred_element_type=F32)


def _dot_tn(a, b):
    return lax.dot_general(_bf(a), _bf(b), (((0,), (0,)), ((), ())), preferred_element_type=F32)


def _split2(a):
    hi = _bf(a)
    lo = _bf(a - hi.astype(F32))
    return hi, lo


def _split3(a):
    hi = _bf(a)
    r = a - hi.astype(F32)
    mid = _bf(r)
    lo = _bf(r - mid.astype(F32))
    return hi, mid, lo


def _dot_exact_lhs(m, a):
    hi, mid, lo = _split3(a)
    return (jnp.dot(m, hi, preferred_element_type=F32) + jnp.dot(m, mid, preferred_element_type=F32)
            + jnp.dot(m, lo, preferred_element_type=F32))


def _dot_exact_rhs(a, m):
    hi, lo = _split2(a)
    return jnp.dot(hi, m, preferred_element_type=F32) + jnp.dot(lo, m, preferred_element_type=F32)


def _dot_hp(a, b):
    ah, al = _split2(a)
    bh, bl = _split2(b)
    return (jnp.dot(ah, bh, preferred_element_type=F32) + jnp.dot(ah, bl, preferred_element_type=F32)
            + jnp.dot(al, bh, preferred_element_type=F32))


def _dot_hp_nt(a, b):
    ah, al = _split2(a)
    bh, bl = _split2(b)
    dn = (((1,), (1,)), ((), ()))
    return (lax.dot_general(ah, bh, dn, preferred_element_type=F32)
            + lax.dot_general(ah, bl, dn, preferred_element_type=F32)
            + lax.dot_general(al, bh, dn, preferred_element_type=F32))


def _sigmoid(x):
    return 1.0 / (1.0 + jnp.exp(-x))


def _softplus(x):
    return jnp.maximum(x, 0.0) + jnp.log(1.0 + jnp.exp(-jnp.abs(x)))


def _iota(shape, dim):
    return lax.broadcasted_iota(jnp.int32, shape, dim)


def _head_mask(h, width, hd):
    return (_iota((1, width), 1) // hd == h).astype(F32)


def _stack_heads(z, hd):
    w = z.shape[1]
    return jnp.concatenate([z * _head_mask(h, w, hd) for h in range(N_HEADS)], axis=0)


def _block_diag_bf(z):
    w = z.shape[1]
    lane_head = _iota((1, w), 1) // (w // N_HEADS)
    zero = jnp.zeros_like(z)
    return jnp.concatenate([jnp.where(lane_head == h, z, zero) for h in range(N_HEADS)], axis=0)


def _unstack_sum(z, c):
    return z[0:c] + z[c:2 * c] + z[2 * c:3 * c] + z[3 * c:4 * c]


def _tri_mask(n, lower, strict):
    i, j = _iota((n, n), 0), _iota((n, n), 1)
    if lower:
        return (j < i) if strict else (j <= i)
    return (j > i) if strict else (j >= i)


def _block_tri_mask(c, lower, strict):
    n = N_HEADS * c
    i, j = _iota((n, n), 0), _iota((n, n), 1)
    same = (i // c) == (j // c)
    ii, jj = i % c, j % c
    if lower:
        t = (jj < ii) if strict else (jj <= ii)
    else:
        t = (jj > ii) if strict else (jj >= ii)
    return same & t


def _head_stats(y, avg):
    return _dot_exact_rhs(y, avg)


def _shift_rows(z, edge, fwd):
    n = z.shape[0]
    row = _iota((n, 1), 0)
    if fwd:
        return jnp.where(row == 0, edge, pltpu.roll(z, 1, 0))
    return jnp.where(row == n - 1, edge, pltpu.roll(z, n - 1, 0))


def _in_proj_kernel(x_ref, g_ref, w_ref, o_ref, *, tn):
    x = x_ref[...]
    ms = jnp.mean(x * x, axis=-1, keepdims=True)
    xn = _bf(x * lax.rsqrt(ms + NORM_EPS) * g_ref[...])
    for c in range(o_ref.shape[1] // tn):
        o_ref[:, c * tn:(c + 1) * tn] = jnp.dot(xn, w_ref[:, c * tn:(c + 1) * tn], preferred_element_type=F32)


def _in_proj(x2, gain, w_bf, tl):
    n, d = x2.shape
    tm, tn = tl["tm_proj"], tl["tn_proj"]
    nc = w_bf.shape[1]
    return pl.pallas_call(
        functools.partial(_in_proj_kernel, tn=tn),
        grid=(n // tm,),
        in_specs=[pl.BlockSpec((tm, d), lambda i: (i, 0)),
                  pl.BlockSpec((1, d), lambda i: (0, 0)),
                  pl.BlockSpec((d, nc), lambda i: (0, 0), pipeline_mode=pl.Buffered(1))],
        out_specs=pl.BlockSpec((tm, nc), lambda i: (i, 0)),
        out_shape=jax.ShapeDtypeStruct((n, nc), F32),
        compiler_params=_params("parallel"),
        name="in_proj",
    )(x2, gain, w_bf)


def _rope_kernel(pos_ref, inv_ref, sgn_ref, cos_ref, sin_ref):
    ang = pos_ref[...].astype(F32) * inv_ref[...]
    cos_ref[...] = jnp.cos(ang)
    sin_ref[...] = jnp.sin(ang) * sgn_ref[...]


def _rope_tables(positions):
    b, t = positions.shape
    n = b * t
    tm = min(1024, n)
    lane = jnp.arange(LANE)
    inv = jnp.power(ROPE_BASE, -jnp.arange(0, HEAD_DIM, 2, dtype=F32) / HEAD_DIM)
    inv_l = inv[lane % (HEAD_DIM // 2)][None, :]
    sgn = jnp.where(lane % HEAD_DIM < HEAD_DIM // 2, -1.0, 1.0).astype(F32)[None, :]
    cos, sin = pl.pallas_call(
        _rope_kernel,
        grid=(n // tm,),
        in_specs=[pl.BlockSpec((tm, 1), lambda i: (i, 0)),
                  pl.BlockSpec((1, LANE), lambda i: (0, 0)),
                  pl.BlockSpec((1, LANE), lambda i: (0, 0))],
        out_specs=[pl.BlockSpec((tm, LANE), lambda i: (i, 0))] * 2,
        out_shape=[jax.ShapeDtypeStruct((n, LANE), F32)] * 2,
        compiler_params=_params("parallel"),
        name="rope_tables",
    )(positions.reshape(n, 1), inv_l, sgn)
    return cos.reshape(b, t, LANE), sin.reshape(b, t, LANE)


def _ret_kernel(x_ref, cos_ref, sin_ref, lgl_ref, gn_ref, avg_ref, *rest, TB, fwd):
    if fwd:
        yo_ref, o_ref, s_ref = rest
    else:
        o_ref, s_ref = rest
    W, C = GROUP_W, CHUNK

    @pl.when(pl.program_id(1) == 0)
    def _init():
        s_ref[...] = jnp.zeros_like(s_ref)

    x = x_ref[...]
    q, k, v, g = x[:, 0:W], x[:, W:2 * W], x[:, 2 * W:3 * W], x[:, 3 * W:4 * W]
    cos = jnp.concatenate([cos_ref[...]] * (W // LANE), axis=1)
    sin = jnp.concatenate([sin_ref[...]] * (W // LANE), axis=1)
    first_half = (_iota((1, W), 1) % HEAD_DIM) < (HEAD_DIM // 2)

    def rot(z):
        swapped = jnp.where(first_half, pltpu.roll(z, W - HEAD_DIM // 2, 1), pltpu.roll(z, HEAD_DIM // 2, 1))
        return z * cos + swapped * sin

    q = rot(q)
    k = rot(k) * (HEAD_DIM ** -0.5)
    lgl = lgl_ref[...]
    row_t = _iota((C, W), 0)
    lane_s = _iota((C, W), 1) % C
    rel = ((row_t - lane_s) if fwd else (lane_s - row_t)).astype(F32)
    dmask = jnp.where(rel >= 0, jnp.exp(lgl * jnp.maximum(rel, 0.0)), 0.0)
    idx = _iota((C, 1), 0).astype(F32)
    if fwd:
        zeta = jnp.exp(lgl * (C - 1.0 - idx))
        xi = jnp.exp(lgl * (idx + 1.0))
    else:
        zeta = jnp.exp(lgl * idx)
        xi = jnp.exp(lgl * (C - idx))
    cd = jnp.exp(lgl * float(C))
    bmask = ((_iota((W, W), 0) // HEAD_DIM) == (_iota((W, W), 1) // HEAD_DIM)).astype(F32)

    n_chunks = TB // C
    order = range(n_chunks) if fwd else range(n_chunks - 1, -1, -1)
    sls = [slice(c * C, (c + 1) * C) for c in order]
    kbs = [_bf(k[sl]) for sl in sls]
    vbs = [_bf(v[sl]) for sl in sls]
    scs = [_dot_nt(q[sl], _block_diag_bf(kb)) * dmask for sl, kb in zip(sls, kbs)]
    yis = [_dot(sc, _block_diag_bf(vb)) for sc, vb in zip(scs, vbs)]
    upds = [_dot_tn(k[sl] * zeta, vb) * bmask for sl, vb in zip(sls, vbs)]
    state = s_ref[...]
    for j, sl in enumerate(sls):
        o_ref[sl, :] = yis[j] + _dot(q[sl] * xi, state)
        state = state * cd + upds[j]
    s_ref[...] = state

    if fwd:
        yt = o_ref[...] + yo_ref[...]
        avg = avg_ref[...]
        yc = yt - _head_stats(yt, avg)
        var = _head_stats(yc * yc, avg)
        o_ref[...] = yc * lax.rsqrt(var + NORM_EPS) * gn_ref[...] * (g * _sigmoid(g))


def _retention(proj3, cos, sin, log_decay, gn_w, avg, tl):
    b, t, _ = proj3.shape
    tb = tl["tb_ret"]
    nb = t // tb

    def call(fwd, lg, extra):
        def cm(i):
            return i if fwd else nb - 1 - i

        in_specs = [
            pl.BlockSpec((None, tb, 1024), lambda bi, i: (bi, cm(i), COL_RET // 1024)),
            pl.BlockSpec((None, tb, LANE), lambda bi, i: (bi, cm(i), 0)),
            pl.BlockSpec((None, tb, LANE), lambda bi, i: (bi, cm(i), 0)),
            pl.BlockSpec((1, GROUP_W), lambda bi, i: (0, 0)),
            pl.BlockSpec((1, GROUP_W), lambda bi, i: (0, 0)),
            pl.BlockSpec((GROUP_W, GROUP_W), lambda bi, i: (0, 0)),
        ]
        args = [proj3, cos, sin, jnp.repeat(lg, HEAD_DIM)[None, :], gn_w[None, :], avg]
        if fwd:
            in_specs.append(pl.BlockSpec((None, tb, GROUP_W), lambda bi, i: (bi, cm(i), 0)))
            args.append(extra)
        return pl.pallas_call(
            functools.partial(_ret_kernel, TB=tb, fwd=fwd),
            grid=(b, nb),
            in_specs=in_specs,
            out_specs=pl.BlockSpec((None, tb, GROUP_W), lambda bi, i: (bi, cm(i), 0)),
            out_shape=jax.ShapeDtypeStruct((b, t, GROUP_W), F32),
            scratch_shapes=[pltpu.VMEM((GROUP_W, GROUP_W), F32)],
            compiler_params=_params("parallel", "arbitrary"),
            name="retention_fwd" if fwd else "retention_bwd",
        )(*args)

    y_b = call(False, log_decay[1], None)
    return call(True, log_decay[0], y_b)


def _gla_kernel(x_ref, aup_ref, ab_ref, gn_ref, avg_ref, *rest, TB, fwd):
    if fwd:
        yo_ref, o_ref, st_ref = rest
    else:
        o_ref, st_ref = rest
    W, KD, C = GROUP_W, GLA_KDIM, CHUNK

    @pl.when(pl.program_id(1) == 0)
    def _init():
        st_ref[...] = jnp.zeros_like(st_ref)

    x = x_ref[...]
    q = x[:, 0:KD] * (GLA_HEAD_K ** -0.5)
    k = x[:, KD:2 * KD]
    v = x[:, 2 * KD:2 * KD + W]
    og = x[:, 2 * KD + W:2 * KD + 2 * W]
    xa = x[:, 2 * KD + 2 * W:2 * KD + 2 * W + LANE]
    z = _dot_hp(xa, aup_ref[...]) + ab_ref[...]
    la = -_softplus(-z) * (1.0 / GLA_TAU)
    tri = jnp.where(_tri_mask(C, fwd, False), 1.0, 0.0).astype(BF16)
    row_t = _iota((C, W), 0)
    lane_s = _iota((C, W), 1) % C
    m_incl = (lane_s <= row_t) if fwd else (lane_s >= row_t)
    bmask_t = ((_iota((W, KD), 0) // HEAD_DIM) == (_iota((W, KD), 1) // GLA_HEAD_K)).astype(F32)

    n_chunks = TB // C
    order = range(n_chunks) if fwd else range(n_chunks - 1, -1, -1)
    sls = [slice(c * C, (c + 1) * C) for c in order]
    bcs = [_dot_exact_lhs(tri, la[sl]) for sl in sls]
    bls = [bc[C - 1:C] if fwd else bc[0:1] for bc in bcs]
    qds = [q[sl] * jnp.exp(bc) for sl, bc in zip(sls, bcs)]
    kds = [_bf(k[sl] * jnp.exp(-bc)) for sl, bc in zip(sls, bcs)]
    khs = [k[sl] * jnp.exp(bl - bc) for sl, bc, bl in zip(sls, bcs, bls)]
    vbs = [_bf(v[sl]) for sl in sls]
    scs = [jnp.where(m_incl, _dot_nt(qd, _block_diag_bf(kd)), 0.0) for qd, kd in zip(qds, kds)]
    yis = [_dot(sc, _block_diag_bf(vb)) for sc, vb in zip(scs, vbs)]
    upds = [_dot_tn(vb, kh) * bmask_t for vb, kh in zip(vbs, khs)]
    st = st_ref[...]
    for j, sl in enumerate(sls):
        o_ref[sl, :] = yis[j] + _dot_nt(qds[j], st)
        st = st * jnp.exp(bls[j]) + upds[j]
    st_ref[...] = st

    if fwd:
        yt = o_ref[...] + yo_ref[...]
        var = _head_stats(yt * yt, avg_ref[...])
        o_ref[...] = yt * lax.rsqrt(var + NORM_EPS) * gn_ref[...] * (og * _sigmoid(og))


def _gla(proj3, alpha_up, alpha_b, gn_w, avg, tl):
    b, t, _ = proj3.shape
    tb = tl["tb_gla"]
    nb = t // tb

    def call(fwd, d, extra):
        def bm(i):
            return i if fwd else nb - 1 - i

        aup = jnp.zeros((LANE, GLA_KDIM), F32).at[:alpha_up.shape[1]].set(alpha_up[d])
        in_specs = [
            pl.BlockSpec((None, tb, 1024), lambda bi, i: (bi, bm(i), COL_GLA // 1024)),
            pl.BlockSpec((LANE, GLA_KDIM), lambda bi, i: (0, 0)),
            pl.BlockSpec((1, GLA_KDIM), lambda bi, i: (0, 0)),
            pl.BlockSpec((1, GROUP_W), lambda bi, i: (0, 0)),
            pl.BlockSpec((GROUP_W, GROUP_W), lambda bi, i: (0, 0)),
        ]
        args = [proj3, aup, alpha_b[d][None, :], gn_w[None, :], avg]
        if fwd:
            in_specs.append(pl.BlockSpec((None, tb, GROUP_W), lambda bi, i: (bi, bm(i), 0)))
            args.append(extra)
        return pl.pallas_call(
            functools.partial(_gla_kernel, TB=tb, fwd=fwd),
            grid=(b, nb),
            in_specs=in_specs,
            out_specs=pl.BlockSpec((None, tb, GROUP_W), lambda bi, i: (bi, bm(i), 0)),
            out_shape=jax.ShapeDtypeStruct((b, t, GROUP_W), F32),
            scratch_shapes=[pltpu.VMEM((GROUP_W, GLA_KDIM), F32)],
            compiler_params=_params("parallel", "arbitrary"),
            name="gla_fwd" if fwd else "gla_bwd",
        )(*args)

    y_b = call(False, 1, None)
    return call(True, 0, y_b)


def _lru_kernel(x_ref, hp_ref, hn_ref, cw_ref, cb_ref, gw_ref, gb_ref, lam_ref, *rest, TB, NB, fwd):
    if fwd:
        ho_ref, o_ref, h_ref = rest
    else:
        o_ref, h_ref = rest
    W = GROUP_W
    i = pl.program_id(1)
    blk = i if fwd else NB - 1 - i

    @pl.when(i == 0)
    def _init():
        h_ref[...] = jnp.zeros_like(h_ref)

    x = x_ref[...]
    xr, gate = x[:, 0:W], x[:, W:2 * W]
    has_prev = (blk > 0).astype(F32)
    has_next = (blk < NB - 1).astype(F32)
    prev = hp_ref[...][:, 0:W] * has_prev
    nxt = hn_ref[...][:, 0:W] * has_next
    row = _iota((TB, 1), 0)
    x_m1 = jnp.where(row == 0, prev[7:8], pltpu.roll(xr, 1, 0))
    x_m2 = jnp.where(row == 0, prev[6:7], jnp.where(row == 1, prev[7:8], pltpu.roll(xr, 2, 0)))
    x_p1 = jnp.where(row == TB - 1, nxt[0:1], pltpu.roll(xr, TB - 1, 0))
    cw = cw_ref[...]
    xc = x_m2 * cw[0:1] + x_m1 * cw[1:2] + xr * cw[2:3] + x_p1 * cw[3:4] + cb_ref[...]

    gx = _dot(xc, gw_ref[...]) + gb_ref[...]
    rec = _sigmoid(gx[:, 0:W])
    ing = _sigmoid(gx[:, W:2 * W])
    log_a = -RGLRU_C * rec * _softplus(-lam_ref[...])
    a = jnp.exp(log_a)
    bx = jnp.sqrt(-jnp.tanh(log_a) * (a * a + 1.0)) * (ing * xc)

    s = 1
    while s < TB:
        if s % SUBLANE == 0:
            ones, zeros = jnp.ones((s, W), F32), jnp.zeros((s, W), F32)
            if fwd:
                a_sh = jnp.concatenate([ones, a[0:TB - s]], axis=0)
                b_sh = jnp.concatenate([zeros, bx[0:TB - s]], axis=0)
            else:
                a_sh = jnp.concatenate([a[s:TB], ones], axis=0)
                b_sh = jnp.concatenate([bx[s:TB], zeros], axis=0)
        elif fwd:
            keep = row >= s
            a_sh = jnp.where(keep, pltpu.roll(a, s, 0), 1.0)
            b_sh = jnp.where(keep, pltpu.roll(bx, s, 0), 0.0)
        else:
            keep = row < TB - s
            a_sh = jnp.where(keep, pltpu.roll(a, TB - s, 0), 1.0)
            b_sh = jnp.where(keep, pltpu.roll(bx, TB - s, 0), 0.0)
        bx = a * b_sh + bx
        a = a * a_sh
        s *= 2
    h = a * h_ref[...] + bx
    h_ref[...] = h[TB - 1:TB] if fwd else h[0:1]

    if fwd:
        o_ref[...] = (h + ho_ref[...]) * jax.nn.gelu(gate, approximate=True)
    else:
        o_ref[...] = h


def _block_diag_heads(w):
    h, d, e = w.shape
    eye = jnp.eye(h, dtype=w.dtype)
    return jnp.einsum('hde,hg->hdge', w, eye).reshape(h * d, h * e)


def _rglru(proj3, conv_w, conv_b, gate_w, gate_b, lam, tl):
    b, t, _ = proj3.shape
    tb = tl["tb_lru"]
    nb = t // tb
    r8 = tb // SUBLANE
    last8 = t // SUBLANE - 1
    colb = COL_LRU // 512

    def call(fwd, d, extra):
        def bm(i):
            return i if fwd else nb - 1 - i

        gw = jnp.concatenate([_block_diag_heads(gate_w[d, 0]), _block_diag_heads(gate_w[d, 1])], axis=1)
        gb = jnp.concatenate([gate_b[d, 0], gate_b[d, 1]])[None, :]
        in_specs = [
            pl.BlockSpec((None, tb, 512), lambda bi, i: (bi, bm(i), colb)),
            pl.BlockSpec((None, SUBLANE, 512), lambda bi, i: (bi, jnp.maximum(bm(i) * r8 - 1, 0), colb)),
            pl.BlockSpec((None, SUBLANE, 512), lambda bi, i: (bi, jnp.minimum((bm(i) + 1) * r8, last8), colb)),
            pl.BlockSpec((4, GROUP_W), lambda bi, i: (0, 0)),
            pl.BlockSpec((1, GROUP_W), lambda bi, i: (0, 0)),
            pl.BlockSpec((GROUP_W, 2 * GROUP_W), lambda bi, i: (0, 0)),
            pl.BlockSpec((1, 2 * GROUP_W), lambda bi, i: (0, 0)),
            pl.BlockSpec((1, GROUP_W), lambda bi, i: (0, 0)),
        ]
        args = [proj3, proj3, proj3, conv_w, conv_b[None, :], _bf(gw), gb, lam[d][None, :]]
        if fwd:
            in_specs.append(pl.BlockSpec((None, tb, GROUP_W), lambda bi, i: (bi, bm(i), 0)))
            args.append(extra)
        return pl.pallas_call(
            functools.partial(_lru_kernel, TB=tb, NB=nb, fwd=fwd),
            grid=(b, nb),
            in_specs=in_specs,
            out_specs=pl.BlockSpec((None, tb, GROUP_W), lambda bi, i: (bi, bm(i), 0)),
            out_shape=jax.ShapeDtypeStruct((b, t, GROUP_W), F32),
            scratch_shapes=[pltpu.VMEM((1, GROUP_W), F32)],
            compiler_params=_params("parallel", "arbitrary"),
            name="rglru_fwd" if fwd else "rglru_bwd",
        )(*args)

    h_b = call(False, 1, None)
    return call(True, 0, h_b)


def _rwkv_kernel(x_ref, halo_ref, mu_ref, lora_ref, w0_ref, a0_ref, kk_ref, ka_ref, rk_ref, ones_ref,
                 *rest, TB, NB, fwd, GROUP):
    if fwd:
        gup_ref, gn_ref, avg_ref, yo_ref, bo_ref, o_ref, s_ref = rest
    else:
        y_ref, b_ref, s_ref = rest
    W, C = GROUP_W, CHUNK
    i = pl.program_id(1)
    blk = i if fwd else NB - 1 - i

    @pl.when(i == 0)
    def _init():
        s_ref[...] = jnp.zeros_like(s_ref)

    x = x_ref[...]
    NS = 3 * W + LANE
    zs = x[:, 0:NS]
    halo = halo_ref[...][:, 0:NS]
    if fwd:
        edge = halo[SUBLANE - 1:SUBLANE] * (blk > 0).astype(F32)
    else:
        edge = halo[0:1] * (blk < NB - 1).astype(F32)
    zs = zs + mu_ref[...] * (_shift_rows(zs, edge, fwd) - zs)
    r, k, v, xwa = zs[:, 0:W], zs[:, W:2 * W], zs[:, 2 * W:3 * W], zs[:, 3 * W:NS]
    lane_l = _iota((1, LANE), 1)
    xwa = jnp.where(lane_l < LANE // 2, jnp.tanh(xwa), xwa)
    lo = _dot_hp(xwa, lora_ref[...])
    w_log = -_softplus(-(w0_ref[...] + lo[:, 0:W])) - 0.5
    lw = -jnp.exp(w_log)
    a = _sigmoid(a0_ref[...] + lo[:, W:2 * W])
    ones_bd = ones_ref[...]
    kk = k * kk_ref[...]
    kk = kk / jnp.maximum(jnp.sqrt(_dot_exact_rhs(kk * kk, ones_bd)), 1e-12)
    k = k * (1.0 + (a - 1.0) * ka_ref[...])
    bonus = _dot_exact_rhs(r * k * rk_ref[...], ones_bd) * v
    am = -kk
    bm = kk * a

    tri = jnp.where(_tri_mask(C, fwd, False), 1.0, 0.0).astype(BF16)
    row_t = _iota((C, W), 0)
    lane_s = _iota((C, W), 1) % C
    m_strict = (lane_s < row_t) if fwd else (lane_s > row_t)
    m_incl = (lane_s <= row_t) if fwd else (lane_s >= row_t)
    eye_c = (lane_s == row_t).astype(F32)
    bmask = (_iota((W, W), 0) // HEAD_DIM) == (_iota((W, W), 1) // HEAD_DIM)

    def bd(z):
        return _block_diag_bf(_bf(z))

    n_chunks = TB // C
    order = range(n_chunks) if fwd else range(n_chunks - 1, -1, -1)
    def state_free_part(sls):
        cums = [_dot_exact_lhs(tri, lw[sl]) for sl in sls]
        tots = [cum[C - 1:C] if fwd else cum[0:1] for cum in cums]
        e_negs = [jnp.exp(-cum) for cum in cums]
        e_ends = [jnp.exp(tot - cum) for tot, cum in zip(tots, cums)]
        a_ts = [am[sl] * jnp.exp(cum - lw[sl]) for sl, cum in zip(sls, cums)]
        r_ts = [r[sl] * jnp.exp(cum) for sl, cum in zip(sls, cums)]
        yield None
        gs = [_dot_nt(jnp.concatenate([a_t, r_t], axis=0),
                      jnp.concatenate([bd(bm[sl] * e_neg), bd(k[sl] * e_neg)], axis=0))
              for sl, a_t, r_t, e_neg in zip(sls, a_ts, r_ts, e_negs)]
        yield None
        a_abs = [jnp.where(m_strict, g[0:C, 0:W], 0.0) for g in gs]
        a_aks = [jnp.where(m_strict, g[0:C, W:2 * W], 0.0) for g in gs]
        a_rbs = [jnp.where(m_incl, g[C:2 * C, 0:W], 0.0) for g in gs]
        a_rks = [jnp.where(m_incl, g[C:2 * C, W:2 * W], 0.0) for g in gs]

        tinvs = [eye_c + a_ab for a_ab in a_abs]
        ps = [_dot(a_ab, bd(a_ab)) for a_ab in a_abs]
        avs = [_dot(jnp.concatenate([a_ak, a_rk], axis=0), bd(v[sl]))
               for a_ak, a_rk, sl in zip(a_aks, a_rks, sls)]
        yield None
        n_terms = 2
        while 2 * n_terms < C:
            xps = [_dot(jnp.concatenate([tinv, p], axis=0), bd(p)) for p, tinv in zip(ps, tinvs)]
            tinvs = [tinv + xp[0:C] for tinv, xp in zip(tinvs, xps)]
            ps = [xp[C:2 * C] for xp in xps]
            n_terms *= 2
            yield None
        tinvs = [tinv + _dot(tinv, bd(p)) for p, tinv in zip(ps, tinvs)]
        yield None
        w12s = [_dot(tinv, jnp.concatenate([bd(a_t), bd(av[0:C])], axis=1))
                for tinv, a_t, av in zip(tinvs, a_ts, avs)]
        yield None
        qys = [_dot(a_rb, jnp.concatenate([bd(w12[:, 0:W]), bd(w12[:, W:2 * W])], axis=1))
               for a_rb, w12 in zip(a_rbs, w12s)]
        yield [dict(sl=sl, w1=w12[:, 0:W], w2=w12[:, W:2 * W], qp=r_t + qy[:, 0:W],
                    y0=av[C:2 * C] + qy[:, W:2 * W], decay=jnp.exp(tot),
                    kb=jnp.concatenate([k[sl] * e_end, bm[sl] * e_end], axis=0))
               for sl, w12, r_t, qy, av, tot, e_end in zip(sls, w12s, r_ts, qys, avs, tots, e_ends)]

    def recur(state, p_):
        sl = p_["sl"]
        uy = _dot_nt(jnp.concatenate([p_["w1"], p_["qp"]], axis=0), state)
        u = uy[0:C] + p_["w2"]
        y = uy[C:2 * C] + p_["y0"]
        upd = _dot_tn(jnp.concatenate([v[sl], u], axis=0), p_["kb"])
        if fwd:
            o_ref[sl, :] = y
        else:
            y_ref[sl, :] = y
        return state * p_["decay"] + jnp.where(bmask, upd, 0.0)

    sls_all = [slice(c * C, (c + 1) * C) for c in order]
    group = GROUP
    state = s_ref[...]
    pending = []
    for g0 in range(0, n_chunks, group):
        done = None
        for n_stage, out in enumerate(state_free_part(sls_all[g0:g0 + group])):
            if out is not None:
                done = out
            elif pending and n_stage % 2 == 1:
                state = recur(state, pending.pop(0))
        while pending:
            state = recur(state, pending.pop(0))
        pending = done
    while pending:
        state = recur(state, pending.pop(0))
    s_ref[...] = state

    if fwd:
        yt = o_ref[...] + yo_ref[...]
        avg = avg_ref[...]
        yc = yt - _head_stats(yt, avg)
        var = _head_stats(yc * yc, avg)
        o = yc * lax.rsqrt(var + RWKV_GN_EPS) * gn_ref[...] + bonus + bo_ref[...]
        xg = x[:, NS:NS + LANE]
        o_ref[...] = o * _dot_hp(_sigmoid(xg), gup_ref[...])
    else:
        b_ref[...] = bonus


def _rwkv7(proj3, mu_rkv, mu_w, mu_a, w0, w_up, a0, a_up, g_up, k_k, k_a, r_k, gn_w, avg, ones_bd, tl):
    b, t, _ = proj3.shape
    tb = tl["tb_rwkv"]
    nb = t // tb
    r8 = tb // SUBLANE
    last8 = t // SUBLANE - 1
    colb = COL_RWKV // 1024
    W = GROUP_W
    mu = jnp.concatenate([mu_rkv.reshape(-1), mu_w, mu_a])[None, :]
    nl = w_up.shape[1]

    def call(fwd, d, extra):
        def bm(i):
            return i if fwd else nb - 1 - i

        lora = jnp.zeros((LANE, 2 * W), F32).at[0:nl, 0:W].set(w_up[d]).at[nl:2 * nl, W:2 * W].set(a_up[d])
        if fwd:
            halo_map = lambda bi, i: (bi, jnp.maximum(bm(i) * r8 - 1, 0), colb)
        else:
            halo_map = lambda bi, i: (bi, jnp.minimum((bm(i) + 1) * r8, last8), colb)
        row = lambda n: pl.BlockSpec((1, n), lambda bi, i: (0, 0))
        in_specs = [
            pl.BlockSpec((None, tb, 1024), lambda bi, i: (bi, bm(i), colb)),
            pl.BlockSpec((None, SUBLANE, 1024), halo_map),
            row(3 * W + LANE),
            pl.BlockSpec((LANE, 2 * W), lambda bi, i: (0, 0)),
            row(W), row(W), row(W), row(W), row(W),
            pl.BlockSpec((W, W), lambda bi, i: (0, 0)),
        ]
        args = [proj3, proj3, mu, lora, w0[d][None, :], a0[d][None, :], k_k[None, :], k_a[None, :],
                r_k.reshape(1, W), ones_bd]
        blk = pl.BlockSpec((None, tb, W), lambda bi, i: (bi, bm(i), 0))
        if fwd:
            in_specs += [pl.BlockSpec((LANE, W), lambda bi, i: (0, 0)), row(W),
                         pl.BlockSpec((W, W), lambda bi, i: (0, 0)), blk, blk]
            args += [g_up, gn_w[None, :], avg, extra[0], extra[1]]
            out_specs = blk
            out_shape = jax.ShapeDtypeStruct((b, t, W), F32)
        else:
            out_specs = [blk, blk]
            out_shape = [jax.ShapeDtypeStruct((b, t, W), F32)] * 2
        return pl.pallas_call(
            functools.partial(_rwkv_kernel, TB=tb, NB=nb, fwd=fwd, GROUP=tl["rwkv_group"]),
            grid=(b, nb),
            in_specs=in_specs,
            out_specs=out_specs,
            out_shape=out_shape,
            scratch_shapes=[pltpu.VMEM((W, W), F32)],
            compiler_params=_params("parallel", "arbitrary"),
            name="rwkv7_fwd" if fwd else "rwkv7_bwd",
        )(*args)

    y_b, bonus_b = call(False, 1, None)
    return call(True, 0, (y_b, bonus_b))


def _out_proj_kernel(x_ref, o1_ref, o2_ref, o3_ref, o4_ref, w_ref, g_ref, rw_ref, rb_ref, rwt_ref, rbt_ref,
                     xo_ref, xn_ref, aff_ref, afft_ref):
    W = GROUP_W
    w = w_ref[...]
    acc = x_ref[...]
    for gi, o_ref in enumerate((o1_ref, o2_ref, o3_ref, o4_ref)):
        acc = acc + jnp.dot(_bf(o_ref[...]), w[gi * W:(gi + 1) * W], preferred_element_type=F32)
    xo_ref[...] = acc
    ms = jnp.mean(acc * acc, axis=-1, keepdims=True)
    xn = acc * lax.rsqrt(ms + NORM_EPS) * g_ref[...]
    xn_ref[...] = _bf(xn)
    logits = _dot_hp(xn, rw_ref[...]) + rb_ref[...]
    logits = logits - jnp.max(logits, axis=-1, keepdims=True)
    e = jnp.exp(logits)
    aff_ref[...] = e / jnp.sum(e, axis=-1, keepdims=True)
    lt = _dot_hp_nt(rwt_ref[...], xn) + rbt_ref[...]
    lt = lt - jnp.max(lt, axis=0, keepdims=True)
    et = jnp.exp(lt)
    afft_ref[...] = et / jnp.sum(et, axis=0, keepdims=True)


def _out_proj(x2, outs, w_out_bf, gain, router_w, router_b, b, t, tl):
    n, d = x2.shape
    tm = tl["tm_out"]
    tpb = t // tm
    E = N_EXPERTS
    rw = jnp.zeros((d, LANE), F32).at[:, :E].set(router_w)
    rb = jnp.full((1, LANE), -1e30, F32).at[0, :E].set(router_b)
    full = lambda r, c: pl.BlockSpec((r, c), lambda i: (0, 0))
    tile = lambda c: pl.BlockSpec((tm, c), lambda i: (i, 0))
    return pl.pallas_call(
        _out_proj_kernel,
        grid=(n // tm,),
        in_specs=[tile(d)] + [tile(GROUP_W)] * 4 + [full(d, d), full(1, d), full(d, LANE), full(1, LANE),
                                                     full(E, d), full(E, 1)],
        out_specs=[tile(d), tile(d), tile(LANE),
                   pl.BlockSpec((None, E, tm), lambda i: (i // tpb, 0, i % tpb))],
        out_shape=[jax.ShapeDtypeStruct((n, d), F32), jax.ShapeDtypeStruct((n, d), BF16),
                   jax.ShapeDtypeStruct((n, LANE), F32), jax.ShapeDtypeStruct((b, E, t), F32)],
        compiler_params=_params("parallel"),
        name="out_proj_router",
    )(x2, *outs, w_out_bf, gain, rw, rb, router_w.T, router_b[:, None])


def _prefix_count(m, triu):
    e, t = m.shape
    nblk = t // LANE
    stacked = jnp.concatenate([m[:, j * LANE:(j + 1) * LANE] for j in range(nblk)], axis=0)
    incl = jnp.dot(_bf(stacked), triu, preferred_element_type=F32)
    pieces = []
    off = jnp.zeros((e, 1), F32)
    for j in range(nblk):
        blk = incl[j * e:(j + 1) * e]
        pieces.append(blk - stacked[j * e:(j + 1) * e] + off)
        off = off + blk[:, LANE - 1:LANE]
    return jnp.concatenate(pieces, axis=1)


def _select_kernel(afft_ref, rank_ref, rankc_ref, *, cap):
    aff = afft_ref[...]
    e, t = aff.shape
    bits = lax.bitcast_convert_type(aff, jnp.int32)

    def body(i, prefix):
        cand = prefix | jnp.left_shift(jnp.int32(1), 30 - i)
        cnt = jnp.sum((bits >= cand).astype(F32), axis=1, keepdims=True)
        return jnp.where(cnt >= cap, cand, prefix)

    thr = lax.fori_loop(0, 31, body, jnp.zeros((e, 1), jnp.int32))
    gt = (bits > thr).astype(F32)
    eq = (bits == thr).astype(F32)
    need = cap - jnp.sum(gt, axis=1, keepdims=True)
    triu = jnp.where(_tri_mask(LANE, False, False), 1.0, 0.0).astype(BF16)
    sel = gt + eq * (_prefix_count(eq, triu) < need).astype(F32)
    rank = jnp.where(sel > 0.5, _prefix_count(sel, triu), -1.0)
    rank_ref[...] = rank
    padded = jnp.concatenate([rank, jnp.full((LANE - e, t), -1.0, F32)], axis=0)
    rankc_ref[...] = padded.T


def _select(afft, cap):
    b, e, t = afft.shape
    return pl.pallas_call(
        functools.partial(_select_kernel, cap=cap),
        grid=(b,),
        in_specs=[pl.BlockSpec((None, e, t), lambda i: (i, 0, 0))],
        out_specs=[pl.BlockSpec((None, e, t), lambda i: (i, 0, 0)),
                   pl.BlockSpec((None, t, LANE), lambda i: (i, 0, 0))],
        out_shape=[jax.ShapeDtypeStruct((b, e, t), F32), jax.ShapeDtypeStruct((b, t, LANE), F32)],
        compiler_params=_params("parallel"),
        name="expert_choice_select",
    )(afft)


def _gather_kernel(rank_ref, xn_ref, xs_ref, acc_ref, *, cap, tk, group):
    g0 = pl.program_id(1) * group
    t = xn_ref.shape[0]
    slot = _iota((cap, tk), 0).astype(F32)
    for j in range(t // tk):
        onehot = jnp.concatenate(
            [jnp.where(rank_ref[pl.ds(g0 + g, 1), j * tk:(j + 1) * tk] == slot, 1.0, 0.0).astype(BF16)
             for g in range(group)], axis=0)
        part = jnp.dot(onehot, xn_ref[j * tk:(j + 1) * tk, :], preferred_element_type=F32)
        if j == 0:
            acc_ref[...] = part
        else:
            acc_ref[...] += part
    for g in range(group):
        xs_ref[g] = _bf(acc_ref[g * cap:(g + 1) * cap, :])


def _gather(rank, xn3, cap, tl):
    b, e, t = rank.shape
    d = xn3.shape[2]
    group = tl["gather_group"]
    return pl.pallas_call(
        functools.partial(_gather_kernel, cap=cap, tk=tl["tk_gather"], group=group),
        grid=(b, e // group),
        in_specs=[pl.BlockSpec((None, e, t), lambda bi, gi: (bi, 0, 0)),
                  pl.BlockSpec((None, t, d), lambda bi, gi: (bi, 0, 0))],
        out_specs=pl.BlockSpec((None, group, cap, d), lambda bi, gi: (bi, gi, 0, 0)),
        out_shape=jax.ShapeDtypeStruct((b, e, cap, d), BF16),
        scratch_shapes=[pltpu.VMEM((group * cap, d), F32)],
        compiler_params=_params("parallel", "arbitrary"),
        name="moe_gather",
    )(rank, xn3)


def _ffn_kernel(xs_ref, wg_ref, wu_ref, wd_ref, o_ref, wgb_ref, wub_ref, wdb_ref):
    @pl.when(pl.program_id(1) == 0)
    def _():
        wgb_ref[...] = _bf(wg_ref[...])
        wub_ref[...] = _bf(wu_ref[...])
        wdb_ref[...] = _bf(wd_ref[...])

    nseq = xs_ref.shape[0]
    xs = [xs_ref[i] for i in range(nseq)]
    hg = [jnp.dot(x, wgb_ref[...], preferred_element_type=F32) for x in xs]
    hu = [jnp.dot(x, wub_ref[...], preferred_element_type=F32) for x in xs]
    hid = [_bf(g * _sigmoid(g) * u) for g, u in zip(hg, hu)]
    for i in range(nseq):
        o_ref[i] = _bf(jnp.dot(hid[i], wdb_ref[...], preferred_element_type=F32))


def _ffn(xs, w_gate, w_up, w_down, layer, tl):
    b, e, cap, d = xs.shape
    f = w_gate.shape[3]
    nseq = min(tl["ffn_seqs"], b)
    wspec = lambda r, c: pl.BlockSpec((None, None, r, c), lambda ei, bi: (layer, ei, 0, 0))
    xspec = pl.BlockSpec((nseq, None, cap, d), lambda ei, bi: (bi, ei, 0, 0))
    return pl.pallas_call(
        _ffn_kernel,
        grid=(e, b // nseq),
        in_specs=[xspec, wspec(d, f), wspec(d, f), wspec(f, d)],
        out_specs=xspec,
        out_shape=jax.ShapeDtypeStruct((b, e, cap, d), BF16),
        scratch_shapes=[pltpu.VMEM((d, f), BF16), pltpu.VMEM((d, f), BF16), pltpu.VMEM((f, d), BF16)],
        compiler_params=_params("arbitrary", "arbitrary"),
        name="moe_ffn",
    )(xs, w_gate, w_up, w_down)


def _combine_kernel(x_ref, rankc_ref, aff_ref, o_ref, g_ref, out_ref, *, cap, final_norm):
    rankc = rankc_ref[...]
    aff = aff_ref[...]
    tm = x_ref.shape[0]
    slot = _iota((tm, cap), 1).astype(F32)
    acc = x_ref[...]
    for e in range(N_EXPERTS):
        pt = jnp.where(rankc[:, e:e + 1] == slot, aff[:, e:e + 1], 0.0)
        acc = acc + jnp.dot(_bf(pt), o_ref[e * cap:(e + 1) * cap, :], preferred_element_type=F32)
    if final_norm:
        ms = jnp.mean(acc * acc, axis=-1, keepdims=True)
        acc = acc * lax.rsqrt(ms + NORM_EPS) * g_ref[...]
    out_ref[...] = acc


def _combine(x3, rankc, aff3, o_flat, gain, cap, final_norm, tl):
    b, t, d = x3.shape
    tm = tl["tm_comb"]
    tile = lambda c: pl.BlockSpec((None, tm, c), lambda bi, i: (bi, i, 0))
    return pl.pallas_call(
        functools.partial(_combine_kernel, cap=cap, final_norm=final_norm),
        grid=(b, t // tm),
        in_specs=[tile(d), tile(LANE), tile(LANE),
                  pl.BlockSpec((None, N_EXPERTS * cap, d), lambda bi, i: (bi, 0, 0),
                               pipeline_mode=pl.Buffered(1)),
                  pl.BlockSpec((1, d), lambda bi, i: (0, 0))],
        out_specs=tile(d),
        out_shape=jax.ShapeDtypeStruct((b, t, d), F32),
        compiler_params=_params("parallel", "arbitrary"),
        name="moe_combine",
    )(x3, rankc, aff3, o_flat, gain)


def kernel(x, positions, norm_mix, w_in, w_out, ret_log_decay, ret_gn, rwkv_mu_rkv, rwkv_mu_w, rwkv_mu_a, rwkv_w0, rwkv_w_up, rwkv_a0, rwkv_a_up, rwkv_g_up, rwkv_k_k, rwkv_k_a, rwkv_r_k, rwkv_gn, lru_conv_w, lru_conv_b, lru_gate_w, lru_gate_b, lru_lambda, gla_alpha_up, gla_alpha_b, gla_gn, norm_ffn, router_w, router_b, exp_w_gate, exp_w_up, exp_w_down, norm_final):
    b, t, d = x.shape
    depth = w_in.shape[0]
    n = b * t
    tl = _tiles(t)
    cap = EC_CAPACITY_FACTOR * t // N_EXPERTS
    lane = jnp.arange(GROUP_W)
    same_head = (lane[:, None] // HEAD_DIM) == (lane[None, :] // HEAD_DIM)
    avg = jnp.where(same_head, 1.0 / HEAD_DIM, 0.0).astype(BF16)
    ones_bd = jnp.where(same_head, 1.0, 0.0).astype(BF16)
    cos, sin = _rope_tables(positions)

    x2 = x.reshape(n, d)
    for l in range(depth):
        w = w_in[l]
        split = 2048 + 512
        w_perm = jnp.concatenate(
            [w[:, :2048], w[:, split:], jnp.zeros((d, IN_COLS_PAD - w.shape[1]), F32), w[:, 2048:split]], axis=1)
        proj3 = _in_proj(x2, norm_mix[l][None, :], _bf(w_perm), tl).reshape(b, t, IN_COLS_PAD)

        o_ret = _retention(proj3, cos, sin, ret_log_decay[l], ret_gn[l], avg, tl)
        o_rwkv = _rwkv7(proj3, rwkv_mu_rkv[l], rwkv_mu_w[l], rwkv_mu_a[l], rwkv_w0[l], rwkv_w_up[l],
                        rwkv_a0[l], rwkv_a_up[l], rwkv_g_up[l], rwkv_k_k[l], rwkv_k_a[l], rwkv_r_k[l],
                        rwkv_gn[l], avg, ones_bd, tl)
        o_lru = _rglru(proj3, lru_conv_w[l], lru_conv_b[l], lru_gate_w[l], lru_gate_b[l], lru_lambda[l], tl)
        o_gla = _gla(proj3, gla_alpha_up[l], gla_alpha_b[l], gla_gn[l], avg, tl)
        outs = [o.reshape(n, GROUP_W) for o in (o_ret, o_rwkv, o_lru, o_gla)]

        x_mid, xn, aff, afft = _out_proj(x2, outs, _bf(w_out[l]), norm_ffn[l][None, :], router_w[l],
                                         router_b[l], b, t, tl)
        rank, rankc = _select(afft, cap)
        xs = _gather(rank, xn.reshape(b, t, d), cap, tl)
        o_exp = _ffn(xs, exp_w_gate, exp_w_up, exp_w_down, l, tl)
        x3 = _combine(x_mid.reshape(b, t, d), rankc, aff.reshape(b, t, LANE),
                      o_exp.reshape(b, N_EXPERTS * cap, d), norm_final[None, :], cap, l == depth - 1, tl)
        x2 = x3.reshape(n, d)
    return x2.reshape(b, t, d)
```

```python
import functools

import jax
import jax.numpy as jnp
from jax import lax
from jax.experimental import pallas as pl
from jax.experimental.pallas import tpu as pltpu

F32 = jnp.float32
BF16 = jnp.bfloat16

D_MODEL = 1024
GROUP_W = 256
N_HEADS = 4
HEAD_DIM = 64
ROPE_BASE = 10000.0
RWKV_GN_EPS = 64e-5
RGLRU_C = 8.0
GLA_KDIM = 128
GLA_HEAD_K = 32
GLA_TAU = 16.0
N_EXPERTS = 16
EC_CAPACITY_FACTOR = 2
NORM_EPS = 1e-6

LANE = 128
SUBLANE = 8
CHUNK = 64
VMEM_LIMIT = 56 * 1024 * 1024
VMEM_LIMIT_COMBINE = 62 * 1024 * 1024

COL_RET, COL_RWKV, COL_GLA, COL_LRU = 0, 1024, 2048, 3072
IN_COLS_PAD = 3584


def _tiles(T):
    return dict(
        tm_proj=min(512, T), tn_proj=512,
        tb_ret=min(512, T), tb_gla=min(512, T), tb_lru=min(512, T), tb_rwkv=min(1024, T),
        tm_out=min(512, T), tm_comb=min(1024, T), tk_gather=min(1024, T), gather_group=4, ffn_seqs=2, rwkv_group=(8, 6, 2),
    )


def _params(*sem):
    return pltpu.CompilerParams(dimension_semantics=sem, vmem_limit_bytes=VMEM_LIMIT)


def _bf(x):
    return x.astype(BF16)


def _dot(a, b):
    return jnp.dot(_bf(a), _bf(b), preferred_element_type=F32)


def _dot_nt(a, b):
    return lax.dot_general(_bf(a), _bf(b), (((1,), (1,)), ((), ())), preferred_element_type=F32)


def _dot_tn(a, b):
    return lax.dot_general(_bf(a), _bf(b), (((0,), (0,)), ((), ())), preferred_element_type=F32)


def _split2(a):
    hi = _bf(a)
    lo = _bf(a - hi.astype(F32))
    return hi, lo


def _split3(a):
    hi = _bf(a)
    r = a - hi.astype(F32)
    mid = _bf(r)
    lo = _bf(r - mid.astype(F32))
    return hi, mid, lo


def _dot_exact_lhs(m, a):
    hi, mid, lo = _split3(a)
    return (jnp.dot(m, hi, preferred_element_type=F32) + jnp.dot(m, mid, preferred_element_type=F32)
            + jnp.dot(m, lo, preferred_element_type=F32))


def _dot_exact_rhs(a, m):
    hi, lo = _split2(a)
    return jnp.dot(hi, m, preferred_element_type=F32) + jnp.dot(lo, m, preferred_element_type=F32)


def _dot_hp(a, b):
    ah, al = _split2(a)
    bh, bl = _split2(b)
    return (jnp.dot(ah, bh, preferred_element_type=F32) + jnp.dot(ah, bl, preferred_element_type=F32)
            + jnp.dot(al, bh, preferred_element_type=F32))


def _dot_hp_nt(a, b):
    ah, al = _split2(a)
    bh, bl = _split2(b)
    dn = (((1,), (1,)), ((), ()))
    return (lax.dot_general(ah, bh, dn, preferred_element_type=F32)
            + lax.dot_general(ah, bl, dn, preferred_element_type=F32)
            + lax.dot_general(al, bh, dn, preferred_element_type=F32))


def _sigmoid(x):
    return 1.0 / (1.0 + jnp.exp(-x))


def _softplus(x):
    return jnp.maximum(x, 0.0) + jnp.log(1.0 + jnp.exp(-jnp.abs(x)))


def _iota(shape, dim):
    return lax.broadcasted_iota(jnp.int32, shape, dim)


def _head_mask(h, width, hd):
    return (_iota((1, width), 1) // hd == h).astype(F32)


def _stack_heads(z, hd):
    w = z.shape[1]
    return jnp.concatenate([z * _head_mask(h, w, hd) for h in range(N_HEADS)], axis=0)


def _block_diag_bf(z):
    w = z.shape[1]
    lane_head = _iota((1, w), 1) // (w // N_HEADS)
    zero = jnp.zeros_like(z)
    return jnp.concatenate([jnp.where(lane_head == h, z, zero) for h in range(N_HEADS)], axis=0)


def _unstack_sum(z, c):
    return z[0:c] + z[c:2 * c] + z[2 * c:3 * c] + z[3 * c:4 * c]


def _tri_mask(n, lower, strict):
    i, j = _iota((n, n), 0), _iota((n, n), 1)
    if lower:
        return (j < i) if strict else (j <= i)
    return (j > i) if strict else (j >= i)


def _block_tri_mask(c, lower, strict):
    n = N_HEADS * c
    i, j = _iota((n, n), 0), _iota((n, n), 1)
    same = (i // c) == (j // c)
    ii, jj = i % c, j % c
    if lower:
        t = (jj < ii) if strict else (jj <= ii)
    else:
        t = (jj > ii) if strict else (jj >= ii)
    return same & t


def _head_stats(y, avg):
    return _dot_exact_rhs(y, avg)


def _shift_rows(z, edge, fwd):
    n = z.shape[0]
    row = _iota((n, 1), 0)
    if fwd:
        return jnp.where(row == 0, edge, pltpu.roll(z, 1, 0))
    return jnp.where(row == n - 1, edge, pltpu.roll(z, n - 1, 0))


def _in_proj_kernel(x_ref, g_ref, w_ref, o_ref, *, tn):
    x = x_ref[...]
    ms = jnp.mean(x * x, axis=-1, keepdims=True)
    xn = _bf(x * lax.rsqrt(ms + NORM_EPS) * g_ref[...])
    for c in range(o_ref.shape[1] // tn):
        o_ref[:, c * tn:(c + 1) * tn] = jnp.dot(xn, w_ref[:, c * tn:(c + 1) * tn], preferred_element_type=F32)


def _in_proj(x2, gain, w_bf, tl):
    n, d = x2.shape
    tm, tn = tl["tm_proj"], tl["tn_proj"]
    nc = w_bf.shape[1]
    return pl.pallas_call(
        functools.partial(_in_proj_kernel, tn=tn),
        grid=(n // tm,),
        in_specs=[pl.BlockSpec((tm, d), lambda i: (i, 0)),
                  pl.BlockSpec((1, d), lambda i: (0, 0)),
                  pl.BlockSpec((d, nc), lambda i: (0, 0), pipeline_mode=pl.Buffered(1))],
        out_specs=pl.BlockSpec((tm, nc), lambda i: (i, 0)),
        out_shape=jax.ShapeDtypeStruct((n, nc), F32),
        compiler_params=_params("parallel"),
        name="in_proj",
    )(x2, gain, w_bf)


def _rope_kernel(pos_ref, inv_ref, sgn_ref, cos_ref, sin_ref):
    ang = pos_ref[...].astype(F32) * inv_ref[...]
    cos_ref[...] = jnp.cos(ang)
    sin_ref[...] = jnp.sin(ang) * sgn_ref[...]


def _rope_tables(positions):
    b, t = positions.shape
    n = b * t
    tm = min(1024, n)
    lane = jnp.arange(LANE)
    inv = jnp.power(ROPE_BASE, -jnp.arange(0, HEAD_DIM, 2, dtype=F32) / HEAD_DIM)
    inv_l = inv[lane % (HEAD_DIM // 2)][None, :]
    sgn = jnp.where(lane % HEAD_DIM < HEAD_DIM // 2, -1.0, 1.0).astype(F32)[None, :]
    cos, sin = pl.pallas_call(
        _rope_kernel,
        grid=(n // tm,),
        in_specs=[pl.BlockSpec((tm, 1), lambda i: (i, 0)),
                  pl.BlockSpec((1, LANE), lambda i: (0, 0)),
                  pl.BlockSpec((1, LANE), lambda i: (0, 0))],
        out_specs=[pl.BlockSpec((tm, LANE), lambda i: (i, 0))] * 2,
        out_shape=[jax.ShapeDtypeStruct((n, LANE), F32)] * 2,
        compiler_params=_params("parallel"),
        name="rope_tables",
    )(positions.reshape(n, 1), inv_l, sgn)
    return cos.reshape(b, t, LANE), sin.reshape(b, t, LANE)


def _ret_kernel(x_ref, cos_ref, sin_ref, lgl_ref, gn_ref, avg_ref, *rest, TB, fwd):
    if fwd:
        yo_ref, o_ref, s_ref = rest
    else:
        o_ref, s_ref = rest
    W, C = GROUP_W, CHUNK

    @pl.when(pl.program_id(1) == 0)
    def _init():
        s_ref[...] = jnp.zeros_like(s_ref)

    x = x_ref[...]
    q, k, v, g = x[:, 0:W], x[:, W:2 * W], x[:, 2 * W:3 * W], x[:, 3 * W:4 * W]
    cos = jnp.concatenate([cos_ref[...]] * (W // LANE), axis=1)
    sin = jnp.concatenate([sin_ref[...]] * (W // LANE), axis=1)
    first_half = (_iota((1, W), 1) % HEAD_DIM) < (HEAD_DIM // 2)

    def rot(z):
        swapped = jnp.where(first_half, pltpu.roll(z, W - HEAD_DIM // 2, 1), pltpu.roll(z, HEAD_DIM // 2, 1))
        return z * cos + swapped * sin

    q = rot(q)
    k = rot(k) * (HEAD_DIM ** -0.5)
    lgl = lgl_ref[...]
    row_t = _iota((C, W), 0)
    lane_s = _iota((C, W), 1) % C
    rel = ((row_t - lane_s) if fwd else (lane_s - row_t)).astype(F32)
    dmask = jnp.where(rel >= 0, jnp.exp(lgl * jnp.maximum(rel, 0.0)), 0.0)
    idx = _iota((C, 1), 0).astype(F32)
    if fwd:
        zeta = jnp.exp(lgl * (C - 1.0 - idx))
        xi = jnp.exp(lgl * (idx + 1.0))
    else:
        zeta = jnp.exp(lgl * idx)
        xi = jnp.exp(lgl * (C - idx))
    cd = jnp.exp(lgl * float(C))
    bmask = ((_iota((W, W), 0) // HEAD_DIM) == (_iota((W, W), 1) // HEAD_DIM)).astype(F32)

    n_chunks = TB // C
    order = range(n_chunks) if fwd else range(n_chunks - 1, -1, -1)
    sls = [slice(c * C, (c + 1) * C) for c in order]
    kbs = [_bf(k[sl]) for sl in sls]
    vbs = [_bf(v[sl]) for sl in sls]
    scs = [_dot_nt(q[sl], _block_diag_bf(kb)) * dmask for sl, kb in zip(sls, kbs)]
    yis = [_dot(sc, _block_diag_bf(vb)) for sc, vb in zip(scs, vbs)]
    upds = [_dot_tn(k[sl] * zeta, vb) * bmask for sl, vb in zip(sls, vbs)]
    state = s_ref[...]
    ys = []
    for j, sl in enumerate(sls):
        ys.append(yis[j] + _dot(q[sl] * xi, state))
        state = state * cd + upds[j]
    s_ref[...] = state
    y = jnp.concatenate(ys if fwd else ys[::-1], axis=0)

    if fwd:
        yt = y + yo_ref[...]
        avg = avg_ref[...]
        yc = yt - _head_stats(yt, avg)
        var = _head_stats(yc * yc, avg)
        o_ref[...] = (yc * lax.rsqrt(var + NORM_EPS) * gn_ref[...] * (g * _sigmoid(g))).astype(o_ref.dtype)
    else:
        o_ref[...] = y


def _retention(proj3, cos, sin, log_decay, gn_w, avg, tl):
    b, t, _ = proj3.shape
    tb = tl["tb_ret"]
    nb = t // tb

    def call(fwd, lg, extra):
        def cm(i):
            return i if fwd else nb - 1 - i

        in_specs = [
            pl.BlockSpec((None, tb, 1024), lambda bi, i: (bi, cm(i), COL_RET // 1024)),
            pl.BlockSpec((None, tb, LANE), lambda bi, i: (bi, cm(i), 0)),
            pl.BlockSpec((None, tb, LANE), lambda bi, i: (bi, cm(i), 0)),
            pl.BlockSpec((1, GROUP_W), lambda bi, i: (0, 0)),
            pl.BlockSpec((1, GROUP_W), lambda bi, i: (0, 0)),
            pl.BlockSpec((GROUP_W, GROUP_W), lambda bi, i: (0, 0)),
        ]
        args = [proj3, cos, sin, jnp.repeat(lg, HEAD_DIM)[None, :], gn_w[None, :], avg]
        if fwd:
            in_specs.append(pl.BlockSpec((None, tb, GROUP_W), lambda bi, i: (bi, cm(i), 0)))
            args.append(extra)
        return pl.pallas_call(
            functools.partial(_ret_kernel, TB=tb, fwd=fwd),
            grid=(b, nb),
            in_specs=in_specs,
            out_specs=pl.BlockSpec((None, tb, GROUP_W), lambda bi, i: (bi, cm(i), 0)),
            out_shape=jax.ShapeDtypeStruct((b, t, GROUP_W), BF16 if fwd else F32),
            scratch_shapes=[pltpu.VMEM((GROUP_W, GROUP_W), F32)],
            compiler_params=_params("parallel", "arbitrary"),
            name="retention_fwd" if fwd else "retention_bwd",
        )(*args)

    y_b = call(False, log_decay[1], None)
    return call(True, log_decay[0], y_b)


def _gla_kernel(x_ref, aup_ref, ab_ref, gn_ref, avg_ref, *rest, TB, fwd):
    if fwd:
        yo_ref, o_ref, st_ref = rest
    else:
        o_ref, st_ref = rest
    W, KD, C = GROUP_W, GLA_KDIM, CHUNK

    @pl.when(pl.program_id(1) == 0)
    def _init():
        st_ref[...] = jnp.zeros_like(st_ref)

    x = x_ref[...]
    q = x[:, 0:KD] * (GLA_HEAD_K ** -0.5)
    k = x[:, KD:2 * KD]
    v = x[:, 2 * KD:2 * KD + W]
    og = x[:, 2 * KD + W:2 * KD + 2 * W]
    xa = x[:, 2 * KD + 2 * W:2 * KD + 2 * W + LANE]
    z = _dot_hp(xa, aup_ref[...]) + ab_ref[...]
    la = -_softplus(-z) * (1.0 / GLA_TAU)
    tri = jnp.where(_tri_mask(C, fwd, False), 1.0, 0.0).astype(BF16)
    row_t = _iota((C, W), 0)
    lane_s = _iota((C, W), 1) % C
    m_incl = (lane_s <= row_t) if fwd else (lane_s >= row_t)
    bmask_t = ((_iota((W, KD), 0) // HEAD_DIM) == (_iota((W, KD), 1) // GLA_HEAD_K)).astype(F32)

    n_chunks = TB // C
    order = range(n_chunks) if fwd else range(n_chunks - 1, -1, -1)
    sls = [slice(c * C, (c + 1) * C) for c in order]
    bcs = [_dot_exact_lhs(tri, la[sl]) for sl in sls]
    bls = [bc[C - 1:C] if fwd else bc[0:1] for bc in bcs]
    qds = [q[sl] * jnp.exp(bc) for sl, bc in zip(sls, bcs)]
    kds = [_bf(k[sl] * jnp.exp(-bc)) for sl, bc in zip(sls, bcs)]
    khs = [k[sl] * jnp.exp(bl - bc) for sl, bc, bl in zip(sls, bcs, bls)]
    vbs = [_bf(v[sl]) for sl in sls]
    scs = [jnp.where(m_incl, _dot_nt(qd, _block_diag_bf(kd)), 0.0) for qd, kd in zip(qds, kds)]
    yis = [_dot(sc, _block_diag_bf(vb)) for sc, vb in zip(scs, vbs)]
    upds = [_dot_tn(vb, kh) * bmask_t for vb, kh in zip(vbs, khs)]
    st = st_ref[...]
    ys = []
    for j, sl in enumerate(sls):
        ys.append(yis[j] + _dot_nt(qds[j], st))
        st = st * jnp.exp(bls[j]) + upds[j]
    st_ref[...] = st
    y = jnp.concatenate(ys if fwd else ys[::-1], axis=0)

    if fwd:
        yt = y + yo_ref[...]
        var = _head_stats(yt * yt, avg_ref[...])
        o_ref[...] = (yt * lax.rsqrt(var + NORM_EPS) * gn_ref[...] * (og * _sigmoid(og))).astype(o_ref.dtype)
    else:
        o_ref[...] = y


def _gla(proj3, alpha_up, alpha_b, gn_w, avg, tl):
    b, t, _ = proj3.shape
    tb = tl["tb_gla"]
    nb = t // tb

    def call(fwd, d, extra):
        def bm(i):
            return i if fwd else nb - 1 - i

        aup = jnp.zeros((LANE, GLA_KDIM), F32).at[:alpha_up.shape[1]].set(alpha_up[d])
        in_specs = [
            pl.BlockSpec((None, tb, 1024), lambda bi, i: (bi, bm(i), COL_GLA // 1024)),
            pl.BlockSpec((LANE, GLA_KDIM), lambda bi, i: (0, 0)),
            pl.BlockSpec((1, GLA_KDIM), lambda bi, i: (0, 0)),
            pl.BlockSpec((1, GROUP_W), lambda bi, i: (0, 0)),
            pl.BlockSpec((GROUP_W, GROUP_W), lambda bi, i: (0, 0)),
        ]
        args = [proj3, aup, alpha_b[d][None, :], gn_w[None, :], avg]
        if fwd:
            in_specs.append(pl.BlockSpec((None, tb, GROUP_W), lambda bi, i: (bi, bm(i), 0)))
            args.append(extra)
        return pl.pallas_call(
            functools.partial(_gla_kernel, TB=tb, fwd=fwd),
            grid=(b, nb),
            in_specs=in_specs,
            out_specs=pl.BlockSpec((None, tb, GROUP_W), lambda bi, i: (bi, bm(i), 0)),
            out_shape=jax.ShapeDtypeStruct((b, t, GROUP_W), BF16 if fwd else F32),
            scratch_shapes=[pltpu.VMEM((GROUP_W, GLA_KDIM), F32)],
            compiler_params=_params("parallel", "arbitrary"),
            name="gla_fwd" if fwd else "gla_bwd",
        )(*args)

    y_b = call(False, 1, None)
    return call(True, 0, y_b)


def _lru_kernel(x_ref, hp_ref, hn_ref, cw_ref, cb_ref, gw_ref, gb_ref, lam_ref, *rest, TB, NB, fwd):
    if fwd:
        ho_ref, o_ref, h_ref = rest
    else:
        o_ref, h_ref = rest
    W = GROUP_W
    i = pl.program_id(1)
    blk = i if fwd else NB - 1 - i

    @pl.when(i == 0)
    def _init():
        h_ref[...] = jnp.zeros_like(h_ref)

    x = x_ref[...]
    xr, gate = x[:, 0:W], x[:, W:2 * W]
    has_prev = (blk > 0).astype(F32)
    has_next = (blk < NB - 1).astype(F32)
    prev = hp_ref[...][:, 0:W] * has_prev
    nxt = hn_ref[...][:, 0:W] * has_next
    row = _iota((TB, 1), 0)
    x_m1 = jnp.where(row == 0, prev[7:8], pltpu.roll(xr, 1, 0))
    x_m2 = jnp.where(row == 0, prev[6:7], jnp.where(row == 1, prev[7:8], pltpu.roll(xr, 2, 0)))
    x_p1 = jnp.where(row == TB - 1, nxt[0:1], pltpu.roll(xr, TB - 1, 0))
    cw = cw_ref[...]
    xc = x_m2 * cw[0:1] + x_m1 * cw[1:2] + xr * cw[2:3] + x_p1 * cw[3:4] + cb_ref[...]

    gx = _dot(xc, gw_ref[...]) + gb_ref[...]
    rec = _sigmoid(gx[:, 0:W])
    ing = _sigmoid(gx[:, W:2 * W])
    log_a = -RGLRU_C * rec * _softplus(-lam_ref[...])
    a = jnp.exp(log_a)
    bx = jnp.sqrt(-jnp.tanh(log_a) * (a * a + 1.0)) * (ing * xc)

    s = 1
    while s < TB:
        if s % SUBLANE == 0:
            ones, zeros = jnp.ones((s, W), F32), jnp.zeros((s, W), F32)
            if fwd:
                a_sh = jnp.concatenate([ones, a[0:TB - s]], axis=0)
                b_sh = jnp.concatenate([zeros, bx[0:TB - s]], axis=0)
            else:
                a_sh = jnp.concatenate([a[s:TB], ones], axis=0)
                b_sh = jnp.concatenate([bx[s:TB], zeros], axis=0)
        elif fwd:
            keep = row >= s
            a_sh = jnp.where(keep, pltpu.roll(a, s, 0), 1.0)
            b_sh = jnp.where(keep, pltpu.roll(bx, s, 0), 0.0)
        else:
            keep = row < TB - s
            a_sh = jnp.where(keep, pltpu.roll(a, TB - s, 0), 1.0)
            b_sh = jnp.where(keep, pltpu.roll(bx, TB - s, 0), 0.0)
        bx = a * b_sh + bx
        a = a * a_sh
        s *= 2
    h = a * h_ref[...] + bx
    h_ref[...] = h[TB - 1:TB] if fwd else h[0:1]

    if fwd:
        o_ref[...] = ((h + ho_ref[...]) * jax.nn.gelu(gate, approximate=True)).astype(o_ref.dtype)
    else:
        o_ref[...] = h


def _block_diag_heads(w):
    h, d, e = w.shape
    eye = jnp.eye(h, dtype=w.dtype)
    return jnp.einsum('hde,hg->hdge', w, eye).reshape(h * d, h * e)


def _rglru(proj3, conv_w, conv_b, gate_w, gate_b, lam, tl):
    b, t, _ = proj3.shape
    tb = tl["tb_lru"]
    nb = t // tb
    r8 = tb // SUBLANE
    last8 = t // SUBLANE - 1
    colb = COL_LRU // 512

    def call(fwd, d, extra):
        def bm(i):
            return i if fwd else nb - 1 - i

        gw = jnp.concatenate([_block_diag_heads(gate_w[d, 0]), _block_diag_heads(gate_w[d, 1])], axis=1)
        gb = jnp.concatenate([gate_b[d, 0], gate_b[d, 1]])[None, :]
        in_specs = [
            pl.BlockSpec((None, tb, 512), lambda bi, i: (bi, bm(i), colb)),
            pl.BlockSpec((None, SUBLANE, 512), lambda bi, i: (bi, jnp.maximum(bm(i) * r8 - 1, 0), colb)),
            pl.BlockSpec((None, SUBLANE, 512), lambda bi, i: (bi, jnp.minimum((bm(i) + 1) * r8, last8), colb)),
            pl.BlockSpec((4, GROUP_W), lambda bi, i: (0, 0)),
            pl.BlockSpec((1, GROUP_W), lambda bi, i: (0, 0)),
            pl.BlockSpec((GROUP_W, 2 * GROUP_W), lambda bi, i: (0, 0)),
            pl.BlockSpec((1, 2 * GROUP_W), lambda bi, i: (0, 0)),
            pl.BlockSpec((1, GROUP_W), lambda bi, i: (0, 0)),
        ]
        args = [proj3, proj3, proj3, conv_w, conv_b[None, :], _bf(gw), gb, lam[d][None, :]]
        if fwd:
            in_specs.append(pl.BlockSpec((None, tb, GROUP_W), lambda bi, i: (bi, bm(i), 0)))
            args.append(extra)
        return pl.pallas_call(
            functools.partial(_lru_kernel, TB=tb, NB=nb, fwd=fwd),
            grid=(b, nb),
            in_specs=in_specs,
            out_specs=pl.BlockSpec((None, tb, GROUP_W), lambda bi, i: (bi, bm(i), 0)),
            out_shape=jax.ShapeDtypeStruct((b, t, GROUP_W), BF16 if fwd else F32),
            scratch_shapes=[pltpu.VMEM((1, GROUP_W), F32)],
            compiler_params=_params("parallel", "arbitrary"),
            name="rglru_fwd" if fwd else "rglru_bwd",
        )(*args)

    h_b = call(False, 1, None)
    return call(True, 0, h_b)


def _rwkv_kernel(x_ref, halo_ref, mu_ref, lora_ref, w0_ref, a0_ref, kk_ref, ka_ref, rk_ref, ones_ref,
                 *rest, TB, NB, fwd, GROUP):
    if fwd:
        gup_ref, gn_ref, avg_ref, yo_ref, bo_ref, o_ref, s_ref = rest
    else:
        y_ref, b_ref, s_ref = rest
    W, C = GROUP_W, CHUNK
    i = pl.program_id(1)
    blk = i if fwd else NB - 1 - i

    @pl.when(i == 0)
    def _init():
        s_ref[...] = jnp.zeros_like(s_ref)

    x = x_ref[...]
    NS = 3 * W + LANE
    zs = x[:, 0:NS]
    halo = halo_ref[...][:, 0:NS]
    if fwd:
        edge = halo[SUBLANE - 1:SUBLANE] * (blk > 0).astype(F32)
    else:
        edge = halo[0:1] * (blk < NB - 1).astype(F32)
    zs = zs + mu_ref[...] * (_shift_rows(zs, edge, fwd) - zs)
    r, k, v, xwa = zs[:, 0:W], zs[:, W:2 * W], zs[:, 2 * W:3 * W], zs[:, 3 * W:NS]
    lane_l = _iota((1, LANE), 1)
    xwa = jnp.where(lane_l < LANE // 2, jnp.tanh(xwa), xwa)
    lo = _dot_hp(xwa, lora_ref[...])
    w_log = -_softplus(-(w0_ref[...] + lo[:, 0:W])) - 0.5
    lw = -jnp.exp(w_log)
    a = _sigmoid(a0_ref[...] + lo[:, W:2 * W])
    ones_bd = ones_ref[...]
    kk = k * kk_ref[...]
    kk = kk / jnp.maximum(jnp.sqrt(_dot_exact_rhs(kk * kk, ones_bd)), 1e-12)
    k = k * (1.0 + (a - 1.0) * ka_ref[...])
    bonus = _dot_exact_rhs(r * k * rk_ref[...], ones_bd) * v
    am = -kk
    bm = kk * a

    tri = jnp.where(_tri_mask(C, fwd, False), 1.0, 0.0).astype(BF16)
    row_t = _iota((C, W), 0)
    lane_s = _iota((C, W), 1) % C
    m_strict = (lane_s < row_t) if fwd else (lane_s > row_t)
    m_incl = (lane_s <= row_t) if fwd else (lane_s >= row_t)
    eye_c = (lane_s == row_t).astype(F32)
    bmask = (_iota((W, W), 0) // HEAD_DIM) == (_iota((W, W), 1) // HEAD_DIM)

    def bd(z):
        return _block_diag_bf(_bf(z))

    n_chunks = TB // C
    order = range(n_chunks) if fwd else range(n_chunks - 1, -1, -1)
    def state_free_part(sls):
        cums = [_dot_exact_lhs(tri, lw[sl]) for sl in sls]
        tots = [cum[C - 1:C] if fwd else cum[0:1] for cum in cums]
        e_negs = [jnp.exp(-cum) for cum in cums]
        e_ends = [jnp.exp(tot - cum) for tot, cum in zip(tots, cums)]
        a_ts = [am[sl] * jnp.exp(cum - lw[sl]) for sl, cum in zip(sls, cums)]
        r_ts = [r[sl] * jnp.exp(cum) for sl, cum in zip(sls, cums)]
        yield None
        gs = [_dot_nt(jnp.concatenate([a_t, r_t], axis=0),
                      jnp.concatenate([bd(bm[sl] * e_neg), bd(k[sl] * e_neg)], axis=0))
              for sl, a_t, r_t, e_neg in zip(sls, a_ts, r_ts, e_negs)]
        yield None
        a_abs = [jnp.where(m_strict, g[0:C, 0:W], 0.0) for g in gs]
        a_aks = [jnp.where(m_strict, g[0:C, W:2 * W], 0.0) for g in gs]
        a_rbs = [jnp.where(m_incl, g[C:2 * C, 0:W], 0.0) for g in gs]
        a_rks = [jnp.where(m_incl, g[C:2 * C, W:2 * W], 0.0) for g in gs]

        tinvs = [eye_c + a_ab for a_ab in a_abs]
        ps = [_dot(a_ab, bd(a_ab)) for a_ab in a_abs]
        avs = [_dot(jnp.concatenate([a_ak, a_rk], axis=0), bd(v[sl]))
               for a_ak, a_rk, sl in zip(a_aks, a_rks, sls)]
        yield None
        n_terms = 2
        while 2 * n_terms < C:
            xps = [_dot(jnp.concatenate([tinv, p], axis=0), bd(p)) for p, tinv in zip(ps, tinvs)]
            tinvs = [tinv + xp[0:C] for tinv, xp in zip(tinvs, xps)]
            ps = [xp[C:2 * C] for xp in xps]
            n_terms *= 2
            yield None
        tinvs = [tinv + _dot(tinv, bd(p)) for p, tinv in zip(ps, tinvs)]
        yield None
        w12s = [_dot(tinv, jnp.concatenate([bd(a_t), bd(av[0:C])], axis=1))
                for tinv, a_t, av in zip(tinvs, a_ts, avs)]
        yield None
        qys = [_dot(a_rb, jnp.concatenate([bd(w12[:, 0:W]), bd(w12[:, W:2 * W])], axis=1))
               for a_rb, w12 in zip(a_rbs, w12s)]
        yield [dict(sl=sl, w1=w12[:, 0:W], w2=w12[:, W:2 * W], qp=r_t + qy[:, 0:W],
                    y0=av[C:2 * C] + qy[:, W:2 * W], decay=jnp.exp(tot),
                    kb=jnp.concatenate([k[sl] * e_end, bm[sl] * e_end], axis=0))
               for sl, w12, r_t, qy, av, tot, e_end in zip(sls, w12s, r_ts, qys, avs, tots, e_ends)]

    def recur(state, p_):
        sl = p_["sl"]
        uy = _dot_nt(jnp.concatenate([p_["w1"], p_["qp"]], axis=0), state)
        u = uy[0:C] + p_["w2"]
        y = uy[C:2 * C] + p_["y0"]
        upd = _dot_tn(jnp.concatenate([v[sl], u], axis=0), p_["kb"])
        ys.append(y)
        return state * p_["decay"] + jnp.where(bmask, upd, 0.0)

    ys = []

    sls_all = [slice(c * C, (c + 1) * C) for c in order]
    groups = GROUP if sum(GROUP) == n_chunks else (n_chunks,)
    state = s_ref[...]
    pending = []
    g0 = 0
    for group in groups:
        done = None
        for out in state_free_part(sls_all[g0:g0 + group]):
            if out is not None:
                done = out
            elif pending:
                state = recur(state, pending.pop(0))
        while pending:
            state = recur(state, pending.pop(0))
        pending = done
        g0 += group
    while pending:
        state = recur(state, pending.pop(0))
    s_ref[...] = state
    y = jnp.concatenate(ys if fwd else ys[::-1], axis=0)

    if fwd:
        yt = y + yo_ref[...]
        avg = avg_ref[...]
        yc = yt - _head_stats(yt, avg)
        var = _head_stats(yc * yc, avg)
        o = yc * lax.rsqrt(var + RWKV_GN_EPS) * gn_ref[...] + bonus + bo_ref[...]
        xg = x[:, NS:NS + LANE]
        o_ref[...] = (o * _dot_hp(_sigmoid(xg), gup_ref[...])).astype(o_ref.dtype)
    else:
        y_ref[...] = y
        b_ref[...] = bonus


def _rwkv7(proj3, mu_rkv, mu_w, mu_a, w0, w_up, a0, a_up, g_up, k_k, k_a, r_k, gn_w, avg, ones_bd, tl):
    b, t, _ = proj3.shape
    tb = tl["tb_rwkv"]
    nb = t // tb
    r8 = tb // SUBLANE
    last8 = t // SUBLANE - 1
    colb = COL_RWKV // 1024
    W = GROUP_W
    mu = jnp.concatenate([mu_rkv.reshape(-1), mu_w, mu_a])[None, :]
    nl = w_up.shape[1]

    def call(fwd, d, extra):
        def bm(i):
            return i if fwd else nb - 1 - i

        lora = jnp.zeros((LANE, 2 * W), F32).at[0:nl, 0:W].set(w_up[d]).at[nl:2 * nl, W:2 * W].set(a_up[d])
        if fwd:
            halo_map = lambda bi, i: (bi, jnp.maximum(bm(i) * r8 - 1, 0), colb)
        else:
            halo_map = lambda bi, i: (bi, jnp.minimum((bm(i) + 1) * r8, last8), colb)
        row = lambda n: pl.BlockSpec((1, n), lambda bi, i: (0, 0))
        in_specs = [
            pl.BlockSpec((None, tb, 1024), lambda bi, i: (bi, bm(i), colb)),
            pl.BlockSpec((None, SUBLANE, 1024), halo_map),
            row(3 * W + LANE),
            pl.BlockSpec((LANE, 2 * W), lambda bi, i: (0, 0)),
            row(W), row(W), row(W), row(W), row(W),
            pl.BlockSpec((W, W), lambda bi, i: (0, 0)),
        ]
        args = [proj3, proj3, mu, lora, w0[d][None, :], a0[d][None, :], k_k[None, :], k_a[None, :],
                r_k.reshape(1, W), ones_bd]
        blk = pl.BlockSpec((None, tb, W), lambda bi, i: (bi, bm(i), 0))
        if fwd:
            in_specs += [pl.BlockSpec((LANE, W), lambda bi, i: (0, 0)), row(W),
                         pl.BlockSpec((W, W), lambda bi, i: (0, 0)), blk, blk]
            args += [g_up, gn_w[None, :], avg, extra[0], extra[1]]
            out_specs = blk
            out_shape = jax.ShapeDtypeStruct((b, t, W), BF16)
        else:
            out_specs = [blk, blk]
            out_shape = [jax.ShapeDtypeStruct((b, t, W), F32)] * 2
        return pl.pallas_call(
            functools.partial(_rwkv_kernel, TB=tb, NB=nb, fwd=fwd, GROUP=tl["rwkv_group"]),
            grid=(b, nb),
            in_specs=in_specs,
            out_specs=out_specs,
            out_shape=out_shape,
            scratch_shapes=[pltpu.VMEM((W, W), F32)],
            compiler_params=_params("parallel", "arbitrary"),
            name="rwkv7_fwd" if fwd else "rwkv7_bwd",
        )(*args)

    y_b, bonus_b = call(False, 1, None)
    return call(True, 0, (y_b, bonus_b))


def _out_proj_kernel(x_ref, o1_ref, o2_ref, o3_ref, o4_ref, w_ref, g_ref, rw_ref, rb_ref, rwt_ref, rbt_ref,
                     xo_ref, xn_ref, aff_ref, afft_ref):
    W = GROUP_W
    w = w_ref[...]
    acc = x_ref[...]
    for gi, o_ref in enumerate((o1_ref, o2_ref, o3_ref, o4_ref)):
        acc = acc + jnp.dot(_bf(o_ref[...]), w[gi * W:(gi + 1) * W], preferred_element_type=F32)
    xo_ref[...] = acc
    ms = jnp.mean(acc * acc, axis=-1, keepdims=True)
    xn = acc * lax.rsqrt(ms + NORM_EPS) * g_ref[...]
    xn_ref[...] = _bf(xn)
    logits = _dot_hp(xn, rw_ref[...]) + rb_ref[...]
    logits = logits - jnp.max(logits, axis=-1, keepdims=True)
    e = jnp.exp(logits)
    aff_ref[...] = e / jnp.sum(e, axis=-1, keepdims=True)
    lt = _dot_hp_nt(rwt_ref[...], xn) + rbt_ref[...]
    lt = lt - jnp.max(lt, axis=0, keepdims=True)
    et = jnp.exp(lt)
    afft_ref[...] = et / jnp.sum(et, axis=0, keepdims=True)


def _out_proj(x2, outs, w_out_bf, gain, router_w, router_b, b, t, tl):
    n, d = x2.shape
    tm = tl["tm_out"]
    tpb = t // tm
    E = N_EXPERTS
    rw = jnp.zeros((d, LANE), F32).at[:, :E].set(router_w)
    rb = jnp.full((1, LANE), -1e30, F32).at[0, :E].set(router_b)
    full = lambda r, c: pl.BlockSpec((r, c), lambda i: (0, 0))
    tile = lambda c: pl.BlockSpec((tm, c), lambda i: (i, 0))
    return pl.pallas_call(
        _out_proj_kernel,
        grid=(n // tm,),
        in_specs=[tile(d)] + [tile(GROUP_W)] * 4 + [full(d, d), full(1, d), full(d, LANE), full(1, LANE),
                                                     full(E, d), full(E, 1)],
        out_specs=[tile(d), tile(d), tile(LANE),
                   pl.BlockSpec((None, E, tm), lambda i: (i // tpb, 0, i % tpb))],
        out_shape=[jax.ShapeDtypeStruct((n, d), F32), jax.ShapeDtypeStruct((n, d), BF16),
                   jax.ShapeDtypeStruct((n, LANE), F32), jax.ShapeDtypeStruct((b, E, t), F32)],
        compiler_params=_params("parallel"),
        name="out_proj_router",
    )(x2, *outs, w_out_bf, gain, rw, rb, router_w.T, router_b[:, None])


def _prefix_count(m, triu):
    e, t = m.shape
    nblk = t // LANE
    stacked = jnp.concatenate([m[:, j * LANE:(j + 1) * LANE] for j in range(nblk)], axis=0)
    incl = jnp.dot(_bf(stacked), triu, preferred_element_type=F32)
    pieces = []
    off = jnp.zeros((e, 1), F32)
    for j in range(nblk):
        blk = incl[j * e:(j + 1) * e]
        pieces.append(blk - stacked[j * e:(j + 1) * e] + off)
        off = off + blk[:, LANE - 1:LANE]
    return jnp.concatenate(pieces, axis=1)


def _select_kernel(afft_ref, rank_ref, rankc_ref, *, cap):
    aff = afft_ref[...]
    e, t = aff.shape
    bits = lax.bitcast_convert_type(aff, jnp.int32)

    def body(i, prefix):
        cand = prefix | jnp.left_shift(jnp.int32(1), 30 - i)
        cnt = jnp.sum((bits >= cand).astype(F32), axis=1, keepdims=True)
        return jnp.where(cnt >= cap, cand, prefix)

    thr = lax.fori_loop(0, 31, body, jnp.zeros((e, 1), jnp.int32))
    gt = (bits > thr).astype(F32)
    eq = (bits == thr).astype(F32)
    need = cap - jnp.sum(gt, axis=1, keepdims=True)
    triu = jnp.where(_tri_mask(LANE, False, False), 1.0, 0.0).astype(BF16)
    sel = gt + eq * (_prefix_count(eq, triu) < need).astype(F32)
    rank = jnp.where(sel > 0.5, _prefix_count(sel, triu), -1.0)
    rank_ref[...] = rank
    padded = jnp.concatenate([rank, jnp.full((LANE - e, t), -1.0, F32)], axis=0)
    rankc_ref[...] = padded.T


def _select(afft, cap):
    b, e, t = afft.shape
    return pl.pallas_call(
        functools.partial(_select_kernel, cap=cap),
        grid=(b,),
        in_specs=[pl.BlockSpec((None, e, t), lambda i: (i, 0, 0))],
        out_specs=[pl.BlockSpec((None, e, t), lambda i: (i, 0, 0)),
                   pl.BlockSpec((None, t, LANE), lambda i: (i, 0, 0))],
        out_shape=[jax.ShapeDtypeStruct((b, e, t), F32), jax.ShapeDtypeStruct((b, t, LANE), F32)],
        compiler_params=_params("parallel"),
        name="expert_choice_select",
    )(afft)


def _gather_kernel(rank_ref, xn_ref, xs_ref, acc_ref, *, cap, tk, group):
    g0 = pl.program_id(1) * group
    t = xn_ref.shape[0]
    slot = _iota((cap, tk), 0).astype(F32)
    for j in range(t // tk):
        onehot = jnp.concatenate(
            [jnp.where(rank_ref[pl.ds(g0 + g, 1), j * tk:(j + 1) * tk] == slot, 1.0, 0.0).astype(BF16)
             for g in range(group)], axis=0)
        part = jnp.dot(onehot, xn_ref[j * tk:(j + 1) * tk, :], preferred_element_type=F32)
        if j == 0:
            acc_ref[...] = part
        else:
            acc_ref[...] += part
    for g in range(group):
        xs_ref[g] = _bf(acc_ref[g * cap:(g + 1) * cap, :])


def _gather(rank, xn3, cap, tl):
    b, e, t = rank.shape
    d = xn3.shape[2]
    group = tl["gather_group"]
    return pl.pallas_call(
        functools.partial(_gather_kernel, cap=cap, tk=tl["tk_gather"], group=group),
        grid=(b, e // group),
        in_specs=[pl.BlockSpec((None, e, t), lambda bi, gi: (bi, 0, 0)),
                  pl.BlockSpec((None, t, d), lambda bi, gi: (bi, 0, 0))],
        out_specs=pl.BlockSpec((None, group, cap, d), lambda bi, gi: (bi, gi, 0, 0)),
        out_shape=jax.ShapeDtypeStruct((b, e, cap, d), BF16),
        scratch_shapes=[pltpu.VMEM((group * cap, d), F32)],
        compiler_params=_params("parallel", "arbitrary"),
        name="moe_gather",
    )(rank, xn3)


def _ffn_kernel(xs_ref, wg_ref, wu_ref, wd_ref, o_ref, wgb_ref, wub_ref, wdb_ref):
    @pl.when(pl.program_id(1) == 0)
    def _():
        wgb_ref[...] = _bf(wg_ref[...])
        wub_ref[...] = _bf(wu_ref[...])
        wdb_ref[...] = _bf(wd_ref[...])

    nseq = xs_ref.shape[0]
    xs = [xs_ref[i] for i in range(nseq)]
    hg = [jnp.dot(x, wgb_ref[...], preferred_element_type=F32) for x in xs]
    hu = [jnp.dot(x, wub_ref[...], preferred_element_type=F32) for x in xs]
    hid = [_bf(g * _sigmoid(g) * u) for g, u in zip(hg, hu)]
    for i in range(nseq):
        o_ref[i] = _bf(jnp.dot(hid[i], wdb_ref[...], preferred_element_type=F32))


def _ffn(xs, w_gate, w_up, w_down, layer, tl):
    b, e, cap, d = xs.shape
    f = w_gate.shape[3]
    nseq = min(tl["ffn_seqs"], b)
    wspec = lambda r, c: pl.BlockSpec((None, None, r, c), lambda ei, bi: (layer, ei, 0, 0))
    xspec = pl.BlockSpec((nseq, None, cap, d), lambda ei, bi: (bi, ei, 0, 0))
    return pl.pallas_call(
        _ffn_kernel,
        grid=(e, b // nseq),
        in_specs=[xspec, wspec(d, f), wspec(d, f), wspec(f, d)],
        out_specs=xspec,
        out_shape=jax.ShapeDtypeStruct((b, e, cap, d), BF16),
        scratch_shapes=[pltpu.VMEM((d, f), BF16), pltpu.VMEM((d, f), BF16), pltpu.VMEM((f, d), BF16)],
        compiler_params=_params("arbitrary", "arbitrary"),
        name="moe_ffn",
    )(xs, w_gate, w_up, w_down)


def _combine_kernel(x_ref, rankc_ref, aff_ref, o_ref, g_ref, out_ref, *, cap, final_norm):
    rankc = rankc_ref[...]
    aff = aff_ref[...]
    tm = x_ref.shape[0]
    slot = _iota((tm, cap), 1).astype(F32)
    acc = x_ref[...]
    for e in range(N_EXPERTS):
        pt = jnp.where(rankc[:, e:e + 1] == slot, aff[:, e:e + 1], 0.0)
        acc = acc + jnp.dot(_bf(pt), o_ref[e * cap:(e + 1) * cap, :], preferred_element_type=F32)
    if final_norm:
        ms = jnp.mean(acc * acc, axis=-1, keepdims=True)
        acc = acc * lax.rsqrt(ms + NORM_EPS) * g_ref[...]
    out_ref[...] = acc


def _combine(x3, rankc, aff3, o_flat, gain, cap, final_norm, tl):
    b, t, d = x3.shape
    tm = tl["tm_comb"]
    tile = lambda c: pl.BlockSpec((None, tm, c), lambda bi, i: (bi, i, 0))
    return pl.pallas_call(
        functools.partial(_combine_kernel, cap=cap, final_norm=final_norm),
        grid=(b, t // tm),
        in_specs=[tile(d), tile(LANE), tile(LANE),
                  pl.BlockSpec((None, N_EXPERTS * cap, d), lambda bi, i: (bi, 0, 0)),
                  pl.BlockSpec((1, d), lambda bi, i: (0, 0))],
        out_specs=tile(d),
        out_shape=jax.ShapeDtypeStruct((b, t, d), F32),
        compiler_params=pltpu.CompilerParams(dimension_semantics=("parallel", "arbitrary"),
                                             vmem_limit_bytes=VMEM_LIMIT_COMBINE),
        name="moe_combine",
    )(x3, rankc, aff3, o_flat, gain)


def kernel(x, positions, norm_mix, w_in, w_out, ret_log_decay, ret_gn, rwkv_mu_rkv, rwkv_mu_w, rwkv_mu_a, rwkv_w0, rwkv_w_up, rwkv_a0, rwkv_a_up, rwkv_g_up, rwkv_k_k, rwkv_k_a, rwkv_r_k, rwkv_gn, lru_conv_w, lru_conv_b, lru_gate_w, lru_gate_b, lru_lambda, gla_alpha_up, gla_alpha_b, gla_gn, norm_ffn, router_w, router_b, exp_w_gate, exp_w_up, exp_w_down, norm_final):
    b, t, d = x.shape
    depth = w_in.shape[0]
    n = b * t
    tl = _tiles(t)
    cap = EC_CAPACITY_FACTOR * t // N_EXPERTS
    lane = jnp.arange(GROUP_W)
    same_head = (lane[:, None] // HEAD_DIM) == (lane[None, :] // HEAD_DIM)
    avg = jnp.where(same_head, 1.0 / HEAD_DIM, 0.0).astype(BF16)
    ones_bd = jnp.where(same_head, 1.0, 0.0).astype(BF16)
    cos, sin = _rope_tables(positions)

    x2 = x.reshape(n, d)
    for l in range(depth):
        w = w_in[l]
        split = 2048 + 512
        w_perm = jnp.concatenate(
            [w[:, :2048], w[:, split:], jnp.zeros((d, IN_COLS_PAD - w.shape[1]), F32), w[:, 2048:split]], axis=1)
        proj3 = _in_proj(x2, norm_mix[l][None, :], _bf(w_perm), tl).reshape(b, t, IN_COLS_PAD)

        o_ret = _retention(proj3, cos, sin, ret_log_decay[l], ret_gn[l], avg, tl)
        o_rwkv = _rwkv7(proj3, rwkv_mu_rkv[l], rwkv_mu_w[l], rwkv_mu_a[l], rwkv_w0[l], rwkv_w_up[l],
                        rwkv_a0[l], rwkv_a_up[l], rwkv_g_up[l], rwkv_k_k[l], rwkv_k_a[l], rwkv_r_k[l],
                        rwkv_gn[l], avg, ones_bd, tl)
        o_lru = _rglru(proj3, lru_conv_w[l], lru_conv_b[l], lru_gate_w[l], lru_gate_b[l], lru_lambda[l], tl)
        o_gla = _gla(proj3, gla_alpha_up[l], gla_alpha_b[l], gla_gn[l], avg, tl)
        outs = [o.reshape(n, GROUP_W) for o in (o_ret, o_rwkv, o_lru, o_gla)]

        x_mid, xn, aff, afft = _out_proj(x2, outs, _bf(w_out[l]), norm_ffn[l][None, :], router_w[l],
                                         router_b[l], b, t, tl)
        rank, rankc = _select(afft, cap)
        xs = _gather(rank, xn.reshape(b, t, d), cap, tl)
        o_exp = _ffn(xs, exp_w_gate, exp_w_up, exp_w_down, l, tl)
        x3 = _combine(x_mid.reshape(b, t, d), rankc, aff.reshape(b, t, LANE),
                      o_exp.reshape(b, N_EXPERTS * cap, d), norm_final[None, :], cap, l == depth - 1, tl)
        x2 = x3.reshape(n, d)
    return x2.reshape(b, t, d)
```

```python
import functools

import jax
import jax.numpy as jnp
from jax import lax
from jax.experimental import pallas as pl
from jax.experimental.pallas import tpu as pltpu

F32 = jnp.float32
BF16 = jnp.bfloat16

D_MODEL = 1024
GROUP_W = 256
N_HEADS = 4
HEAD_DIM = 64
ROPE_BASE = 10000.0
RWKV_GN_EPS = 64e-5
RGLRU_C = 8.0
GLA_KDIM = 128
GLA_HEAD_K = 32
GLA_TAU = 16.0
N_EXPERTS = 16
EC_CAPACITY_FACTOR = 2
NORM_EPS = 1e-6

LANE = 128
SUBLANE = 8
CHUNK = 64
VMEM_LIMIT = 56 * 1024 * 1024
VMEM_LIMIT_COMBINE = 62 * 1024 * 1024

COL_RET, COL_RWKV, COL_GLA, COL_LRU = 0, 1024, 2048, 3072
IN_COLS_PAD = 3584


def _tiles(T):
    return dict(
        tm_proj=min(512, T), tn_proj=512,
        tb_ret=min(1024, T), tb_gla=min(1024, T), tb_lru=min(512, T), tb_rwkv=min(1024, T),
        tm_out=min(512, T), tm_comb=min(1024, T), tk_gather=min(1024, T), gather_group=4, ffn_seqs=2, rwkv_group=(8, 6, 2),
    )


def _params(*sem):
    return pltpu.CompilerParams(dimension_semantics=sem, vmem_limit_bytes=VMEM_LIMIT)


def _bf(x):
    return x.astype(BF16)


def _dot(a, b):
    return jnp.dot(_bf(a), _bf(b), preferred_element_type=F32)


def _dot_nt(a, b):
    return lax.dot_general(_bf(a), _bf(b), (((1,), (1,)), ((), ())), preferred_element_type=F32)


def _dot_tn(a, b):
    return lax.dot_general(_bf(a), _bf(b), (((0,), (0,)), ((), ())), preferred_element_type=F32)


def _split2(a):
    hi = _bf(a)
    lo = _bf(a - hi.astype(F32))
    return hi, lo


def _split3(a):
    hi = _bf(a)
    r = a - hi.astype(F32)
    mid = _bf(r)
    lo = _bf(r - mid.astype(F32))
    return hi, mid, lo


def _dot_exact_lhs(m, a):
    hi, mid, lo = _split3(a)
    return (jnp.dot(m, hi, preferred_element_type=F32) + jnp.dot(m, mid, preferred_element_type=F32)
            + jnp.dot(m, lo, preferred_element_type=F32))


def _dot_exact_rhs(a, m):
    hi, lo = _split2(a)
    return jnp.dot(hi, m, preferred_element_type=F32) + jnp.dot(lo, m, preferred_element_type=F32)


def _dot_hp(a, b):
    ah, al = _split2(a)
    bh, bl = _split2(b)
    return (jnp.dot(ah, bh, preferred_element_type=F32) + jnp.dot(ah, bl, preferred_element_type=F32)
            + jnp.dot(al, bh, preferred_element_type=F32))


def _sigmoid(x):
    return 1.0 / (1.0 + jnp.exp(-x))


def _softplus(x):
    return jnp.maximum(x, 0.0) + jnp.log(1.0 + jnp.exp(-jnp.abs(x)))


def _iota(shape, dim):
    return lax.broadcasted_iota(jnp.int32, shape, dim)


def _block_diag_bf(z):
    w = z.shape[1]
    lane_head = _iota((1, w), 1) // (w // N_HEADS)
    zero = jnp.zeros_like(z)
    return jnp.concatenate([jnp.where(lane_head == h, z, zero) for h in range(N_HEADS)], axis=0)


def _tri_mask(n, lower, strict):
    i, j = _iota((n, n), 0), _iota((n, n), 1)
    if lower:
        return (j < i) if strict else (j <= i)
    return (j > i) if strict else (j >= i)


def _head_stats(y, avg):
    return _dot_exact_rhs(y, avg)


def _shift_rows(z, edge, fwd):
    n = z.shape[0]
    row = _iota((n, 1), 0)
    if fwd:
        return jnp.where(row == 0, edge, pltpu.roll(z, 1, 0))
    return jnp.where(row == n - 1, edge, pltpu.roll(z, n - 1, 0))


def _in_proj_kernel(x_ref, g_ref, w_ref, o_ref, *, tn):
    x = x_ref[...]
    ms = jnp.mean(x * x, axis=-1, keepdims=True)
    xn = _bf(x * lax.rsqrt(ms + NORM_EPS) * g_ref[...])
    for c in range(o_ref.shape[1] // tn):
        o_ref[:, c * tn:(c + 1) * tn] = jnp.dot(xn, w_ref[:, c * tn:(c + 1) * tn], preferred_element_type=F32)


def _in_proj(x2, gain, w_bf, tl):
    n, d = x2.shape
    tm, tn = tl["tm_proj"], tl["tn_proj"]
    nc = w_bf.shape[1]
    return pl.pallas_call(
        functools.partial(_in_proj_kernel, tn=tn),
        grid=(n // tm,),
        in_specs=[pl.BlockSpec((tm, d), lambda i: (i, 0)),
                  pl.BlockSpec((1, d), lambda i: (0, 0)),
                  pl.BlockSpec((d, nc), lambda i: (0, 0), pipeline_mode=pl.Buffered(1))],
        out_specs=pl.BlockSpec((tm, nc), lambda i: (i, 0)),
        out_shape=jax.ShapeDtypeStruct((n, nc), F32),
        compiler_params=_params("parallel"),
        name="in_proj",
    )(x2, gain, w_bf)


def _rope_kernel(pos_ref, inv_ref, sgn_ref, cos_ref, sin_ref):
    ang = pos_ref[...].astype(F32) * inv_ref[...]
    cos_ref[...] = jnp.cos(ang)
    sin_ref[...] = jnp.sin(ang) * sgn_ref[...]


def _rope_tables(positions):
    b, t = positions.shape
    n = b * t
    tm = min(1024, n)
    lane = jnp.arange(LANE)
    inv = jnp.power(ROPE_BASE, -jnp.arange(0, HEAD_DIM, 2, dtype=F32) / HEAD_DIM)
    inv_l = inv[lane % (HEAD_DIM // 2)][None, :]
    sgn = jnp.where(lane % HEAD_DIM < HEAD_DIM // 2, -1.0, 1.0).astype(F32)[None, :]
    cos, sin = pl.pallas_call(
        _rope_kernel,
        grid=(n // tm,),
        in_specs=[pl.BlockSpec((tm, 1), lambda i: (i, 0)),
                  pl.BlockSpec((1, LANE), lambda i: (0, 0)),
                  pl.BlockSpec((1, LANE), lambda i: (0, 0))],
        out_specs=[pl.BlockSpec((tm, LANE), lambda i: (i, 0))] * 2,
        out_shape=[jax.ShapeDtypeStruct((n, LANE), F32)] * 2,
        compiler_params=_params("parallel"),
        name="rope_tables",
    )(positions.reshape(n, 1), inv_l, sgn)
    return cos.reshape(b, t, LANE), sin.reshape(b, t, LANE)


def _ret_kernel(x_ref, cos_ref, sin_ref, lgl_ref, gn_ref, avg_ref, *rest, TB, fwd):
    if fwd:
        yo_ref, o_ref, s_ref = rest
    else:
        o_ref, s_ref = rest
    W, C = GROUP_W, CHUNK

    @pl.when(pl.program_id(1) == 0)
    def _init():
        s_ref[...] = jnp.zeros_like(s_ref)

    x = x_ref[...]
    q, k, v, g = x[:, 0:W], x[:, W:2 * W], x[:, 2 * W:3 * W], x[:, 3 * W:4 * W]
    cos = jnp.concatenate([cos_ref[...]] * (W // LANE), axis=1)
    sin = jnp.concatenate([sin_ref[...]] * (W // LANE), axis=1)
    first_half = (_iota((1, W), 1) % HEAD_DIM) < (HEAD_DIM // 2)

    def rot(z):
        swapped = jnp.where(first_half, pltpu.roll(z, W - HEAD_DIM // 2, 1), pltpu.roll(z, HEAD_DIM // 2, 1))
        return z * cos + swapped * sin

    q = rot(q)
    k = rot(k) * (HEAD_DIM ** -0.5)
    lgl = lgl_ref[...]
    row_t = _iota((C, W), 0)
    lane_s = _iota((C, W), 1) % C
    rel = ((row_t - lane_s) if fwd else (lane_s - row_t)).astype(F32)
    dmask = jnp.where(rel >= 0, jnp.exp(lgl * jnp.maximum(rel, 0.0)), 0.0)
    idx = _iota((C, 1), 0).astype(F32)
    if fwd:
        zeta = jnp.exp(lgl * (C - 1.0 - idx))
        xi = jnp.exp(lgl * (idx + 1.0))
    else:
        zeta = jnp.exp(lgl * idx)
        xi = jnp.exp(lgl * (C - idx))
    cd = jnp.exp(lgl * float(C))
    bmask = ((_iota((W, W), 0) // HEAD_DIM) == (_iota((W, W), 1) // HEAD_DIM)).astype(F32)

    n_chunks = TB // C
    order = range(n_chunks) if fwd else range(n_chunks - 1, -1, -1)
    sls = [slice(c * C, (c + 1) * C) for c in order]
    kbs = [_bf(k[sl]) for sl in sls]
    vbs = [_bf(v[sl]) for sl in sls]
    scs = [_dot_nt(q[sl], _block_diag_bf(kb)) * dmask for sl, kb in zip(sls, kbs)]
    yis = [_dot(sc, _block_diag_bf(vb)) for sc, vb in zip(scs, vbs)]
    upds = [_dot_tn(k[sl] * zeta, vb) * bmask for sl, vb in zip(sls, vbs)]
    state = s_ref[...]
    ys = []
    for j, sl in enumerate(sls):
        ys.append(yis[j] + _dot(q[sl] * xi, state))
        state = state * cd + upds[j]
    s_ref[...] = state
    y = jnp.concatenate(ys if fwd else ys[::-1], axis=0)

    if fwd:
        yt = y + yo_ref[...]
        avg = avg_ref[...]
        yc = yt - _head_stats(yt, avg)
        var = _head_stats(yc * yc, avg)
        o_ref[...] = (yc * lax.rsqrt(var + NORM_EPS) * gn_ref[...] * (g * _sigmoid(g))).astype(o_ref.dtype)
    else:
        o_ref[...] = y


def _retention(proj3, cos, sin, log_decay, gn_w, avg, tl):
    b, t, _ = proj3.shape
    tb = tl["tb_ret"]
    nb = t // tb

    def call(fwd, lg, extra):
        def cm(i):
            return i if fwd else nb - 1 - i

        in_specs = [
            pl.BlockSpec((None, tb, 1024), lambda bi, i: (bi, cm(i), COL_RET // 1024)),
            pl.BlockSpec((None, tb, LANE), lambda bi, i: (bi, cm(i), 0)),
            pl.BlockSpec((None, tb, LANE), lambda bi, i: (bi, cm(i), 0)),
            pl.BlockSpec((1, GROUP_W), lambda bi, i: (0, 0)),
            pl.BlockSpec((1, GROUP_W), lambda bi, i: (0, 0)),
            pl.BlockSpec((GROUP_W, GROUP_W), lambda bi, i: (0, 0)),
        ]
        args = [proj3, cos, sin, jnp.repeat(lg, HEAD_DIM)[None, :], gn_w[None, :], avg]
        if fwd:
            in_specs.append(pl.BlockSpec((None, tb, GROUP_W), lambda bi, i: (bi, cm(i), 0)))
            args.append(extra)
        return pl.pallas_call(
            functools.partial(_ret_kernel, TB=tb, fwd=fwd),
            grid=(b, nb),
            in_specs=in_specs,
            out_specs=pl.BlockSpec((None, tb, GROUP_W), lambda bi, i: (bi, cm(i), 0)),
            out_shape=jax.ShapeDtypeStruct((b, t, GROUP_W), BF16 if fwd else F32),
            scratch_shapes=[pltpu.VMEM((GROUP_W, GROUP_W), F32)],
            compiler_params=_params("parallel", "arbitrary"),
            name="retention_fwd" if fwd else "retention_bwd",
        )(*args)

    y_b = call(False, log_decay[1], None)
    return call(True, log_decay[0], y_b)


def _gla_kernel(x_ref, aup_ref, ab_ref, gn_ref, avg_ref, *rest, TB, fwd):
    if fwd:
        yo_ref, o_ref, st_ref = rest
    else:
        o_ref, st_ref = rest
    W, KD, C = GROUP_W, GLA_KDIM, CHUNK

    @pl.when(pl.program_id(1) == 0)
    def _init():
        st_ref[...] = jnp.zeros_like(st_ref)

    x = x_ref[...]
    q = x[:, 0:KD] * (GLA_HEAD_K ** -0.5)
    k = x[:, KD:2 * KD]
    v = x[:, 2 * KD:2 * KD + W]
    og = x[:, 2 * KD + W:2 * KD + 2 * W]
    xa = x[:, 2 * KD + 2 * W:2 * KD + 2 * W + LANE]
    z = _dot_hp(xa, aup_ref[...]) + ab_ref[...]
    la = -_softplus(-z) * (1.0 / GLA_TAU)
    tri = jnp.where(_tri_mask(C, fwd, False), 1.0, 0.0).astype(BF16)
    row_t = _iota((C, W), 0)
    lane_s = _iota((C, W), 1) % C
    m_incl = (lane_s <= row_t) if fwd else (lane_s >= row_t)
    bmask_t = ((_iota((W, KD), 0) // HEAD_DIM) == (_iota((W, KD), 1) // GLA_HEAD_K)).astype(F32)

    n_chunks = TB // C
    order = range(n_chunks) if fwd else range(n_chunks - 1, -1, -1)
    sls = [slice(c * C, (c + 1) * C) for c in order]
    bcs = [_dot_exact_lhs(tri, la[sl]) for sl in sls]
    bls = [bc[C - 1:C] if fwd else bc[0:1] for bc in bcs]
    qds = [q[sl] * jnp.exp(bc) for sl, bc in zip(sls, bcs)]
    kds = [_bf(k[sl] * jnp.exp(-bc)) for sl, bc in zip(sls, bcs)]
    khs = [k[sl] * jnp.exp(bl - bc) for sl, bc, bl in zip(sls, bcs, bls)]
    vbs = [_bf(v[sl]) for sl in sls]
    scs = [jnp.where(m_incl, _dot_nt(qd, _block_diag_bf(kd)), 0.0) for qd, kd in zip(qds, kds)]
    yis = [_dot(sc, _block_diag_bf(vb)) for sc, vb in zip(scs, vbs)]
    upds = [_dot_tn(vb, kh) * bmask_t for vb, kh in zip(vbs, khs)]
    st = st_ref[...]
    ys = []
    for j, sl in enumerate(sls):
        ys.append(yis[j] + _dot_nt(qds[j], st))
        st = st * jnp.exp(bls[j]) + upds[j]
    st_ref[...] = st
    y = jnp.concatenate(ys if fwd else ys[::-1], axis=0)

    if fwd:
        yt = y + yo_ref[...]
        var = _head_stats(yt * yt, avg_ref[...])
        o_ref[...] = (yt * lax.rsqrt(var + NORM_EPS) * gn_ref[...] * (og * _sigmoid(og))).astype(o_ref.dtype)
    else:
        o_ref[...] = y


def _gla(proj3, alpha_up, alpha_b, gn_w, avg, tl):
    b, t, _ = proj3.shape
    tb = tl["tb_gla"]
    nb = t // tb

    def call(fwd, d, extra):
        def bm(i):
            return i if fwd else nb - 1 - i

        aup = jnp.zeros((LANE, GLA_KDIM), F32).at[:alpha_up.shape[1]].set(alpha_up[d])
        in_specs = [
            pl.BlockSpec((None, tb, 1024), lambda bi, i: (bi, bm(i), COL_GLA // 1024)),
            pl.BlockSpec((LANE, GLA_KDIM), lambda bi, i: (0, 0)),
            pl.BlockSpec((1, GLA_KDIM), lambda bi, i: (0, 0)),
            pl.BlockSpec((1, GROUP_W), lambda bi, i: (0, 0)),
            pl.BlockSpec((GROUP_W, GROUP_W), lambda bi, i: (0, 0)),
        ]
        args = [proj3, aup, alpha_b[d][None, :], gn_w[None, :], avg]
        if fwd:
            in_specs.append(pl.BlockSpec((None, tb, GROUP_W), lambda bi, i: (bi, bm(i), 0)))
            args.append(extra)
        return pl.pallas_call(
            functools.partial(_gla_kernel, TB=tb, fwd=fwd),
            grid=(b, nb),
            in_specs=in_specs,
            out_specs=pl.BlockSpec((None, tb, GROUP_W), lambda bi, i: (bi, bm(i), 0)),
            out_shape=jax.ShapeDtypeStruct((b, t, GROUP_W), BF16 if fwd else F32),
            scratch_shapes=[pltpu.VMEM((GROUP_W, GLA_KDIM), F32)],
            compiler_params=_params("parallel", "arbitrary"),
            name="gla_fwd" if fwd else "gla_bwd",
        )(*args)

    y_b = call(False, 1, None)
    return call(True, 0, y_b)


def _lru_kernel(x_ref, hp_ref, hn_ref, cw_ref, cb_ref, gw_ref, gb_ref, lam_ref, *rest, TB, NB, fwd):
    if fwd:
        ho_ref, o_ref, h_ref = rest
    else:
        o_ref, h_ref = rest
    W = GROUP_W
    i = pl.program_id(1)
    blk = i if fwd else NB - 1 - i

    @pl.when(i == 0)
    def _init():
        h_ref[...] = jnp.zeros_like(h_ref)

    x = x_ref[...]
    xr, gate = x[:, 0:W], x[:, W:2 * W]
    has_prev = (blk > 0).astype(F32)
    has_next = (blk < NB - 1).astype(F32)
    prev = hp_ref[...][:, 0:W] * has_prev
    nxt = hn_ref[...][:, 0:W] * has_next
    row = _iota((TB, 1), 0)
    x_m1 = jnp.where(row == 0, prev[7:8], pltpu.roll(xr, 1, 0))
    x_m2 = jnp.where(row == 0, prev[6:7], jnp.where(row == 1, prev[7:8], pltpu.roll(xr, 2, 0)))
    x_p1 = jnp.where(row == TB - 1, nxt[0:1], pltpu.roll(xr, TB - 1, 0))
    cw = cw_ref[...]
    xc = x_m2 * cw[0:1] + x_m1 * cw[1:2] + xr * cw[2:3] + x_p1 * cw[3:4] + cb_ref[...]

    gx = _dot(xc, gw_ref[...]) + gb_ref[...]
    rec = _sigmoid(gx[:, 0:W])
    ing = _sigmoid(gx[:, W:2 * W])
    log_a = -RGLRU_C * rec * _softplus(-lam_ref[...])
    a = jnp.exp(log_a)
    bx = jnp.sqrt(-jnp.tanh(log_a) * (a * a + 1.0)) * (ing * xc)

    s = 1
    while s < TB:
        if s % SUBLANE == 0:
            ones, zeros = jnp.ones((s, W), F32), jnp.zeros((s, W), F32)
            if fwd:
                a_sh = jnp.concatenate([ones, a[0:TB - s]], axis=0)
                b_sh = jnp.concatenate([zeros, bx[0:TB - s]], axis=0)
            else:
                a_sh = jnp.concatenate([a[s:TB], ones], axis=0)
                b_sh = jnp.concatenate([bx[s:TB], zeros], axis=0)
        elif fwd:
            keep = row >= s
            a_sh = jnp.where(keep, pltpu.roll(a, s, 0), 1.0)
            b_sh = jnp.where(keep, pltpu.roll(bx, s, 0), 0.0)
        else:
            keep = row < TB - s
            a_sh = jnp.where(keep, pltpu.roll(a, TB - s, 0), 1.0)
            b_sh = jnp.where(keep, pltpu.roll(bx, TB - s, 0), 0.0)
        bx = a * b_sh + bx
        a = a * a_sh
        s *= 2
    h = a * h_ref[...] + bx
    h_ref[...] = h[TB - 1:TB] if fwd else h[0:1]

    if fwd:
        o_ref[...] = ((h + ho_ref[...]) * jax.nn.gelu(gate, approximate=True)).astype(o_ref.dtype)
    else:
        o_ref[...] = h


def _block_diag_heads(w):
    h, d, e = w.shape
    eye = jnp.eye(h, dtype=w.dtype)
    return jnp.einsum('hde,hg->hdge', w, eye).reshape(h * d, h * e)


def _rglru(proj3, conv_w, conv_b, gate_w, gate_b, lam, tl):
    b, t, _ = proj3.shape
    tb = tl["tb_lru"]
    nb = t // tb
    r8 = tb // SUBLANE
    last8 = t // SUBLANE - 1
    colb = COL_LRU // 512

    def call(fwd, d, extra):
        def bm(i):
            return i if fwd else nb - 1 - i

        gw = jnp.concatenate([_block_diag_heads(gate_w[d, 0]), _block_diag_heads(gate_w[d, 1])], axis=1)
        gb = jnp.concatenate([gate_b[d, 0], gate_b[d, 1]])[None, :]
        in_specs = [
            pl.BlockSpec((None, tb, 512), lambda bi, i: (bi, bm(i), colb)),
            pl.BlockSpec((None, SUBLANE, 512), lambda bi, i: (bi, jnp.maximum(bm(i) * r8 - 1, 0), colb)),
            pl.BlockSpec((None, SUBLANE, 512), lambda bi, i: (bi, jnp.minimum((bm(i) + 1) * r8, last8), colb)),
            pl.BlockSpec((4, GROUP_W), lambda bi, i: (0, 0)),
            pl.BlockSpec((1, GROUP_W), lambda bi, i: (0, 0)),
            pl.BlockSpec((GROUP_W, 2 * GROUP_W), lambda bi, i: (0, 0)),
            pl.BlockSpec((1, 2 * GROUP_W), lambda bi, i: (0, 0)),
            pl.BlockSpec((1, GROUP_W), lambda bi, i: (0, 0)),
        ]
        args = [proj3, proj3, proj3, conv_w, conv_b[None, :], _bf(gw), gb, lam[d][None, :]]
        if fwd:
            in_specs.append(pl.BlockSpec((None, tb, GROUP_W), lambda bi, i: (bi, bm(i), 0)))
            args.append(extra)
        return pl.pallas_call(
            functools.partial(_lru_kernel, TB=tb, NB=nb, fwd=fwd),
            grid=(b, nb),
            in_specs=in_specs,
            out_specs=pl.BlockSpec((None, tb, GROUP_W), lambda bi, i: (bi, bm(i), 0)),
            out_shape=jax.ShapeDtypeStruct((b, t, GROUP_W), BF16 if fwd else F32),
            scratch_shapes=[pltpu.VMEM((1, GROUP_W), F32)],
            compiler_params=_params("parallel", "arbitrary"),
            name="rglru_fwd" if fwd else "rglru_bwd",
        )(*args)

    h_b = call(False, 1, None)
    return call(True, 0, h_b)


def _rwkv_kernel(x_ref, halo_ref, mu_ref, lora_ref, w0_ref, a0_ref, kk_ref, ka_ref, rk_ref, ones_ref,
                 *rest, TB, NB, fwd, GROUP):
    if fwd:
        gup_ref, gn_ref, avg_ref, yo_ref, bo_ref, o_ref, s_ref = rest
    else:
        y_ref, b_ref, s_ref = rest
    W, C = GROUP_W, CHUNK
    i = pl.program_id(1)
    blk = i if fwd else NB - 1 - i

    @pl.when(i == 0)
    def _init():
        s_ref[...] = jnp.zeros_like(s_ref)

    x = x_ref[...]
    NS = 3 * W + LANE
    zs = x[:, 0:NS]
    halo = halo_ref[...][:, 0:NS]
    if fwd:
        edge = halo[SUBLANE - 1:SUBLANE] * (blk > 0).astype(F32)
    else:
        edge = halo[0:1] * (blk < NB - 1).astype(F32)
    zs = zs + mu_ref[...] * (_shift_rows(zs, edge, fwd) - zs)
    r, k, v, xwa = zs[:, 0:W], zs[:, W:2 * W], zs[:, 2 * W:3 * W], zs[:, 3 * W:NS]
    lane_l = _iota((1, LANE), 1)
    xwa = jnp.where(lane_l < LANE // 2, jnp.tanh(xwa), xwa)
    lo = _dot_hp(xwa, lora_ref[...])
    w_log = -_softplus(-(w0_ref[...] + lo[:, 0:W])) - 0.5
    lw = -jnp.exp(w_log)
    a = _sigmoid(a0_ref[...] + lo[:, W:2 * W])
    ones_bd = ones_ref[...]
    kk = k * kk_ref[...]
    kk = kk / jnp.maximum(jnp.sqrt(_dot_exact_rhs(kk * kk, ones_bd)), 1e-12)
    k = k * (1.0 + (a - 1.0) * ka_ref[...])
    bonus = _dot_exact_rhs(r * k * rk_ref[...], ones_bd) * v
    am = -kk
    bm = kk * a

    tri = jnp.where(_tri_mask(C, fwd, False), 1.0, 0.0).astype(BF16)
    row_t = _iota((C, W), 0)
    lane_s = _iota((C, W), 1) % C
    m_strict = (lane_s < row_t) if fwd else (lane_s > row_t)
    m_incl = (lane_s <= row_t) if fwd else (lane_s >= row_t)
    eye_c = (lane_s == row_t).astype(F32)
    bmask = (_iota((W, W), 0) // HEAD_DIM) == (_iota((W, W), 1) // HEAD_DIM)

    def bd(z):
        return _block_diag_bf(_bf(z))

    n_chunks = TB // C
    order = range(n_chunks) if fwd else range(n_chunks - 1, -1, -1)
    def state_free_part(sls):
        cums = [_dot_exact_lhs(tri, lw[sl]) for sl in sls]
        tots = [cum[C - 1:C] if fwd else cum[0:1] for cum in cums]
        e_negs = [jnp.exp(-cum) for cum in cums]
        e_ends = [jnp.exp(tot - cum) for tot, cum in zip(tots, cums)]
        a_ts = [am[sl] * jnp.exp(cum - lw[sl]) for sl, cum in zip(sls, cums)]
        r_ts = [r[sl] * jnp.exp(cum) for sl, cum in zip(sls, cums)]
        yield None
        gs = [_dot_nt(jnp.concatenate([a_t, r_t], axis=0),
                      jnp.concatenate([bd(bm[sl] * e_neg), bd(k[sl] * e_neg)], axis=0))
              for sl, a_t, r_t, e_neg in zip(sls, a_ts, r_ts, e_negs)]
        yield None
        a_abs = [jnp.where(m_strict, g[0:C, 0:W], 0.0) for g in gs]
        a_aks = [jnp.where(m_strict, g[0:C, W:2 * W], 0.0) for g in gs]
        a_rbs = [jnp.where(m_incl, g[C:2 * C, 0:W], 0.0) for g in gs]
        a_rks = [jnp.where(m_incl, g[C:2 * C, W:2 * W], 0.0) for g in gs]

        tinvs = [eye_c + a_ab for a_ab in a_abs]
        ps = [_dot(a_ab, bd(a_ab)) for a_ab in a_abs]
        avs = [_dot(jnp.concatenate([a_ak, a_rk], axis=0), bd(v[sl]))
               for a_ak, a_rk, sl in zip(a_aks, a_rks, sls)]
        yield None
        n_terms = 2
        while 2 * n_terms < C:
            xps = [_dot(jnp.concatenate([tinv, p], axis=0), bd(p)) for p, tinv in zip(ps, tinvs)]
            tinvs = [tinv + xp[0:C] for tinv, xp in zip(tinvs, xps)]
            ps = [xp[C:2 * C] for xp in xps]
            n_terms *= 2
            yield None
        tinvs = [tinv + _dot(tinv, bd(p)) for p, tinv in zip(ps, tinvs)]
        yield None
        w12s = [_dot(tinv, jnp.concatenate([bd(a_t), bd(av[0:C])], axis=1))
                for tinv, a_t, av in zip(tinvs, a_ts, avs)]
        yield None
        qys = [_dot(a_rb, jnp.concatenate([bd(w12[:, 0:W]), bd(w12[:, W:2 * W])], axis=1))
               for a_rb, w12 in zip(a_rbs, w12s)]
        yield [dict(sl=sl, w1=w12[:, 0:W], w2=w12[:, W:2 * W], qp=r_t + qy[:, 0:W],
                    y0=av[C:2 * C] + qy[:, W:2 * W], decay=jnp.exp(tot),
                    kb=jnp.concatenate([k[sl] * e_end, bm[sl] * e_end], axis=0))
               for sl, w12, r_t, qy, av, tot, e_end in zip(sls, w12s, r_ts, qys, avs, tots, e_ends)]

    def recur(state, p_):
        sl = p_["sl"]
        uy = _dot_nt(jnp.concatenate([p_["w1"], p_["qp"]], axis=0), state)
        u = uy[0:C] + p_["w2"]
        y = uy[C:2 * C] + p_["y0"]
        upd = _dot_tn(jnp.concatenate([v[sl], u], axis=0), p_["kb"])
        ys.append(y)
        return state * p_["decay"] + jnp.where(bmask, upd, 0.0)

    ys = []

    sls_all = [slice(c * C, (c + 1) * C) for c in order]
    groups = GROUP if sum(GROUP) == n_chunks else (n_chunks,)
    state = s_ref[...]
    pending = []
    g0 = 0
    for group in groups:
        done = None
        for out in state_free_part(sls_all[g0:g0 + group]):
            if out is not None:
                done = out
            elif pending:
                state = recur(state, pending.pop(0))
        while pending:
            state = recur(state, pending.pop(0))
        pending = done
        g0 += group
    while pending:
        state = recur(state, pending.pop(0))
    s_ref[...] = state
    y = jnp.concatenate(ys if fwd else ys[::-1], axis=0)

    if fwd:
        yt = y + yo_ref[...]
        avg = avg_ref[...]
        yc = yt - _head_stats(yt, avg)
        var = _head_stats(yc * yc, avg)
        o = yc * lax.rsqrt(var + RWKV_GN_EPS) * gn_ref[...] + bonus + bo_ref[...]
        xg = x[:, NS:NS + LANE]
        o_ref[...] = (o * _dot_hp(_sigmoid(xg), gup_ref[...])).astype(o_ref.dtype)
    else:
        y_ref[...] = y
        b_ref[...] = bonus


def _rwkv7(proj3, mu_rkv, mu_w, mu_a, w0, w_up, a0, a_up, g_up, k_k, k_a, r_k, gn_w, avg, ones_bd, tl):
    b, t, _ = proj3.shape
    tb = tl["tb_rwkv"]
    nb = t // tb
    r8 = tb // SUBLANE
    last8 = t // SUBLANE - 1
    colb = COL_RWKV // 1024
    W = GROUP_W
    mu = jnp.concatenate([mu_rkv.reshape(-1), mu_w, mu_a])[None, :]
    nl = w_up.shape[1]

    def call(fwd, d, extra):
        def bm(i):
            return i if fwd else nb - 1 - i

        lora = jnp.zeros((LANE, 2 * W), F32).at[0:nl, 0:W].set(w_up[d]).at[nl:2 * nl, W:2 * W].set(a_up[d])
        if fwd:
            halo_map = lambda bi, i: (bi, jnp.maximum(bm(i) * r8 - 1, 0), colb)
        else:
            halo_map = lambda bi, i: (bi, jnp.minimum((bm(i) + 1) * r8, last8), colb)
        row = lambda n: pl.BlockSpec((1, n), lambda bi, i: (0, 0))
        in_specs = [
            pl.BlockSpec((None, tb, 1024), lambda bi, i: (bi, bm(i), colb)),
            pl.BlockSpec((None, SUBLANE, 1024), halo_map),
            row(3 * W + LANE),
            pl.BlockSpec((LANE, 2 * W), lambda bi, i: (0, 0)),
            row(W), row(W), row(W), row(W), row(W),
            pl.BlockSpec((W, W), lambda bi, i: (0, 0)),
        ]
        args = [proj3, proj3, mu, lora, w0[d][None, :], a0[d][None, :], k_k[None, :], k_a[None, :],
                r_k.reshape(1, W), ones_bd]
        blk = pl.BlockSpec((None, tb, W), lambda bi, i: (bi, bm(i), 0))
        if fwd:
            in_specs += [pl.BlockSpec((LANE, W), lambda bi, i: (0, 0)), row(W),
                         pl.BlockSpec((W, W), lambda bi, i: (0, 0)), blk, blk]
            args += [g_up, gn_w[None, :], avg, extra[0], extra[1]]
            out_specs = blk
            out_shape = jax.ShapeDtypeStruct((b, t, W), BF16)
        else:
            out_specs = [blk, blk]
            out_shape = [jax.ShapeDtypeStruct((b, t, W), F32)] * 2
        return pl.pallas_call(
            functools.partial(_rwkv_kernel, TB=tb, NB=nb, fwd=fwd, GROUP=tl["rwkv_group"]),
            grid=(b, nb),
            in_specs=in_specs,
            out_specs=out_specs,
            out_shape=out_shape,
            scratch_shapes=[pltpu.VMEM((W, W), F32)],
            compiler_params=_params("parallel", "arbitrary"),
            name="rwkv7_fwd" if fwd else "rwkv7_bwd",
        )(*args)

    y_b, bonus_b = call(False, 1, None)
    return call(True, 0, (y_b, bonus_b))


def _out_proj_kernel(x_ref, o1_ref, o2_ref, o3_ref, o4_ref, w_ref, g_ref, rw_ref, rb_ref,
                     xo_ref, xn_ref, aff_ref, afft_ref):
    W = GROUP_W
    w = w_ref[...]
    acc = x_ref[...]
    for gi, o_ref in enumerate((o1_ref, o2_ref, o3_ref, o4_ref)):
        acc = acc + jnp.dot(_bf(o_ref[...]), w[gi * W:(gi + 1) * W], preferred_element_type=F32)
    xo_ref[...] = acc
    ms = jnp.mean(acc * acc, axis=-1, keepdims=True)
    xn = acc * lax.rsqrt(ms + NORM_EPS) * g_ref[...]
    xn_ref[...] = _bf(xn)
    raw = _dot_hp(xn, rw_ref[...]) + rb_ref[...]
    logits = raw - jnp.max(raw, axis=-1, keepdims=True)
    e = jnp.exp(logits)
    aff_ref[...] = e / jnp.sum(e, axis=-1, keepdims=True)
    lt = raw.T[0:afft_ref.shape[0]]
    lt = lt - jnp.max(lt, axis=0, keepdims=True)
    et = jnp.exp(lt)
    afft_ref[...] = et / jnp.sum(et, axis=0, keepdims=True)


def _out_proj(x2, outs, w_out_bf, gain, router_w, router_b, b, t, tl):
    n, d = x2.shape
    tm = tl["tm_out"]
    tpb = t // tm
    E = N_EXPERTS
    rw = jnp.zeros((d, LANE), F32).at[:, :E].set(router_w)
    rb = jnp.full((1, LANE), -1e30, F32).at[0, :E].set(router_b)
    full = lambda r, c: pl.BlockSpec((r, c), lambda i: (0, 0))
    tile = lambda c: pl.BlockSpec((tm, c), lambda i: (i, 0))
    return pl.pallas_call(
        _out_proj_kernel,
        grid=(n // tm,),
        in_specs=[tile(d)] + [tile(GROUP_W)] * 4 + [full(d, d), full(1, d), full(d, LANE), full(1, LANE)],
        out_specs=[tile(d), tile(d), tile(LANE),
                   pl.BlockSpec((None, E, tm), lambda i: (i // tpb, 0, i % tpb))],
        out_shape=[jax.ShapeDtypeStruct((n, d), F32), jax.ShapeDtypeStruct((n, d), BF16),
                   jax.ShapeDtypeStruct((n, LANE), F32), jax.ShapeDtypeStruct((b, E, t), F32)],
        compiler_params=_params("parallel"),
        name="out_proj_router",
    )(x2, *outs, w_out_bf, gain, rw, rb)


def _prefix_count(m, triu):
    e, t = m.shape
    nblk = t // LANE
    stacked = jnp.concatenate([m[:, j * LANE:(j + 1) * LANE] for j in range(nblk)], axis=0)
    incl = jnp.dot(_bf(stacked), triu, preferred_element_type=F32)
    pieces = []
    off = jnp.zeros((e, 1), F32)
    for j in range(nblk):
        blk = incl[j * e:(j + 1) * e]
        pieces.append(blk - stacked[j * e:(j + 1) * e] + off)
        off = off + blk[:, LANE - 1:LANE]
    return jnp.concatenate(pieces, axis=1)


def _select_kernel(afft_ref, rank_ref, rankc_ref, *, cap):
    aff = afft_ref[...]
    e, t = aff.shape
    bits = lax.bitcast_convert_type(aff, jnp.int32)

    def body(i, prefix):
        cand = prefix | jnp.left_shift(jnp.int32(1), 30 - i)
        cnt = jnp.sum((bits >= cand).astype(F32), axis=1, keepdims=True)
        return jnp.where(cnt >= cap, cand, prefix)

    thr = lax.fori_loop(0, 31, body, jnp.zeros((e, 1), jnp.int32))
    gt = (bits > thr).astype(F32)
    eq = (bits == thr).astype(F32)
    need = cap - jnp.sum(gt, axis=1, keepdims=True)
    triu = jnp.where(_tri_mask(LANE, False, False), 1.0, 0.0).astype(BF16)
    sel = gt + eq * (_prefix_count(eq, triu) < need).astype(F32)
    rank = jnp.where(sel > 0.5, _prefix_count(sel, triu), -1.0)
    rank_ref[...] = rank
    padded = jnp.concatenate([rank, jnp.full((LANE - e, t), -1.0, F32)], axis=0)
    rankc_ref[...] = padded.T


def _select(afft, cap):
    b, e, t = afft.shape
    return pl.pallas_call(
        functools.partial(_select_kernel, cap=cap),
        grid=(b,),
        in_specs=[pl.BlockSpec((None, e, t), lambda i: (i, 0, 0))],
        out_specs=[pl.BlockSpec((None, e, t), lambda i: (i, 0, 0)),
                   pl.BlockSpec((None, t, LANE), lambda i: (i, 0, 0))],
        out_shape=[jax.ShapeDtypeStruct((b, e, t), F32), jax.ShapeDtypeStruct((b, t, LANE), F32)],
        compiler_params=_params("parallel"),
        name="expert_choice_select",
    )(afft)


def _gather_kernel(rank_ref, xn_ref, xs_ref, acc_ref, *, cap, tk, group):
    g0 = pl.program_id(1) * group
    t = xn_ref.shape[0]
    slot = _iota((cap, tk), 0).astype(F32)
    for j in range(t // tk):
        onehot = jnp.concatenate(
            [jnp.where(rank_ref[pl.ds(g0 + g, 1), j * tk:(j + 1) * tk] == slot, 1.0, 0.0).astype(BF16)
             for g in range(group)], axis=0)
        part = jnp.dot(onehot, xn_ref[j * tk:(j + 1) * tk, :], preferred_element_type=F32)
        if j == 0:
            acc_ref[...] = part
        else:
            acc_ref[...] += part
    for g in range(group):
        xs_ref[g] = _bf(acc_ref[g * cap:(g + 1) * cap, :])


def _gather(rank, xn3, cap, tl):
    b, e, t = rank.shape
    d = xn3.shape[2]
    group = tl["gather_group"]
    return pl.pallas_call(
        functools.partial(_gather_kernel, cap=cap, tk=tl["tk_gather"], group=group),
        grid=(b, e // group),
        in_specs=[pl.BlockSpec((None, e, t), lambda bi, gi: (bi, 0, 0)),
                  pl.BlockSpec((None, t, d), lambda bi, gi: (bi, 0, 0))],
        out_specs=pl.BlockSpec((None, group, cap, d), lambda bi, gi: (bi, gi, 0, 0)),
        out_shape=jax.ShapeDtypeStruct((b, e, cap, d), BF16),
        scratch_shapes=[pltpu.VMEM((group * cap, d), F32)],
        compiler_params=_params("parallel", "arbitrary"),
        name="moe_gather",
    )(rank, xn3)


def _ffn_kernel(xs_ref, wg_ref, wu_ref, wd_ref, o_ref, wgb_ref, wub_ref, wdb_ref):
    @pl.when(pl.program_id(1) == 0)
    def _():
        wgb_ref[...] = _bf(wg_ref[...])
        wub_ref[...] = _bf(wu_ref[...])
        wdb_ref[...] = _bf(wd_ref[...])

    nseq = xs_ref.shape[0]
    xs = [xs_ref[i] for i in range(nseq)]
    hg = [jnp.dot(x, wgb_ref[...], preferred_element_type=F32) for x in xs]
    hu = [jnp.dot(x, wub_ref[...], preferred_element_type=F32) for x in xs]
    hid = [_bf(g * _sigmoid(g) * u) for g, u in zip(hg, hu)]
    for i in range(nseq):
        o_ref[i] = _bf(jnp.dot(hid[i], wdb_ref[...], preferred_element_type=F32))


def _ffn(xs, w_gate, w_up, w_down, layer, tl):
    b, e, cap, d = xs.shape
    f = w_gate.shape[3]
    nseq = min(tl["ffn_seqs"], b)
    wspec = lambda r, c: pl.BlockSpec((None, None, r, c), lambda ei, bi: (layer, ei, 0, 0))
    xspec = pl.BlockSpec((nseq, None, cap, d), lambda ei, bi: (bi, ei, 0, 0))
    return pl.pallas_call(
        _ffn_kernel,
        grid=(e, b // nseq),
        in_specs=[xspec, wspec(d, f), wspec(d, f), wspec(f, d)],
        out_specs=xspec,
        out_shape=jax.ShapeDtypeStruct((b, e, cap, d), BF16),
        scratch_shapes=[pltpu.VMEM((d, f), BF16), pltpu.VMEM((d, f), BF16), pltpu.VMEM((f, d), BF16)],
        compiler_params=_params("arbitrary", "arbitrary"),
        name="moe_ffn",
    )(xs, w_gate, w_up, w_down)


def _combine_kernel(x_ref, rankc_ref, aff_ref, o_ref, g_ref, out_ref, *, cap, final_norm):
    rankc = rankc_ref[...]
    aff = aff_ref[...]
    tm = x_ref.shape[0]
    slot = _iota((tm, cap), 1).astype(F32)
    acc = x_ref[...]
    for e in range(N_EXPERTS):
        pt = jnp.where(rankc[:, e:e + 1] == slot, aff[:, e:e + 1], 0.0)
        acc = acc + jnp.dot(_bf(pt), o_ref[e * cap:(e + 1) * cap, :], preferred_element_type=F32)
    if final_norm:
        ms = jnp.mean(acc * acc, axis=-1, keepdims=True)
        acc = acc * lax.rsqrt(ms + NORM_EPS) * g_ref[...]
    out_ref[...] = acc


def _combine(x3, rankc, aff3, o_flat, gain, cap, final_norm, tl):
    b, t, d = x3.shape
    tm = tl["tm_comb"]
    tile = lambda c: pl.BlockSpec((None, tm, c), lambda bi, i: (bi, i, 0))
    return pl.pallas_call(
        functools.partial(_combine_kernel, cap=cap, final_norm=final_norm),
        grid=(b, t // tm),
        in_specs=[tile(d), tile(LANE), tile(LANE),
                  pl.BlockSpec((None, N_EXPERTS * cap, d), lambda bi, i: (bi, 0, 0)),
                  pl.BlockSpec((1, d), lambda bi, i: (0, 0))],
        out_specs=tile(d),
        out_shape=jax.ShapeDtypeStruct((b, t, d), F32),
        compiler_params=pltpu.CompilerParams(dimension_semantics=("parallel", "arbitrary"),
                                             vmem_limit_bytes=VMEM_LIMIT_COMBINE),
        name="moe_combine",
    )(x3, rankc, aff3, o_flat, gain)


def kernel(x, positions, norm_mix, w_in, w_out, ret_log_decay, ret_gn, rwkv_mu_rkv, rwkv_mu_w, rwkv_mu_a, rwkv_w0, rwkv_w_up, rwkv_a0, rwkv_a_up, rwkv_g_up, rwkv_k_k, rwkv_k_a, rwkv_r_k, rwkv_gn, lru_conv_w, lru_conv_b, lru_gate_w, lru_gate_b, lru_lambda, gla_alpha_up, gla_alpha_b, gla_gn, norm_ffn, router_w, router_b, exp_w_gate, exp_w_up, exp_w_down, norm_final):
    b, t, d = x.shape
    depth = w_in.shape[0]
    n = b * t
    tl = _tiles(t)
    cap = EC_CAPACITY_FACTOR * t // N_EXPERTS
    lane = jnp.arange(GROUP_W)
    same_head = (lane[:, None] // HEAD_DIM) == (lane[None, :] // HEAD_DIM)
    avg = jnp.where(same_head, 1.0 / HEAD_DIM, 0.0).astype(BF16)
    ones_bd = jnp.where(same_head, 1.0, 0.0).astype(BF16)
    cos, sin = _rope_tables(positions)

    x2 = x.reshape(n, d)
    for l in range(depth):
        w = w_in[l]
        split = 2048 + 512
        w_perm = jnp.concatenate(
            [w[:, :2048], w[:, split:], jnp.zeros((d, IN_COLS_PAD - w.shape[1]), F32), w[:, 2048:split]], axis=1)
        proj3 = _in_proj(x2, norm_mix[l][None, :], _bf(w_perm), tl).reshape(b, t, IN_COLS_PAD)

        o_ret = _retention(proj3, cos, sin, ret_log_decay[l], ret_gn[l], avg, tl)
        o_rwkv = _rwkv7(proj3, rwkv_mu_rkv[l], rwkv_mu_w[l], rwkv_mu_a[l], rwkv_w0[l], rwkv_w_up[l],
                        rwkv_a0[l], rwkv_a_up[l], rwkv_g_up[l], rwkv_k_k[l], rwkv_k_a[l], rwkv_r_k[l],
                        rwkv_gn[l], avg, ones_bd, tl)
        o_lru = _rglru(proj3, lru_conv_w[l], lru_conv_b[l], lru_gate_w[l], lru_gate_b[l], lru_lambda[l], tl)
        o_gla = _gla(proj3, gla_alpha_up[l], gla_alpha_b[l], gla_gn[l], avg, tl)
        outs = [o.reshape(n, GROUP_W) for o in (o_ret, o_rwkv, o_lru, o_gla)]

        x_mid, xn, aff, afft = _out_proj(x2, outs, _bf(w_out[l]), norm_ffn[l][None, :], router_w[l],
                                         router_b[l], b, t, tl)
        rank, rankc = _select(afft, cap)
        xs = _gather(rank, xn.reshape(b, t, d), cap, tl)
        o_exp = _ffn(xs, exp_w_gate, exp_w_up, exp_w_down, l, tl)
        x3 = _combine(x_mid.reshape(b, t, d), rankc, aff.reshape(b, t, LANE),
                      o_exp.reshape(b, N_EXPERTS * cap, d), norm_final[None, :], cap, l == depth - 1, tl)
        x2 = x3.reshape(n, d)
    return x2.reshape(b, t, d)
```

```python
import functools

import jax
import jax.numpy as jnp
from jax import lax
from jax.experimental import pallas as pl
from jax.experimental.pallas import tpu as pltpu

F32 = jnp.float32
BF16 = jnp.bfloat16

D_MODEL = 1024
GROUP_W = 256
N_HEADS = 4
HEAD_DIM = 64
ROPE_BASE = 10000.0
RWKV_GN_EPS = 64e-5
RGLRU_C = 8.0
GLA_KDIM = 128
GLA_HEAD_K = 32
GLA_TAU = 16.0
N_EXPERTS = 16
EC_CAPACITY_FACTOR = 2
NORM_EPS = 1e-6

LANE = 128
SUBLANE = 8
CHUNK = 64
VMEM_LIMIT = 56 * 1024 * 1024
VMEM_LIMIT_COMBINE = 62 * 1024 * 1024

COL_RET, COL_RWKV, COL_GLA, COL_LRU = 0, 1024, 2048, 3072
IN_COLS_PAD = 3584


def _tiles(T):
    return dict(
        tm_proj=min(512, T), tn_proj=512,
        tb_ret=min(1024, T), tb_gla=min(1024, T), tb_lru=min(512, T), tb_rwkv=min(1024, T),
        tm_out=min(512, T), tm_comb=min(1024, T), tk_gather=min(1024, T), gather_group=4, ffn_seqs=2, rwkv_group=(8, 6, 2),
    )


def _params(*sem):
    return pltpu.CompilerParams(dimension_semantics=sem, vmem_limit_bytes=VMEM_LIMIT)


def _bf(x):
    return x.astype(BF16)


def _dot(a, b):
    return jnp.dot(_bf(a), _bf(b), preferred_element_type=F32)


def _dot_nt(a, b):
    return lax.dot_general(_bf(a), _bf(b), (((1,), (1,)), ((), ())), preferred_element_type=F32)


def _dot_tn(a, b):
    return lax.dot_general(_bf(a), _bf(b), (((0,), (0,)), ((), ())), preferred_element_type=F32)


def _split2(a):
    hi = _bf(a)
    lo = _bf(a - hi.astype(F32))
    return hi, lo


def _split3(a):
    hi = _bf(a)
    r = a - hi.astype(F32)
    mid = _bf(r)
    lo = _bf(r - mid.astype(F32))
    return hi, mid, lo


def _dot_exact_lhs(m, a):
    hi, mid, lo = _split3(a)
    return (jnp.dot(m, hi, preferred_element_type=F32) + jnp.dot(m, mid, preferred_element_type=F32)
            + jnp.dot(m, lo, preferred_element_type=F32))


def _dot_exact_rhs(a, m):
    hi, lo = _split2(a)
    return jnp.dot(hi, m, preferred_element_type=F32) + jnp.dot(lo, m, preferred_element_type=F32)


def _dot_hp(a, b):
    ah, al = _split2(a)
    bh, bl = _split2(b)
    return (jnp.dot(ah, bh, preferred_element_type=F32) + jnp.dot(ah, bl, preferred_element_type=F32)
            + jnp.dot(al, bh, preferred_element_type=F32))


def _sigmoid(x):
    return 1.0 / (1.0 + jnp.exp(-x))


def _softplus(x):
    return jnp.maximum(x, 0.0) + jnp.log(1.0 + jnp.exp(-jnp.abs(x)))


def _iota(shape, dim):
    return lax.broadcasted_iota(jnp.int32, shape, dim)


def _block_diag_bf(z):
    w = z.shape[1]
    lane_head = _iota((1, w), 1) // (w // N_HEADS)
    zero = jnp.zeros_like(z)
    return jnp.concatenate([jnp.where(lane_head == h, z, zero) for h in range(N_HEADS)], axis=0)


def _tri_mask(n, lower, strict):
    i, j = _iota((n, n), 0), _iota((n, n), 1)
    if lower:
        return (j < i) if strict else (j <= i)
    return (j > i) if strict else (j >= i)


def _head_stats(y, avg):
    return _dot_exact_rhs(y, avg)


def _shift_rows(z, edge, fwd):
    n = z.shape[0]
    row = _iota((n, 1), 0)
    if fwd:
        return jnp.where(row == 0, edge, pltpu.roll(z, 1, 0))
    return jnp.where(row == n - 1, edge, pltpu.roll(z, n - 1, 0))


def _in_proj_kernel(x_ref, g_ref, w_ref, o_ref, *, tn):
    x = x_ref[...]
    ms = jnp.mean(x * x, axis=-1, keepdims=True)
    xn = _bf(x * lax.rsqrt(ms + NORM_EPS) * g_ref[...])
    for c in range(o_ref.shape[1] // tn):
        o_ref[:, c * tn:(c + 1) * tn] = jnp.dot(xn, w_ref[:, c * tn:(c + 1) * tn], preferred_element_type=F32)


def _in_proj(x2, gain, w_bf, tl):
    n, d = x2.shape
    tm, tn = tl["tm_proj"], tl["tn_proj"]
    nc = w_bf.shape[1]
    return pl.pallas_call(
        functools.partial(_in_proj_kernel, tn=tn),
        grid=(n // tm,),
        in_specs=[pl.BlockSpec((tm, d), lambda i: (i, 0)),
                  pl.BlockSpec((1, d), lambda i: (0, 0)),
                  pl.BlockSpec((d, nc), lambda i: (0, 0), pipeline_mode=pl.Buffered(1))],
        out_specs=pl.BlockSpec((tm, nc), lambda i: (i, 0)),
        out_shape=jax.ShapeDtypeStruct((n, nc), F32),
        compiler_params=_params("parallel"),
        name="in_proj",
    )(x2, gain, w_bf)


def _rope_kernel(pos_ref, inv_ref, sgn_ref, cos_ref, sin_ref):
    ang = pos_ref[...].astype(F32) * inv_ref[...]
    cos_ref[...] = jnp.cos(ang)
    sin_ref[...] = jnp.sin(ang) * sgn_ref[...]


def _rope_tables(positions):
    b, t = positions.shape
    n = b * t
    tm = min(1024, n)
    lane = jnp.arange(LANE)
    inv = jnp.power(ROPE_BASE, -jnp.arange(0, HEAD_DIM, 2, dtype=F32) / HEAD_DIM)
    inv_l = inv[lane % (HEAD_DIM // 2)][None, :]
    sgn = jnp.where(lane % HEAD_DIM < HEAD_DIM // 2, -1.0, 1.0).astype(F32)[None, :]
    cos, sin = pl.pallas_call(
        _rope_kernel,
        grid=(n // tm,),
        in_specs=[pl.BlockSpec((tm, 1), lambda i: (i, 0)),
                  pl.BlockSpec((1, LANE), lambda i: (0, 0)),
                  pl.BlockSpec((1, LANE), lambda i: (0, 0))],
        out_specs=[pl.BlockSpec((tm, LANE), lambda i: (i, 0))] * 2,
        out_shape=[jax.ShapeDtypeStruct((n, LANE), F32)] * 2,
        compiler_params=_params("parallel"),
        name="rope_tables",
    )(positions.reshape(n, 1), inv_l, sgn)
    return cos.reshape(b, t, LANE), sin.reshape(b, t, LANE)


def _ret_kernel(x_ref, cos_ref, sin_ref, lgl_ref, gn_ref, avg_ref, *rest, TB, fwd):
    if fwd:
        yo_ref, o_ref, s_ref = rest
    else:
        o_ref, s_ref = rest
    W, C = GROUP_W, CHUNK

    @pl.when(pl.program_id(1) == 0)
    def _init():
        s_ref[...] = jnp.zeros_like(s_ref)

    x = x_ref[...]
    q, k, v, g = x[:, 0:W], x[:, W:2 * W], x[:, 2 * W:3 * W], x[:, 3 * W:4 * W]
    cos = jnp.concatenate([cos_ref[...]] * (W // LANE), axis=1)
    sin = jnp.concatenate([sin_ref[...]] * (W // LANE), axis=1)
    first_half = (_iota((1, W), 1) % HEAD_DIM) < (HEAD_DIM // 2)

    def rot(z):
        swapped = jnp.where(first_half, pltpu.roll(z, W - HEAD_DIM // 2, 1), pltpu.roll(z, HEAD_DIM // 2, 1))
        return z * cos + swapped * sin

    q = rot(q)
    k = rot(k) * (HEAD_DIM ** -0.5)
    lgl = lgl_ref[...]
    row_t = _iota((C, W), 0)
    lane_s = _iota((C, W), 1) % C
    rel = ((row_t - lane_s) if fwd else (lane_s - row_t)).astype(F32)
    dmask = jnp.where(rel >= 0, jnp.exp(lgl * jnp.maximum(rel, 0.0)), 0.0)
    idx = _iota((C, 1), 0).astype(F32)
    if fwd:
        zeta = jnp.exp(lgl * (C - 1.0 - idx))
        xi = jnp.exp(lgl * (idx + 1.0))
    else:
        zeta = jnp.exp(lgl * idx)
        xi = jnp.exp(lgl * (C - idx))
    cd = jnp.exp(lgl * float(C))
    bmask = ((_iota((W, W), 0) // HEAD_DIM) == (_iota((W, W), 1) // HEAD_DIM)).astype(F32)

    n_chunks = TB // C
    order = range(n_chunks) if fwd else range(n_chunks - 1, -1, -1)
    sls = [slice(c * C, (c + 1) * C) for c in order]
    kbs = [_bf(k[sl]) for sl in sls]
    vbs = [_bf(v[sl]) for sl in sls]
    scs = [_dot_nt(q[sl], _block_diag_bf(kb)) * dmask for sl, kb in zip(sls, kbs)]
    yis = [_dot(sc, _block_diag_bf(vb)) for sc, vb in zip(scs, vbs)]
    upds = [_dot_tn(k[sl] * zeta, vb) * bmask for sl, vb in zip(sls, vbs)]
    state = s_ref[...]
    ys = []
    for j, sl in enumerate(sls):
        ys.append(yis[j] + _dot(q[sl] * xi, state))
        state = state * cd + upds[j]
    s_ref[...] = state
    y = jnp.concatenate(ys if fwd else ys[::-1], axis=0)

    if fwd:
        yt = y + yo_ref[...]
        avg = avg_ref[...]
        yc = yt - _head_stats(yt, avg)
        var = _head_stats(yc * yc, avg)
        o_ref[...] = (yc * lax.rsqrt(var + NORM_EPS) * gn_ref[...] * (g * _sigmoid(g))).astype(o_ref.dtype)
    else:
        o_ref[...] = y


def _retention(proj3, cos, sin, log_decay, gn_w, avg, tl):
    b, t, _ = proj3.shape
    tb = tl["tb_ret"]
    nb = t // tb

    def call(fwd, lg, extra):
        def cm(i):
            return i if fwd else nb - 1 - i

        in_specs = [
            pl.BlockSpec((None, tb, 1024), lambda bi, i: (bi, cm(i), COL_RET // 1024)),
            pl.BlockSpec((None, tb, LANE), lambda bi, i: (bi, cm(i), 0)),
            pl.BlockSpec((None, tb, LANE), lambda bi, i: (bi, cm(i), 0)),
            pl.BlockSpec((1, GROUP_W), lambda bi, i: (0, 0)),
            pl.BlockSpec((1, GROUP_W), lambda bi, i: (0, 0)),
            pl.BlockSpec((GROUP_W, GROUP_W), lambda bi, i: (0, 0)),
        ]
        args = [proj3, cos, sin, jnp.repeat(lg, HEAD_DIM)[None, :], gn_w[None, :], avg]
        if fwd:
            in_specs.append(pl.BlockSpec((None, tb, GROUP_W), lambda bi, i: (bi, cm(i), 0)))
            args.append(extra)
        return pl.pallas_call(
            functools.partial(_ret_kernel, TB=tb, fwd=fwd),
            grid=(b, nb),
            in_specs=in_specs,
            out_specs=pl.BlockSpec((None, tb, GROUP_W), lambda bi, i: (bi, cm(i), 0)),
            out_shape=jax.ShapeDtypeStruct((b, t, GROUP_W), BF16 if fwd else F32),
            scratch_shapes=[pltpu.VMEM((GROUP_W, GROUP_W), F32)],
            compiler_params=_params("parallel", "arbitrary"),
            name="retention_fwd" if fwd else "retention_bwd",
        )(*args)

    y_b = call(False, log_decay[1], None)
    return call(True, log_decay[0], y_b)


def _gla_kernel(x_ref, aup_ref, ab_ref, gn_ref, avg_ref, *rest, TB, fwd):
    if fwd:
        yo_ref, o_ref, st_ref = rest
    else:
        o_ref, st_ref = rest
    W, KD, C = GROUP_W, GLA_KDIM, CHUNK

    @pl.when(pl.program_id(1) == 0)
    def _init():
        st_ref[...] = jnp.zeros_like(st_ref)

    x = x_ref[...]
    q = x[:, 0:KD] * (GLA_HEAD_K ** -0.5)
    k = x[:, KD:2 * KD]
    v = x[:, 2 * KD:2 * KD + W]
    og = x[:, 2 * KD + W:2 * KD + 2 * W]
    xa = x[:, 2 * KD + 2 * W:2 * KD + 2 * W + LANE]
    z = _dot_hp(xa, aup_ref[...]) + ab_ref[...]
    la = -_softplus(-z) * (1.0 / GLA_TAU)
    tri = jnp.where(_tri_mask(C, fwd, False), 1.0, 0.0).astype(BF16)
    row_t = _iota((C, W), 0)
    lane_s = _iota((C, W), 1) % C
    m_incl = (lane_s <= row_t) if fwd else (lane_s >= row_t)
    bmask_t = ((_iota((W, KD), 0) // HEAD_DIM) == (_iota((W, KD), 1) // GLA_HEAD_K)).astype(F32)

    n_chunks = TB // C
    order = range(n_chunks) if fwd else range(n_chunks - 1, -1, -1)
    sls = [slice(c * C, (c + 1) * C) for c in order]
    bcs = [_dot_exact_lhs(tri, la[sl]) for sl in sls]
    bls = [bc[C - 1:C] if fwd else bc[0:1] for bc in bcs]
    qds = [q[sl] * jnp.exp(bc) for sl, bc in zip(sls, bcs)]
    kds = [_bf(k[sl] * jnp.exp(-bc)) for sl, bc in zip(sls, bcs)]
    khs = [k[sl] * jnp.exp(bl - bc) for sl, bc, bl in zip(sls, bcs, bls)]
    vbs = [_bf(v[sl]) for sl in sls]
    scs = [jnp.where(m_incl, _dot_nt(qd, _block_diag_bf(kd)), 0.0) for qd, kd in zip(qds, kds)]
    yis = [_dot(sc, _block_diag_bf(vb)) for sc, vb in zip(scs, vbs)]
    upds = [_dot_tn(vb, kh) * bmask_t for vb, kh in zip(vbs, khs)]
    st = st_ref[...]
    ys = []
    for j, sl in enumerate(sls):
        ys.append(yis[j] + _dot_nt(qds[j], st))
        st = st * jnp.exp(bls[j]) + upds[j]
    st_ref[...] = st
    y = jnp.concatenate(ys if fwd else ys[::-1], axis=0)

    if fwd:
        yt = y + yo_ref[...]
        var = _head_stats(yt * yt, avg_ref[...])
        o_ref[...] = (yt * lax.rsqrt(var + NORM_EPS) * gn_ref[...] * (og * _sigmoid(og))).astype(o_ref.dtype)
    else:
        o_ref[...] = y


def _gla(proj3, alpha_up, alpha_b, gn_w, avg, tl):
    b, t, _ = proj3.shape
    tb = tl["tb_gla"]
    nb = t // tb

    def call(fwd, d, extra):
        def bm(i):
            return i if fwd else nb - 1 - i

        aup = jnp.zeros((LANE, GLA_KDIM), F32).at[:alpha_up.shape[1]].set(alpha_up[d])
        in_specs = [
            pl.BlockSpec((None, tb, 1024), lambda bi, i: (bi, bm(i), COL_GLA // 1024)),
            pl.BlockSpec((LANE, GLA_KDIM), lambda bi, i: (0, 0)),
            pl.BlockSpec((1, GLA_KDIM), lambda bi, i: (0, 0)),
            pl.BlockSpec((1, GROUP_W), lambda bi, i: (0, 0)),
            pl.BlockSpec((GROUP_W, GROUP_W), lambda bi, i: (0, 0)),
        ]
        args = [proj3, aup, alpha_b[d][None, :], gn_w[None, :], avg]
        if fwd:
            in_specs.append(pl.BlockSpec((None, tb, GROUP_W), lambda bi, i: (bi, bm(i), 0)))
            args.append(extra)
        return pl.pallas_call(
            functools.partial(_gla_kernel, TB=tb, fwd=fwd),
            grid=(b, nb),
            in_specs=in_specs,
            out_specs=pl.BlockSpec((None, tb, GROUP_W), lambda bi, i: (bi, bm(i), 0)),
            out_shape=jax.ShapeDtypeStruct((b, t, GROUP_W), BF16 if fwd else F32),
            scratch_shapes=[pltpu.VMEM((GROUP_W, GLA_KDIM), F32)],
            compiler_params=_params("parallel", "arbitrary"),
            name="gla_fwd" if fwd else "gla_bwd",
        )(*args)

    y_b = call(False, 1, None)
    return call(True, 0, y_b)


def _lru_kernel(x_ref, hp_ref, hn_ref, cw_ref, cb_ref, gw_ref, gb_ref, lam_ref, *rest, TB, NB, fwd):
    if fwd:
        ho_ref, o_ref, h_ref = rest
    else:
        o_ref, h_ref = rest
    W = GROUP_W
    i = pl.program_id(1)
    blk = i if fwd else NB - 1 - i

    @pl.when(i == 0)
    def _init():
        h_ref[...] = jnp.zeros_like(h_ref)

    x = x_ref[...]
    xr, gate = x[:, 0:W], x[:, W:2 * W]
    has_prev = (blk > 0).astype(F32)
    has_next = (blk < NB - 1).astype(F32)
    prev = hp_ref[...][:, 0:W] * has_prev
    nxt = hn_ref[...][:, 0:W] * has_next
    row = _iota((TB, 1), 0)
    x_m1 = jnp.where(row == 0, prev[7:8], pltpu.roll(xr, 1, 0))
    x_m2 = jnp.where(row == 0, prev[6:7], jnp.where(row == 1, prev[7:8], pltpu.roll(xr, 2, 0)))
    x_p1 = jnp.where(row == TB - 1, nxt[0:1], pltpu.roll(xr, TB - 1, 0))
    cw = cw_ref[...]
    xc = x_m2 * cw[0:1] + x_m1 * cw[1:2] + xr * cw[2:3] + x_p1 * cw[3:4] + cb_ref[...]

    gx = _dot(xc, gw_ref[...]) + gb_ref[...]
    rec = _sigmoid(gx[:, 0:W])
    ing = _sigmoid(gx[:, W:2 * W])
    log_a = -RGLRU_C * rec * _softplus(-lam_ref[...])
    a = jnp.exp(log_a)
    bx = jnp.sqrt(-jnp.tanh(log_a) * (a * a + 1.0)) * (ing * xc)

    s = 1
    while s < TB:
        if s % SUBLANE == 0:
            ones, zeros = jnp.ones((s, W), F32), jnp.zeros((s, W), F32)
            if fwd:
                a_sh = jnp.concatenate([ones, a[0:TB - s]], axis=0)
                b_sh = jnp.concatenate([zeros, bx[0:TB - s]], axis=0)
            else:
                a_sh = jnp.concatenate([a[s:TB], ones], axis=0)
                b_sh = jnp.concatenate([bx[s:TB], zeros], axis=0)
        elif fwd:
            keep = row >= s
            a_sh = jnp.where(keep, pltpu.roll(a, s, 0), 1.0)
            b_sh = jnp.where(keep, pltpu.roll(bx, s, 0), 0.0)
        else:
            keep = row < TB - s
            a_sh = jnp.where(keep, pltpu.roll(a, TB - s, 0), 1.0)
            b_sh = jnp.where(keep, pltpu.roll(bx, TB - s, 0), 0.0)
        bx = a * b_sh + bx
        a = a * a_sh
        s *= 2
    h = a * h_ref[...] + bx
    h_ref[...] = h[TB - 1:TB] if fwd else h[0:1]

    if fwd:
        o_ref[...] = ((h + ho_ref[...]) * jax.nn.gelu(gate, approximate=True)).astype(o_ref.dtype)
    else:
        o_ref[...] = h


def _block_diag_heads(w):
    h, d, e = w.shape
    eye = jnp.eye(h, dtype=w.dtype)
    return jnp.einsum('hde,hg->hdge', w, eye).reshape(h * d, h * e)


def _rglru(proj3, conv_w, conv_b, gate_w, gate_b, lam, tl):
    b, t, _ = proj3.shape
    tb = tl["tb_lru"]
    nb = t // tb
    r8 = tb // SUBLANE
    last8 = t // SUBLANE - 1
    colb = COL_LRU // 512

    def call(fwd, d, extra):
        def bm(i):
            return i if fwd else nb - 1 - i

        gw = jnp.concatenate([_block_diag_heads(gate_w[d, 0]), _block_diag_heads(gate_w[d, 1])], axis=1)
        gb = jnp.concatenate([gate_b[d, 0], gate_b[d, 1]])[None, :]
        in_specs = [
            pl.BlockSpec((None, tb, 512), lambda bi, i: (bi, bm(i), colb)),
            pl.BlockSpec((None, SUBLANE, 512), lambda bi, i: (bi, jnp.maximum(bm(i) * r8 - 1, 0), colb)),
            pl.BlockSpec((None, SUBLANE, 512), lambda bi, i: (bi, jnp.minimum((bm(i) + 1) * r8, last8), colb)),
            pl.BlockSpec((4, GROUP_W), lambda bi, i: (0, 0)),
            pl.BlockSpec((1, GROUP_W), lambda bi, i: (0, 0)),
            pl.BlockSpec((GROUP_W, 2 * GROUP_W), lambda bi, i: (0, 0)),
            pl.BlockSpec((1, 2 * GROUP_W), lambda bi, i: (0, 0)),
            pl.BlockSpec((1, GROUP_W), lambda bi, i: (0, 0)),
        ]
        args = [proj3, proj3, proj3, conv_w, conv_b[None, :], _bf(gw), gb, lam[d][None, :]]
        if fwd:
            in_specs.append(pl.BlockSpec((None, tb, GROUP_W), lambda bi, i: (bi, bm(i), 0)))
            args.append(extra)
        return pl.pallas_call(
            functools.partial(_lru_kernel, TB=tb, NB=nb, fwd=fwd),
            grid=(b, nb),
            in_specs=in_specs,
            out_specs=pl.BlockSpec((None, tb, GROUP_W), lambda bi, i: (bi, bm(i), 0)),
            out_shape=jax.ShapeDtypeStruct((b, t, GROUP_W), BF16 if fwd else F32),
            scratch_shapes=[pltpu.VMEM((1, GROUP_W), F32)],
            compiler_params=_params("parallel", "arbitrary"),
            name="rglru_fwd" if fwd else "rglru_bwd",
        )(*args)

    h_b = call(False, 1, None)
    return call(True, 0, h_b)


def _rwkv_kernel(x_ref, halo_ref, mu_ref, lora_ref, w0_ref, a0_ref, kk_ref, ka_ref, rk_ref, ones_ref,
                 *rest, TB, NB, fwd, GROUP):
    if fwd:
        gup_ref, gn_ref, avg_ref, yo_ref, bo_ref, o_ref, s_ref = rest
    else:
        y_ref, b_ref, s_ref = rest
    W, C = GROUP_W, CHUNK
    i = pl.program_id(1)
    blk = i if fwd else NB - 1 - i

    @pl.when(i == 0)
    def _init():
        s_ref[...] = jnp.zeros_like(s_ref)

    x = x_ref[...]
    NS = 3 * W + LANE
    zs = x[:, 0:NS]
    halo = halo_ref[...][:, 0:NS]
    if fwd:
        edge = halo[SUBLANE - 1:SUBLANE] * (blk > 0).astype(F32)
    else:
        edge = halo[0:1] * (blk < NB - 1).astype(F32)
    zs = zs + mu_ref[...] * (_shift_rows(zs, edge, fwd) - zs)
    r, k, v, xwa = zs[:, 0:W], zs[:, W:2 * W], zs[:, 2 * W:3 * W], zs[:, 3 * W:NS]
    lane_l = _iota((1, LANE), 1)
    xwa = jnp.where(lane_l < LANE // 2, jnp.tanh(xwa), xwa)
    lo = _dot(xwa, lora_ref[...])
    w_log = -_softplus(-(w0_ref[...] + lo[:, 0:W])) - 0.5
    lw = -jnp.exp(w_log)
    a = _sigmoid(a0_ref[...] + lo[:, W:2 * W])
    ones_bd = ones_ref[...]
    kk = k * kk_ref[...]
    kk = kk / jnp.maximum(jnp.sqrt(_dot_exact_rhs(kk * kk, ones_bd)), 1e-12)
    k = k * (1.0 + (a - 1.0) * ka_ref[...])
    bonus = _dot_exact_rhs(r * k * rk_ref[...], ones_bd) * v
    am = -kk
    bm = kk * a

    tri = jnp.where(_tri_mask(C, fwd, False), 1.0, 0.0).astype(BF16)
    row_t = _iota((C, W), 0)
    lane_s = _iota((C, W), 1) % C
    m_strict = (lane_s < row_t) if fwd else (lane_s > row_t)
    m_incl = (lane_s <= row_t) if fwd else (lane_s >= row_t)
    eye_c = (lane_s == row_t).astype(F32)
    bmask = (_iota((W, W), 0) // HEAD_DIM) == (_iota((W, W), 1) // HEAD_DIM)

    def bd(z):
        return _block_diag_bf(_bf(z))

    n_chunks = TB // C
    order = range(n_chunks) if fwd else range(n_chunks - 1, -1, -1)
    def state_free_part(sls):
        cums = [_dot_exact_lhs(tri, lw[sl]) for sl in sls]
        tots = [cum[C - 1:C] if fwd else cum[0:1] for cum in cums]
        e_negs = [jnp.exp(-cum) for cum in cums]
        e_ends = [jnp.exp(tot - cum) for tot, cum in zip(tots, cums)]
        a_ts = [am[sl] * jnp.exp(cum - lw[sl]) for sl, cum in zip(sls, cums)]
        r_ts = [r[sl] * jnp.exp(cum) for sl, cum in zip(sls, cums)]
        yield None
        gs = [_dot_nt(jnp.concatenate([a_t, r_t], axis=0),
                      jnp.concatenate([bd(bm[sl] * e_neg), bd(k[sl] * e_neg)], axis=0))
              for sl, a_t, r_t, e_neg in zip(sls, a_ts, r_ts, e_negs)]
        yield None
        a_abs = [jnp.where(m_strict, g[0:C, 0:W], 0.0) for g in gs]
        a_aks = [jnp.where(m_strict, g[0:C, W:2 * W], 0.0) for g in gs]
        a_rbs = [jnp.where(m_incl, g[C:2 * C, 0:W], 0.0) for g in gs]
        a_rks = [jnp.where(m_incl, g[C:2 * C, W:2 * W], 0.0) for g in gs]

        tinvs = [eye_c + a_ab for a_ab in a_abs]
        ps = [_dot(a_ab, bd(a_ab)) for a_ab in a_abs]
        avs = [_dot(jnp.concatenate([a_ak, a_rk], axis=0), bd(v[sl]))
               for a_ak, a_rk, sl in zip(a_aks, a_rks, sls)]
        yield None
        n_terms = 2
        while 2 * n_terms < C:
            xps = [_dot(jnp.concatenate([tinv, p], axis=0), bd(p)) for p, tinv in zip(ps, tinvs)]
            tinvs = [tinv + xp[0:C] for tinv, xp in zip(tinvs, xps)]
            ps = [xp[C:2 * C] for xp in xps]
            n_terms *= 2
            yield None
        tinvs = [tinv + _dot(tinv, bd(p)) for p, tinv in zip(ps, tinvs)]
        yield None
        tas = [jnp.concatenate([tinv, _dot(a_rb, bd(tinv))], axis=0) for tinv, a_rb in zip(tinvs, a_rbs)]
        yield None
        wqs = [_dot(ta, bd(a_t)) for ta, a_t in zip(tas, a_ts)]
        wys = [_dot(ta, bd(av[0:C])) for ta, av in zip(tas, avs)]
        yield [dict(sl=sl, w1=wq[0:C], w2=wy[0:C], qp=r_t + wq[C:2 * C],
                    y0=av[C:2 * C] + wy[C:2 * C], decay=jnp.exp(tot),
                    kb=jnp.concatenate([k[sl] * e_end, bm[sl] * e_end], axis=0))
               for sl, wq, wy, r_t, av, tot, e_end in zip(sls, wqs, wys, r_ts, avs, tots, e_ends)]

    def recur(state, p_):
        sl = p_["sl"]
        uy = _dot_nt(jnp.concatenate([p_["w1"], p_["qp"]], axis=0), state)
        u = uy[0:C] + p_["w2"]
        y = uy[C:2 * C] + p_["y0"]
        upd = _dot_tn(jnp.concatenate([v[sl], u], axis=0), p_["kb"])
        ys.append(y)
        return state * p_["decay"] + jnp.where(bmask, upd, 0.0)

    ys = []

    sls_all = [slice(c * C, (c + 1) * C) for c in order]
    groups = GROUP if sum(GROUP) == n_chunks else (n_chunks,)
    state = s_ref[...]
    pending = []
    g0 = 0
    for group in groups:
        done = None
        for out in state_free_part(sls_all[g0:g0 + group]):
            if out is not None:
                done = out
            elif pending:
                state = recur(state, pending.pop(0))
        while pending:
            state = recur(state, pending.pop(0))
        pending = done
        g0 += group
    while pending:
        state = recur(state, pending.pop(0))
    s_ref[...] = state
    y = jnp.concatenate(ys if fwd else ys[::-1], axis=0)

    if fwd:
        yt = y + yo_ref[...]
        avg = avg_ref[...]
        yc = yt - _head_stats(yt, avg)
        var = _head_stats(yc * yc, avg)
        o = yc * lax.rsqrt(var + RWKV_GN_EPS) * gn_ref[...] + bonus + bo_ref[...]
        xg = x[:, NS:NS + LANE]
        o_ref[...] = (o * _dot(_sigmoid(xg), gup_ref[...])).astype(o_ref.dtype)
    else:
        y_ref[...] = y
        b_ref[...] = bonus


def _rwkv7(proj3, mu_rkv, mu_w, mu_a, w0, w_up, a0, a_up, g_up, k_k, k_a, r_k, gn_w, avg, ones_bd, tl):
    b, t, _ = proj3.shape
    tb = tl["tb_rwkv"]
    nb = t // tb
    r8 = tb // SUBLANE
    last8 = t // SUBLANE - 1
    colb = COL_RWKV // 1024
    W = GROUP_W
    mu = jnp.concatenate([mu_rkv.reshape(-1), mu_w, mu_a])[None, :]
    nl = w_up.shape[1]

    def call(fwd, d, extra):
        def bm(i):
            return i if fwd else nb - 1 - i

        lora = jnp.zeros((LANE, 2 * W), F32).at[0:nl, 0:W].set(w_up[d]).at[nl:2 * nl, W:2 * W].set(a_up[d])
        if fwd:
            halo_map = lambda bi, i: (bi, jnp.maximum(bm(i) * r8 - 1, 0), colb)
        else:
            halo_map = lambda bi, i: (bi, jnp.minimum((bm(i) + 1) * r8, last8), colb)
        row = lambda n: pl.BlockSpec((1, n), lambda bi, i: (0, 0))
        in_specs = [
            pl.BlockSpec((None, tb, 1024), lambda bi, i: (bi, bm(i), colb)),
            pl.BlockSpec((None, SUBLANE, 1024), halo_map),
            row(3 * W + LANE),
            pl.BlockSpec((LANE, 2 * W), lambda bi, i: (0, 0)),
            row(W), row(W), row(W), row(W), row(W),
            pl.BlockSpec((W, W), lambda bi, i: (0, 0)),
        ]
        args = [proj3, proj3, mu, lora, w0[d][None, :], a0[d][None, :], k_k[None, :], k_a[None, :],
                r_k.reshape(1, W), ones_bd]
        blk = pl.BlockSpec((None, tb, W), lambda bi, i: (bi, bm(i), 0))
        if fwd:
            in_specs += [pl.BlockSpec((LANE, W), lambda bi, i: (0, 0)), row(W),
                         pl.BlockSpec((W, W), lambda bi, i: (0, 0)), blk, blk]
            args += [g_up, gn_w[None, :], avg, extra[0], extra[1]]
            out_specs = blk
            out_shape = jax.ShapeDtypeStruct((b, t, W), BF16)
        else:
            out_specs = [blk, blk]
            out_shape = [jax.ShapeDtypeStruct((b, t, W), F32)] * 2
        return pl.pallas_call(
            functools.partial(_rwkv_kernel, TB=tb, NB=nb, fwd=fwd, GROUP=tl["rwkv_group"]),
            grid=(b, nb),
            in_specs=in_specs,
            out_specs=out_specs,
            out_shape=out_shape,
            scratch_shapes=[pltpu.VMEM((W, W), F32)],
            compiler_params=_params("parallel", "arbitrary"),
            name="rwkv7_fwd" if fwd else "rwkv7_bwd",
        )(*args)

    y_b, bonus_b = call(False, 1, None)
    return call(True, 0, (y_b, bonus_b))


def _out_proj_kernel(x_ref, o1_ref, o2_ref, o3_ref, o4_ref, w_ref, g_ref, rw_ref, rb_ref,
                     xo_ref, xn_ref, aff_ref, afft_ref):
    W = GROUP_W
    w = w_ref[...]
    acc = x_ref[...]
    for gi, o_ref in enumerate((o1_ref, o2_ref, o3_ref, o4_ref)):
        acc = acc + jnp.dot(_bf(o_ref[...]), w[gi * W:(gi + 1) * W], preferred_element_type=F32)
    xo_ref[...] = acc
    ms = jnp.mean(acc * acc, axis=-1, keepdims=True)
    xn = acc * lax.rsqrt(ms + NORM_EPS) * g_ref[...]
    xn_ref[...] = _bf(xn)
    raw = _dot_hp(xn, rw_ref[...]) + rb_ref[...]
    logits = raw - jnp.max(raw, axis=-1, keepdims=True)
    e = jnp.exp(logits)
    aff_ref[...] = e / jnp.sum(e, axis=-1, keepdims=True)
    lt = raw.T[0:afft_ref.shape[0]]
    lt = lt - jnp.max(lt, axis=0, keepdims=True)
    et = jnp.exp(lt)
    afft_ref[...] = et / jnp.sum(et, axis=0, keepdims=True)


def _out_proj(x2, outs, w_out_bf, gain, router_w, router_b, b, t, tl):
    n, d = x2.shape
    tm = tl["tm_out"]
    tpb = t // tm
    E = N_EXPERTS
    rw = jnp.zeros((d, LANE), F32).at[:, :E].set(router_w)
    rb = jnp.full((1, LANE), -1e30, F32).at[0, :E].set(router_b)
    full = lambda r, c: pl.BlockSpec((r, c), lambda i: (0, 0))
    tile = lambda c: pl.BlockSpec((tm, c), lambda i: (i, 0))
    return pl.pallas_call(
        _out_proj_kernel,
        grid=(n // tm,),
        in_specs=[tile(d)] + [tile(GROUP_W)] * 4 + [full(d, d), full(1, d), full(d, LANE), full(1, LANE)],
        out_specs=[tile(d), tile(d), tile(LANE),
                   pl.BlockSpec((None, E, tm), lambda i: (i // tpb, 0, i % tpb))],
        out_shape=[jax.ShapeDtypeStruct((n, d), F32), jax.ShapeDtypeStruct((n, d), BF16),
                   jax.ShapeDtypeStruct((n, LANE), F32), jax.ShapeDtypeStruct((b, E, t), F32)],
        compiler_params=_params("parallel"),
        name="out_proj_router",
    )(x2, *outs, w_out_bf, gain, rw, rb)


def _prefix_count(m, triu):
    e, t = m.shape
    nblk = t // LANE
    stacked = jnp.concatenate([m[:, j * LANE:(j + 1) * LANE] for j in range(nblk)], axis=0)
    incl = jnp.dot(_bf(stacked), triu, preferred_element_type=F32)
    pieces = []
    off = jnp.zeros((e, 1), F32)
    for j in range(nblk):
        blk = incl[j * e:(j + 1) * e]
        pieces.append(blk - stacked[j * e:(j + 1) * e] + off)
        off = off + blk[:, LANE - 1:LANE]
    return jnp.concatenate(pieces, axis=1)


def _select_kernel(afft_ref, rank_ref, rankc_ref, *, cap):
    aff = afft_ref[...]
    e, t = aff.shape
    bits = lax.bitcast_convert_type(aff, jnp.int32)

    def body(i, prefix):
        cand = prefix | jnp.left_shift(jnp.int32(1), 30 - i)
        cnt = jnp.sum((bits >= cand).astype(F32), axis=1, keepdims=True)
        return jnp.where(cnt >= cap, cand, prefix)

    thr = lax.fori_loop(0, 31, body, jnp.zeros((e, 1), jnp.int32))
    gt = (bits > thr).astype(F32)
    eq = (bits == thr).astype(F32)
    need = cap - jnp.sum(gt, axis=1, keepdims=True)
    triu = jnp.where(_tri_mask(LANE, False, False), 1.0, 0.0).astype(BF16)
    sel = gt + eq * (_prefix_count(eq, triu) < need).astype(F32)
    rank = jnp.where(sel > 0.5, _prefix_count(sel, triu), -1.0)
    rank_ref[...] = rank
    padded = jnp.concatenate([rank, jnp.full((LANE - e, t), -1.0, F32)], axis=0)
    rankc_ref[...] = padded.T


def _select(afft, cap):
    b, e, t = afft.shape
    return pl.pallas_call(
        functools.partial(_select_kernel, cap=cap),
        grid=(b,),
        in_specs=[pl.BlockSpec((None, e, t), lambda i: (i, 0, 0))],
        out_specs=[pl.BlockSpec((None, e, t), lambda i: (i, 0, 0)),
                   pl.BlockSpec((None, t, LANE), lambda i: (i, 0, 0))],
        out_shape=[jax.ShapeDtypeStruct((b, e, t), F32), jax.ShapeDtypeStruct((b, t, LANE), F32)],
        compiler_params=_params("parallel"),
        name="expert_choice_select",
    )(afft)


def _gather_kernel(rank_ref, xn_ref, xs_ref, acc_ref, *, cap, tk, group):
    g0 = pl.program_id(1) * group
    t = xn_ref.shape[0]
    slot = _iota((cap, tk), 0).astype(F32)
    for j in range(t // tk):
        onehot = jnp.concatenate(
            [jnp.where(rank_ref[pl.ds(g0 + g, 1), j * tk:(j + 1) * tk] == slot, 1.0, 0.0).astype(BF16)
             for g in range(group)], axis=0)
        part = jnp.dot(onehot, xn_ref[j * tk:(j + 1) * tk, :], preferred_element_type=F32)
        if j == 0:
            acc_ref[...] = part
        else:
            acc_ref[...] += part
    for g in range(group):
        xs_ref[g] = _bf(acc_ref[g * cap:(g + 1) * cap, :])


def _gather(rank, xn3, cap, tl):
    b, e, t = rank.shape
    d = xn3.shape[2]
    group = tl["gather_group"]
    return pl.pallas_call(
        functools.partial(_gather_kernel, cap=cap, tk=tl["tk_gather"], group=group),
        grid=(b, e // group),
        in_specs=[pl.BlockSpec((None, e, t), lambda bi, gi: (bi, 0, 0)),
                  pl.BlockSpec((None, t, d), lambda bi, gi: (bi, 0, 0))],
        out_specs=pl.BlockSpec((None, group, cap, d), lambda bi, gi: (bi, gi, 0, 0)),
        out_shape=jax.ShapeDtypeStruct((b, e, cap, d), BF16),
        scratch_shapes=[pltpu.VMEM((group * cap, d), F32)],
        compiler_params=_params("parallel", "arbitrary"),
        name="moe_gather",
    )(rank, xn3)


def _ffn_kernel(xs_ref, wg_ref, wu_ref, wd_ref, o_ref, wgb_ref, wub_ref, wdb_ref):
    @pl.when(pl.program_id(1) == 0)
    def _():
        wgb_ref[...] = _bf(wg_ref[...])
        wub_ref[...] = _bf(wu_ref[...])
        wdb_ref[...] = _bf(wd_ref[...])

    nseq = xs_ref.shape[0]
    xs = [xs_ref[i] for i in range(nseq)]
    hg = [jnp.dot(x, wgb_ref[...], preferred_element_type=F32) for x in xs]
    hu = [jnp.dot(x, wub_ref[...], preferred_element_type=F32) for x in xs]
    hid = [_bf(g * _sigmoid(g) * u) for g, u in zip(hg, hu)]
    for i in range(nseq):
        o_ref[i] = _bf(jnp.dot(hid[i], wdb_ref[...], preferred_element_type=F32))


def _ffn(xs, w_gate, w_up, w_down, layer, tl):
    b, e, cap, d = xs.shape
    f = w_gate.shape[3]
    nseq = min(tl["ffn_seqs"], b)
    wspec = lambda r, c: pl.BlockSpec((None, None, r, c), lambda ei, bi: (layer, ei, 0, 0))
    xspec = pl.BlockSpec((nseq, None, cap, d), lambda ei, bi: (bi, ei, 0, 0))
    return pl.pallas_call(
        _ffn_kernel,
        grid=(e, b // nseq),
        in_specs=[xspec, wspec(d, f), wspec(d, f), wspec(f, d)],
        out_specs=xspec,
        out_shape=jax.ShapeDtypeStruct((b, e, cap, d), BF16),
        scratch_shapes=[pltpu.VMEM((d, f), BF16), pltpu.VMEM((d, f), BF16), pltpu.VMEM((f, d), BF16)],
        compiler_params=_params("arbitrary", "arbitrary"),
        name="moe_ffn",
    )(xs, w_gate, w_up, w_down)


def _combine_kernel(x_ref, rankc_ref, aff_ref, o_ref, g_ref, out_ref, *, cap, final_norm):
    rankc = rankc_ref[...]
    aff = aff_ref[...]
    tm = x_ref.shape[0]
    slot = _iota((tm, cap), 1).astype(F32)
    acc = x_ref[...]
    for e in range(N_EXPERTS):
        pt = jnp.where(rankc[:, e:e + 1] == slot, aff[:, e:e + 1], 0.0)
        acc = acc + jnp.dot(_bf(pt), o_ref[e * cap:(e + 1) * cap, :], preferred_element_type=F32)
    if final_norm:
        ms = jnp.mean(acc * acc, axis=-1, keepdims=True)
        acc = acc * lax.rsqrt(ms + NORM_EPS) * g_ref[...]
    out_ref[...] = acc


def _combine(x3, rankc, aff3, o_flat, gain, cap, final_norm, tl):
    b, t, d = x3.shape
    tm = tl["tm_comb"]
    tile = lambda c: pl.BlockSpec((None, tm, c), lambda bi, i: (bi, i, 0))
    return pl.pallas_call(
        functools.partial(_combine_kernel, cap=cap, final_norm=final_norm),
        grid=(b, t // tm),
        in_specs=[tile(d), tile(LANE), tile(LANE),
                  pl.BlockSpec((None, N_EXPERTS * cap, d), lambda bi, i: (bi, 0, 0)),
                  pl.BlockSpec((1, d), lambda bi, i: (0, 0))],
        out_specs=tile(d),
        out_shape=jax.ShapeDtypeStruct((b, t, d), F32),
        compiler_params=pltpu.CompilerParams(dimension_semantics=("parallel", "arbitrary"),
                                             vmem_limit_bytes=VMEM_LIMIT_COMBINE),
        name="moe_combine",
    )(x3, rankc, aff3, o_flat, gain)


def kernel(x, positions, norm_mix, w_in, w_out, ret_log_decay, ret_gn, rwkv_mu_rkv, rwkv_mu_w, rwkv_mu_a, rwkv_w0, rwkv_w_up, rwkv_a0, rwkv_a_up, rwkv_g_up, rwkv_k_k, rwkv_k_a, rwkv_r_k, rwkv_gn, lru_conv_w, lru_conv_b, lru_gate_w, lru_gate_b, lru_lambda, gla_alpha_up, gla_alpha_b, gla_gn, norm_ffn, router_w, router_b, exp_w_gate, exp_w_up, exp_w_down, norm_final):
    b, t, d = x.shape
    depth = w_in.shape[0]
    n = b * t
    tl = _tiles(t)
    cap = EC_CAPACITY_FACTOR * t // N_EXPERTS
    lane = jnp.arange(GROUP_W)
    same_head = (lane[:, None] // HEAD_DIM) == (lane[None, :] // HEAD_DIM)
    avg = jnp.where(same_head, 1.0 / HEAD_DIM, 0.0).astype(BF16)
    ones_bd = jnp.where(same_head, 1.0, 0.0).astype(BF16)
    cos, sin = _rope_tables(positions)

    x2 = x.reshape(n, d)
    for l in range(depth):
        w = w_in[l]
        split = 2048 + 512
        w_perm = jnp.concatenate(
            [w[:, :2048], w[:, split:], jnp.zeros((d, IN_COLS_PAD - w.shape[1]), F32), w[:, 2048:split]], axis=1)
        proj3 = _in_proj(x2, norm_mix[l][None, :], _bf(w_perm), tl).reshape(b, t, IN_COLS_PAD)

        o_ret = _retention(proj3, cos, sin, ret_log_decay[l], ret_gn[l], avg, tl)
        o_rwkv = _rwkv7(proj3, rwkv_mu_rkv[l], rwkv_mu_w[l], rwkv_mu_a[l], rwkv_w0[l], rwkv_w_up[l],
                        rwkv_a0[l], rwkv_a_up[l], rwkv_g_up[l], rwkv_k_k[l], rwkv_k_a[l], rwkv_r_k[l],
                        rwkv_gn[l], avg, ones_bd, tl)
        o_lru = _rglru(proj3, lru_conv_w[l], lru_conv_b[l], lru_gate_w[l], lru_gate_b[l], lru_lambda[l], tl)
        o_gla = _gla(proj3, gla_alpha_up[l], gla_alpha_b[l], gla_gn[l], avg, tl)
        outs = [o.reshape(n, GROUP_W) for o in (o_ret, o_rwkv, o_lru, o_gla)]

        x_mid, xn, aff, afft = _out_proj(x2, outs, _bf(w_out[l]), norm_ffn[l][None, :], router_w[l],
                                         router_b[l], b, t, tl)
        rank, rankc = _select(afft, cap)
        xs = _gather(rank, xn.reshape(b, t, d), cap, tl)
        o_exp = _ffn(xs, exp_w_gate, exp_w_up, exp_w_down, l, tl)
        x3 = _combine(x_mid.reshape(b, t, d), rankc, aff.reshape(b, t, LANE),
                      o_exp.reshape(b, N_EXPERTS * cap, d), norm_final[None, :], cap, l == depth - 1, tl)
        x2 = x3.reshape(n, d)
    return x2.reshape(b, t, d)
```

```python
import functools

import jax
import jax.numpy as jnp
from jax import lax
from jax.experimental import pallas as pl
from jax.experimental.pallas import tpu as pltpu

F32 = jnp.float32
BF16 = jnp.bfloat16

D_MODEL = 1024
GROUP_W = 256
N_HEADS = 4
HEAD_DIM = 64
ROPE_BASE = 10000.0
RWKV_GN_EPS = 64e-5
RGLRU_C = 8.0
GLA_KDIM = 128
GLA_HEAD_K = 32
GLA_TAU = 16.0
N_EXPERTS = 16
EC_CAPACITY_FACTOR = 2
NORM_EPS = 1e-6

LANE = 128
SUBLANE = 8
CHUNK = 64
VMEM_LIMIT = 56 * 1024 * 1024
VMEM_LIMIT_COMBINE = 62 * 1024 * 1024

IN_COLS = 3344
SRC_RET, SRC_RWKV, SRC_LRU, SRC_GLA = 0, 1024, 2048, 2560
GLA_MAIN, GLA_XA = 768, 16
PROJ_H = 2048
H_RET, H_GLA = 0, 1024
PROJ_F = 1664
F_RWKV, F_LRU, F_GXA = 0, 1024, 1536


def _tiles(T):
    return dict(
        tm_proj=min(512, T), tn_proj=512,
        tb_ret=min(1024, T), tb_gla=min(1024, T), tb_lru=min(512, T), tb_rwkv=min(1024, T),
        tm_out=min(512, T), tm_comb=min(1024, T), tk_gather=min(1024, T), gather_group=4, ffn_seqs=2, rwkv_group=(8, 6, 2),
    )


def _params(*sem):
    return pltpu.CompilerParams(dimension_semantics=sem, vmem_limit_bytes=VMEM_LIMIT)


def _bf(x):
    return x.astype(BF16)


def _dot(a, b):
    return jnp.dot(_bf(a), _bf(b), preferred_element_type=F32)


def _dot_nt(a, b):
    return lax.dot_general(_bf(a), _bf(b), (((1,), (1,)), ((), ())), preferred_element_type=F32)


def _dot_tn(a, b):
    return lax.dot_general(_bf(a), _bf(b), (((0,), (0,)), ((), ())), preferred_element_type=F32)


def _split2(a):
    hi = _bf(a)
    lo = _bf(a - hi.astype(F32))
    return hi, lo


def _split3(a):
    hi = _bf(a)
    r = a - hi.astype(F32)
    mid = _bf(r)
    lo = _bf(r - mid.astype(F32))
    return hi, mid, lo


def _dot_exact_lhs(m, a):
    hi, mid, lo = _split3(a)
    return (jnp.dot(m, hi, preferred_element_type=F32) + jnp.dot(m, mid, preferred_element_type=F32)
            + jnp.dot(m, lo, preferred_element_type=F32))


def _dot_exact_rhs(a, m):
    hi, lo = _split2(a)
    return jnp.dot(hi, m, preferred_element_type=F32) + jnp.dot(lo, m, preferred_element_type=F32)


def _dot_hp(a, b):
    ah, al = _split2(a)
    bh, bl = _split2(b)
    return (jnp.dot(ah, bh, preferred_element_type=F32) + jnp.dot(ah, bl, preferred_element_type=F32)
            + jnp.dot(al, bh, preferred_element_type=F32))


def _sigmoid(x):
    return 1.0 / (1.0 + jnp.exp(-x))


def _softplus(x):
    return jnp.maximum(x, 0.0) + jnp.log(1.0 + jnp.exp(-jnp.abs(x)))


def _iota(shape, dim):
    return lax.broadcasted_iota(jnp.int32, shape, dim)


def _block_diag_bf(z):
    w = z.shape[1]
    lane_head = _iota((1, w), 1) // (w // N_HEADS)
    zero = jnp.zeros_like(z)
    return jnp.concatenate([jnp.where(lane_head == h, z, zero) for h in range(N_HEADS)], axis=0)


def _tri_mask(n, lower, strict):
    i, j = _iota((n, n), 0), _iota((n, n), 1)
    if lower:
        return (j < i) if strict else (j <= i)
    return (j > i) if strict else (j >= i)


def _head_stats(y, avg):
    return _dot_exact_rhs(y, avg)


def _shift_rows(z, edge, fwd):
    n = z.shape[0]
    row = _iota((n, 1), 0)
    if fwd:
        return jnp.where(row == 0, edge, pltpu.roll(z, 1, 0))
    return jnp.where(row == n - 1, edge, pltpu.roll(z, n - 1, 0))


def _in_proj_kernel(x_ref, g_ref, w_ref, oh_ref, of_ref, wb_ref, *, tn):
    @pl.when(pl.program_id(0) == 0)
    def _():
        zeros = lambda n_: jnp.zeros((wb_ref.shape[0], n_), BF16)
        wb_ref[:, H_RET:H_RET + 1024] = _bf(w_ref[:, SRC_RET:SRC_RET + 1024])
        wb_ref[:, H_GLA:H_GLA + GLA_MAIN] = _bf(w_ref[:, SRC_GLA:SRC_GLA + GLA_MAIN])
        wb_ref[:, H_GLA + GLA_MAIN:PROJ_H] = zeros(PROJ_H - H_GLA - GLA_MAIN)
        f0 = PROJ_H
        wb_ref[:, f0 + F_RWKV:f0 + F_RWKV + 1024] = _bf(w_ref[:, SRC_RWKV:SRC_RWKV + 1024])
        wb_ref[:, f0 + F_LRU:f0 + F_LRU + 512] = _bf(w_ref[:, SRC_LRU:SRC_LRU + 512])
        wb_ref[:, f0 + F_GXA:f0 + PROJ_F] = zeros(PROJ_F - F_GXA)
        wb_ref[:, f0 + F_GXA:f0 + F_GXA + GLA_XA] = _bf(w_ref[:, SRC_GLA + GLA_MAIN:SRC_GLA + GLA_MAIN + GLA_XA])

    x = x_ref[...]
    ms = jnp.mean(x * x, axis=-1, keepdims=True)
    xn = _bf(x * lax.rsqrt(ms + NORM_EPS) * g_ref[...])
    for c0 in range(0, PROJ_H, tn):
        oh_ref[:, c0:c0 + tn] = _bf(jnp.dot(xn, wb_ref[:, c0:c0 + tn], preferred_element_type=F32))
    for c0 in range(0, PROJ_F, tn):
        c1 = min(c0 + tn, PROJ_F)
        of_ref[:, c0:c1] = jnp.dot(xn, wb_ref[:, PROJ_H + c0:PROJ_H + c1], preferred_element_type=F32)


def _in_proj(x2, gain, w_in, layer, tl):
    n, d = x2.shape
    tm, tn = tl["tm_proj"], tl["tn_proj"]
    return pl.pallas_call(
        functools.partial(_in_proj_kernel, tn=tn),
        grid=(n // tm,),
        in_specs=[pl.BlockSpec((tm, d), lambda i: (i, 0)),
                  pl.BlockSpec((1, d), lambda i: (0, 0)),
                  pl.BlockSpec((None, d, IN_COLS), lambda i: (layer, 0, 0), pipeline_mode=pl.Buffered(1))],
        out_specs=[pl.BlockSpec((tm, PROJ_H), lambda i: (i, 0)), pl.BlockSpec((tm, PROJ_F), lambda i: (i, 0))],
        out_shape=[jax.ShapeDtypeStruct((n, PROJ_H), BF16), jax.ShapeDtypeStruct((n, PROJ_F), F32)],
        scratch_shapes=[pltpu.VMEM((d, PROJ_H + PROJ_F), BF16)],
        compiler_params=_params("arbitrary"),
        name="in_proj",
    )(x2, gain, w_in)


def _rope_kernel(pos_ref, inv_ref, sgn_ref, cos_ref, sin_ref):
    ang = pos_ref[...].astype(F32) * inv_ref[...]
    cos_ref[...] = jnp.cos(ang)
    sin_ref[...] = jnp.sin(ang) * sgn_ref[...]


def _rope_tables(positions):
    b, t = positions.shape
    n = b * t
    tm = min(1024, n)
    lane = jnp.arange(LANE)
    inv = jnp.power(ROPE_BASE, -jnp.arange(0, HEAD_DIM, 2, dtype=F32) / HEAD_DIM)
    inv_l = inv[lane % (HEAD_DIM // 2)][None, :]
    sgn = jnp.where(lane % HEAD_DIM < HEAD_DIM // 2, -1.0, 1.0).astype(F32)[None, :]
    cos, sin = pl.pallas_call(
        _rope_kernel,
        grid=(n // tm,),
        in_specs=[pl.BlockSpec((tm, 1), lambda i: (i, 0)),
                  pl.BlockSpec((1, LANE), lambda i: (0, 0)),
                  pl.BlockSpec((1, LANE), lambda i: (0, 0))],
        out_specs=[pl.BlockSpec((tm, LANE), lambda i: (i, 0))] * 2,
        out_shape=[jax.ShapeDtypeStruct((n, LANE), F32)] * 2,
        compiler_params=_params("parallel"),
        name="rope_tables",
    )(positions.reshape(n, 1), inv_l, sgn)
    return cos.reshape(b, t, LANE), sin.reshape(b, t, LANE)


def _ret_kernel(x_ref, cos_ref, sin_ref, lgl_ref, gn_ref, avg_ref, *rest, TB, fwd):
    if fwd:
        yo_ref, o_ref, s_ref = rest
    else:
        o_ref, s_ref = rest
    W, C = GROUP_W, CHUNK

    @pl.when(pl.program_id(1) == 0)
    def _init():
        s_ref[...] = jnp.zeros_like(s_ref)

    x = x_ref[...].astype(F32)
    q, k, v, g = x[:, 0:W], x[:, W:2 * W], x[:, 2 * W:3 * W], x[:, 3 * W:4 * W]
    cos = jnp.concatenate([cos_ref[...]] * (W // LANE), axis=1)
    sin = jnp.concatenate([sin_ref[...]] * (W // LANE), axis=1)
    first_half = (_iota((1, W), 1) % HEAD_DIM) < (HEAD_DIM // 2)

    def rot(z):
        swapped = jnp.where(first_half, pltpu.roll(z, W - HEAD_DIM // 2, 1), pltpu.roll(z, HEAD_DIM // 2, 1))
        return z * cos + swapped * sin

    q = rot(q)
    k = rot(k) * (HEAD_DIM ** -0.5)
    lgl = lgl_ref[...]
    row_t = _iota((C, W), 0)
    lane_s = _iota((C, W), 1) % C
    rel = ((row_t - lane_s) if fwd else (lane_s - row_t)).astype(F32)
    dmask = jnp.where(rel >= 0, jnp.exp(lgl * jnp.maximum(rel, 0.0)), 0.0)
    idx = _iota((C, 1), 0).astype(F32)
    if fwd:
        zeta = jnp.exp(lgl * (C - 1.0 - idx))
        xi = jnp.exp(lgl * (idx + 1.0))
    else:
        zeta = jnp.exp(lgl * idx)
        xi = jnp.exp(lgl * (C - idx))
    cd = jnp.exp(lgl * float(C))
    bmask = ((_iota((W, W), 0) // HEAD_DIM) == (_iota((W, W), 1) // HEAD_DIM)).astype(F32)

    n_chunks = TB // C
    order = range(n_chunks) if fwd else range(n_chunks - 1, -1, -1)
    sls = [slice(c * C, (c + 1) * C) for c in order]
    kbs = [_bf(k[sl]) for sl in sls]
    vbs = [_bf(v[sl]) for sl in sls]
    scs = [_dot_nt(q[sl], _block_diag_bf(kb)) * dmask for sl, kb in zip(sls, kbs)]
    yis = [_dot(sc, _block_diag_bf(vb)) for sc, vb in zip(scs, vbs)]
    upds = [_dot_tn(k[sl] * zeta, vb) * bmask for sl, vb in zip(sls, vbs)]
    state = s_ref[...]
    ys = []
    for j, sl in enumerate(sls):
        ys.append(yis[j] + _dot(q[sl] * xi, state))
        state = state * cd + upds[j]
    s_ref[...] = state
    y = jnp.concatenate(ys if fwd else ys[::-1], axis=0)

    if fwd:
        yt = y + yo_ref[...]
        avg = avg_ref[...]
        yc = yt - _head_stats(yt, avg)
        var = _head_stats(yc * yc, avg)
        o_ref[...] = (yc * lax.rsqrt(var + NORM_EPS) * gn_ref[...] * (g * _sigmoid(g))).astype(o_ref.dtype)
    else:
        o_ref[...] = y


def _retention(proj3, cos, sin, log_decay, gn_w, avg, tl):
    b, t, _ = proj3.shape
    tb = tl["tb_ret"]
    nb = t // tb

    def call(fwd, lg, extra):
        def cm(i):
            return i if fwd else nb - 1 - i

        in_specs = [
            pl.BlockSpec((None, tb, 1024), lambda bi, i: (bi, cm(i), H_RET // 1024)),
            pl.BlockSpec((None, tb, LANE), lambda bi, i: (bi, cm(i), 0)),
            pl.BlockSpec((None, tb, LANE), lambda bi, i: (bi, cm(i), 0)),
            pl.BlockSpec((1, GROUP_W), lambda bi, i: (0, 0)),
            pl.BlockSpec((1, GROUP_W), lambda bi, i: (0, 0)),
            pl.BlockSpec((GROUP_W, GROUP_W), lambda bi, i: (0, 0)),
        ]
        args = [proj3, cos, sin, jnp.repeat(lg, HEAD_DIM)[None, :], gn_w[None, :], avg]
        if fwd:
            in_specs.append(pl.BlockSpec((None, tb, GROUP_W), lambda bi, i: (bi, cm(i), 0)))
            args.append(extra)
        return pl.pallas_call(
            functools.partial(_ret_kernel, TB=tb, fwd=fwd),
            grid=(b, nb),
            in_specs=in_specs,
            out_specs=pl.BlockSpec((None, tb, GROUP_W), lambda bi, i: (bi, cm(i), 0)),
            out_shape=jax.ShapeDtypeStruct((b, t, GROUP_W), BF16 if fwd else F32),
            scratch_shapes=[pltpu.VMEM((GROUP_W, GROUP_W), F32)],
            compiler_params=_params("parallel", "arbitrary"),
            name="retention_fwd" if fwd else "retention_bwd",
        )(*args)

    y_b = call(False, log_decay[1], None)
    return call(True, log_decay[0], y_b)


def _gla_kernel(x_ref, xa_ref, aup_ref, ab_ref, gn_ref, avg_ref, *rest, TB, fwd):
    if fwd:
        yo_ref, o_ref, st_ref = rest
    else:
        o_ref, st_ref = rest
    W, KD, C = GROUP_W, GLA_KDIM, CHUNK

    @pl.when(pl.program_id(1) == 0)
    def _init():
        st_ref[...] = jnp.zeros_like(st_ref)

    x = x_ref[...].astype(F32)
    q = x[:, 0:KD] * (GLA_HEAD_K ** -0.5)
    k = x[:, KD:2 * KD]
    v = x[:, 2 * KD:2 * KD + W]
    og = x[:, 2 * KD + W:2 * KD + 2 * W]
    xa = xa_ref[...]
    z = _dot_hp(xa, aup_ref[...]) + ab_ref[...]
    la = -_softplus(-z) * (1.0 / GLA_TAU)
    tri = jnp.where(_tri_mask(C, fwd, False), 1.0, 0.0).astype(BF16)
    row_t = _iota((C, W), 0)
    lane_s = _iota((C, W), 1) % C
    m_incl = (lane_s <= row_t) if fwd else (lane_s >= row_t)
    bmask_t = ((_iota((W, KD), 0) // HEAD_DIM) == (_iota((W, KD), 1) // GLA_HEAD_K)).astype(F32)

    n_chunks = TB // C
    order = range(n_chunks) if fwd else range(n_chunks - 1, -1, -1)
    sls = [slice(c * C, (c + 1) * C) for c in order]
    bcs = [_dot_exact_lhs(tri, la[sl]) for sl in sls]
    bls = [bc[C - 1:C] if fwd else bc[0:1] for bc in bcs]
    qds = [q[sl] * jnp.exp(bc) for sl, bc in zip(sls, bcs)]
    kds = [_bf(k[sl] * jnp.exp(-bc)) for sl, bc in zip(sls, bcs)]
    khs = [k[sl] * jnp.exp(bl - bc) for sl, bc, bl in zip(sls, bcs, bls)]
    vbs = [_bf(v[sl]) for sl in sls]
    scs = [jnp.where(m_incl, _dot_nt(qd, _block_diag_bf(kd)), 0.0) for qd, kd in zip(qds, kds)]
    yis = [_dot(sc, _block_diag_bf(vb)) for sc, vb in zip(scs, vbs)]
    upds = [_dot_tn(vb, kh) * bmask_t for vb, kh in zip(vbs, khs)]
    st = st_ref[...]
    ys = []
    for j, sl in enumerate(sls):
        ys.append(yis[j] + _dot_nt(qds[j], st))
        st = st * jnp.exp(bls[j]) + upds[j]
    st_ref[...] = st
    y = jnp.concatenate(ys if fwd else ys[::-1], axis=0)

    if fwd:
        yt = y + yo_ref[...]
        var = _head_stats(yt * yt, avg_ref[...])
        o_ref[...] = (yt * lax.rsqrt(var + NORM_EPS) * gn_ref[...] * (og * _sigmoid(og))).astype(o_ref.dtype)
    else:
        o_ref[...] = y


def _gla(proj3, projf3, alpha_up, alpha_b, gn_w, avg, tl):
    b, t, _ = proj3.shape
    tb = tl["tb_gla"]
    nb = t // tb

    def call(fwd, d, extra):
        def bm(i):
            return i if fwd else nb - 1 - i

        aup = jnp.zeros((LANE, GLA_KDIM), F32).at[:alpha_up.shape[1]].set(alpha_up[d])
        in_specs = [
            pl.BlockSpec((None, tb, 1024), lambda bi, i: (bi, bm(i), H_GLA // 1024)),
            pl.BlockSpec((None, tb, LANE), lambda bi, i: (bi, bm(i), F_GXA // LANE)),
            pl.BlockSpec((LANE, GLA_KDIM), lambda bi, i: (0, 0)),
            pl.BlockSpec((1, GLA_KDIM), lambda bi, i: (0, 0)),
            pl.BlockSpec((1, GROUP_W), lambda bi, i: (0, 0)),
            pl.BlockSpec((GROUP_W, GROUP_W), lambda bi, i: (0, 0)),
        ]
        args = [proj3, projf3, aup, alpha_b[d][None, :], gn_w[None, :], avg]
        if fwd:
            in_specs.append(pl.BlockSpec((None, tb, GROUP_W), lambda bi, i: (bi, bm(i), 0)))
            args.append(extra)
        return pl.pallas_call(
            functools.partial(_gla_kernel, TB=tb, fwd=fwd),
            grid=(b, nb),
            in_specs=in_specs,
            out_specs=pl.BlockSpec((None, tb, GROUP_W), lambda bi, i: (bi, bm(i), 0)),
            out_shape=jax.ShapeDtypeStruct((b, t, GROUP_W), BF16 if fwd else F32),
            scratch_shapes=[pltpu.VMEM((GROUP_W, GLA_KDIM), F32)],
            compiler_params=_params("parallel", "arbitrary"),
            name="gla_fwd" if fwd else "gla_bwd",
        )(*args)

    y_b = call(False, 1, None)
    return call(True, 0, y_b)


def _lru_kernel(x_ref, hp_ref, hn_ref, cw_ref, cb_ref, gw_ref, gb_ref, lam_ref, *rest, TB, NB, fwd):
    if fwd:
        ho_ref, o_ref, h_ref = rest
    else:
        o_ref, h_ref = rest
    W = GROUP_W
    i = pl.program_id(1)
    blk = i if fwd else NB - 1 - i

    @pl.when(i == 0)
    def _init():
        h_ref[...] = jnp.zeros_like(h_ref)

    x = x_ref[...]
    xr, gate = x[:, 0:W], x[:, W:2 * W]
    has_prev = (blk > 0).astype(F32)
    has_next = (blk < NB - 1).astype(F32)
    prev = hp_ref[...][:, 0:W] * has_prev
    nxt = hn_ref[...][:, 0:W] * has_next
    row = _iota((TB, 1), 0)
    x_m1 = jnp.where(row == 0, prev[7:8], pltpu.roll(xr, 1, 0))
    x_m2 = jnp.where(row == 0, prev[6:7], jnp.where(row == 1, prev[7:8], pltpu.roll(xr, 2, 0)))
    x_p1 = jnp.where(row == TB - 1, nxt[0:1], pltpu.roll(xr, TB - 1, 0))
    cw = cw_ref[...]
    xc = x_m2 * cw[0:1] + x_m1 * cw[1:2] + xr * cw[2:3] + x_p1 * cw[3:4] + cb_ref[...]

    gx = _dot(xc, gw_ref[...]) + gb_ref[...]
    rec = _sigmoid(gx[:, 0:W])
    ing = _sigmoid(gx[:, W:2 * W])
    log_a = -RGLRU_C * rec * _softplus(-lam_ref[...])
    a = jnp.exp(log_a)
    bx = jnp.sqrt(-jnp.tanh(log_a) * (a * a + 1.0)) * (ing * xc)

    s = 1
    while s < TB:
        if s % SUBLANE == 0:
            ones, zeros = jnp.ones((s, W), F32), jnp.zeros((s, W), F32)
            if fwd:
                a_sh = jnp.concatenate([ones, a[0:TB - s]], axis=0)
                b_sh = jnp.concatenate([zeros, bx[0:TB - s]], axis=0)
            else:
                a_sh = jnp.concatenate([a[s:TB], ones], axis=0)
                b_sh = jnp.concatenate([bx[s:TB], zeros], axis=0)
        elif fwd:
            keep = row >= s
            a_sh = jnp.where(keep, pltpu.roll(a, s, 0), 1.0)
            b_sh = jnp.where(keep, pltpu.roll(bx, s, 0), 0.0)
        else:
            keep = row < TB - s
            a_sh = jnp.where(keep, pltpu.roll(a, TB - s, 0), 1.0)
            b_sh = jnp.where(keep, pltpu.roll(bx, TB - s, 0), 0.0)
        bx = a * b_sh + bx
        a = a * a_sh
        s *= 2
    h = a * h_ref[...] + bx
    h_ref[...] = h[TB - 1:TB] if fwd else h[0:1]

    if fwd:
        o_ref[...] = ((h + ho_ref[...]) * jax.nn.gelu(gate, approximate=True)).astype(o_ref.dtype)
    else:
        o_ref[...] = h


def _block_diag_heads(w):
    h, d, e = w.shape
    eye = jnp.eye(h, dtype=w.dtype)
    return jnp.einsum('hde,hg->hdge', w, eye).reshape(h * d, h * e)


def _rglru(proj3, conv_w, conv_b, gate_w, gate_b, lam, tl):
    b, t, _ = proj3.shape
    tb = tl["tb_lru"]
    nb = t // tb
    r8 = tb // SUBLANE
    last8 = t // SUBLANE - 1
    colb = F_LRU // 512

    def call(fwd, d, extra):
        def bm(i):
            return i if fwd else nb - 1 - i

        gw = jnp.concatenate([_block_diag_heads(gate_w[d, 0]), _block_diag_heads(gate_w[d, 1])], axis=1)
        gb = jnp.concatenate([gate_b[d, 0], gate_b[d, 1]])[None, :]
        in_specs = [
            pl.BlockSpec((None, tb, 512), lambda bi, i: (bi, bm(i), colb)),
            pl.BlockSpec((None, SUBLANE, 512), lambda bi, i: (bi, jnp.maximum(bm(i) * r8 - 1, 0), colb)),
            pl.BlockSpec((None, SUBLANE, 512), lambda bi, i: (bi, jnp.minimum((bm(i) + 1) * r8, last8), colb)),
            pl.BlockSpec((4, GROUP_W), lambda bi, i: (0, 0)),
            pl.BlockSpec((1, GROUP_W), lambda bi, i: (0, 0)),
            pl.BlockSpec((GROUP_W, 2 * GROUP_W), lambda bi, i: (0, 0)),
            pl.BlockSpec((1, 2 * GROUP_W), lambda bi, i: (0, 0)),
            pl.BlockSpec((1, GROUP_W), lambda bi, i: (0, 0)),
        ]
        args = [proj3, proj3, proj3, conv_w, conv_b[None, :], _bf(gw), gb, lam[d][None, :]]
        if fwd:
            in_specs.append(pl.BlockSpec((None, tb, GROUP_W), lambda bi, i: (bi, bm(i), 0)))
            args.append(extra)
        return pl.pallas_call(
            functools.partial(_lru_kernel, TB=tb, NB=nb, fwd=fwd),
            grid=(b, nb),
            in_specs=in_specs,
            out_specs=pl.BlockSpec((None, tb, GROUP_W), lambda bi, i: (bi, bm(i), 0)),
            out_shape=jax.ShapeDtypeStruct((b, t, GROUP_W), BF16 if fwd else F32),
            scratch_shapes=[pltpu.VMEM((1, GROUP_W), F32)],
            compiler_params=_params("parallel", "arbitrary"),
            name="rglru_fwd" if fwd else "rglru_bwd",
        )(*args)

    h_b = call(False, 1, None)
    return call(True, 0, h_b)


def _rwkv_kernel(x_ref, halo_ref, mu_ref, lora_ref, w0_ref, a0_ref, kk_ref, ka_ref, rk_ref, ones_ref,
                 *rest, TB, NB, fwd, GROUP):
    if fwd:
        gup_ref, gn_ref, avg_ref, yo_ref, bo_ref, o_ref, s_ref = rest
    else:
        y_ref, b_ref, s_ref = rest
    W, C = GROUP_W, CHUNK
    i = pl.program_id(1)
    blk = i if fwd else NB - 1 - i

    @pl.when(i == 0)
    def _init():
        s_ref[...] = jnp.zeros_like(s_ref)

    x = x_ref[...]
    NS = 3 * W + LANE
    zs = x[:, 0:NS]
    halo = halo_ref[...][:, 0:NS]
    if fwd:
        edge = halo[SUBLANE - 1:SUBLANE] * (blk > 0).astype(F32)
    else:
        edge = halo[0:1] * (blk < NB - 1).astype(F32)
    zs = zs + mu_ref[...] * (_shift_rows(zs, edge, fwd) - zs)
    r, k, v, xwa = zs[:, 0:W], zs[:, W:2 * W], zs[:, 2 * W:3 * W], zs[:, 3 * W:NS]
    lane_l = _iota((1, LANE), 1)
    xwa = jnp.where(lane_l < LANE // 2, jnp.tanh(xwa), xwa)
    lo = _dot(xwa, lora_ref[...])
    w_log = -_softplus(-(w0_ref[...] + lo[:, 0:W])) - 0.5
    lw = -jnp.exp(w_log)
    a = _sigmoid(a0_ref[...] + lo[:, W:2 * W])
    ones_bd = ones_ref[...]
    kk = k * kk_ref[...]
    kk = kk / jnp.maximum(jnp.sqrt(_dot_exact_rhs(kk * kk, ones_bd)), 1e-12)
    k = k * (1.0 + (a - 1.0) * ka_ref[...])
    bonus = _dot_exact_rhs(r * k * rk_ref[...], ones_bd) * v
    am = -kk
    bm = kk * a

    tri = jnp.where(_tri_mask(C, fwd, False), 1.0, 0.0).astype(BF16)
    row_t = _iota((C, W), 0)
    lane_s = _iota((C, W), 1) % C
    m_strict = (lane_s < row_t) if fwd else (lane_s > row_t)
    m_incl = (lane_s <= row_t) if fwd else (lane_s >= row_t)
    eye_c = (lane_s == row_t).astype(F32)
    bmask = (_iota((W, W), 0) // HEAD_DIM) == (_iota((W, W), 1) // HEAD_DIM)

    def bd(z):
        return _block_diag_bf(_bf(z))

    n_chunks = TB // C
    order = range(n_chunks) if fwd else range(n_chunks - 1, -1, -1)
    def state_free_part(sls):
        cums = [_dot_exact_lhs(tri, lw[sl]) for sl in sls]
        tots = [cum[C - 1:C] if fwd else cum[0:1] for cum in cums]
        e_negs = [jnp.exp(-cum) for cum in cums]
        e_ends = [jnp.exp(tot - cum) for tot, cum in zip(tots, cums)]
        a_ts = [am[sl] * jnp.exp(cum - lw[sl]) for sl, cum in zip(sls, cums)]
        r_ts = [r[sl] * jnp.exp(cum) for sl, cum in zip(sls, cums)]
        yield None
        gs = [_dot_nt(jnp.concatenate([a_t, r_t], axis=0),
                      jnp.concatenate([bd(bm[sl] * e_neg), bd(k[sl] * e_neg)], axis=0))
              for sl, a_t, r_t, e_neg in zip(sls, a_ts, r_ts, e_negs)]
        yield None
        a_abs = [jnp.where(m_strict, g[0:C, 0:W], 0.0) for g in gs]
        a_aks = [jnp.where(m_strict, g[0:C, W:2 * W], 0.0) for g in gs]
        a_rbs = [jnp.where(m_incl, g[C:2 * C, 0:W], 0.0) for g in gs]
        a_rks = [jnp.where(m_incl, g[C:2 * C, W:2 * W], 0.0) for g in gs]

        tinvs = [eye_c + a_ab for a_ab in a_abs]
        ps = [_dot(a_ab, bd(a_ab)) for a_ab in a_abs]
        avs = [_dot(jnp.concatenate([a_ak, a_rk], axis=0), bd(v[sl]))
               for a_ak, a_rk, sl in zip(a_aks, a_rks, sls)]
        yield None
        n_terms = 2
        while 2 * n_terms < C:
            xps = [_dot(jnp.concatenate([tinv, p], axis=0), bd(p)) for p, tinv in zip(ps, tinvs)]
            tinvs = [tinv + xp[0:C] for tinv, xp in zip(tinvs, xps)]
            ps = [xp[C:2 * C] for xp in xps]
            n_terms *= 2
            yield None
        tinvs = [tinv + _dot(tinv, bd(p)) for p, tinv in zip(ps, tinvs)]
        yield None
        tas = [jnp.concatenate([tinv, _dot(a_rb, bd(tinv))], axis=0) for tinv, a_rb in zip(tinvs, a_rbs)]
        yield None
        wqs = [_dot(ta, bd(a_t)) for ta, a_t in zip(tas, a_ts)]
        wys = [_dot(ta, bd(av[0:C])) for ta, av in zip(tas, avs)]
        yield [dict(sl=sl, w1=wq[0:C], w2=wy[0:C], qp=r_t + wq[C:2 * C],
                    y0=av[C:2 * C] + wy[C:2 * C], decay=jnp.exp(tot),
                    kb=jnp.concatenate([k[sl] * e_end, bm[sl] * e_end], axis=0))
               for sl, wq, wy, r_t, av, tot, e_end in zip(sls, wqs, wys, r_ts, avs, tots, e_ends)]

    def recur(state, p_):
        sl = p_["sl"]
        uy = _dot_nt(jnp.concatenate([p_["w1"], p_["qp"]], axis=0), state)
        u = uy[0:C] + p_["w2"]
        y = uy[C:2 * C] + p_["y0"]
        upd = _dot_tn(jnp.concatenate([v[sl], u], axis=0), p_["kb"])
        ys.append(y)
        return state * p_["decay"] + jnp.where(bmask, upd, 0.0)

    ys = []

    sls_all = [slice(c * C, (c + 1) * C) for c in order]
    groups = GROUP if sum(GROUP) == n_chunks else (n_chunks,)
    state = s_ref[...]
    pending = []
    g0 = 0
    for group in groups:
        done = None
        for out in state_free_part(sls_all[g0:g0 + group]):
            if out is not None:
                done = out
            elif pending:
                state = recur(state, pending.pop(0))
        while pending:
            state = recur(state, pending.pop(0))
        pending = done
        g0 += group
    while pending:
        state = recur(state, pending.pop(0))
    s_ref[...] = state
    y = jnp.concatenate(ys if fwd else ys[::-1], axis=0)

    if fwd:
        yt = y + yo_ref[...]
        avg = avg_ref[...]
        yc = yt - _head_stats(yt, avg)
        var = _head_stats(yc * yc, avg)
        o = yc * lax.rsqrt(var + RWKV_GN_EPS) * gn_ref[...] + bonus + bo_ref[...]
        xg = x[:, NS:NS + LANE]
        o_ref[...] = (o * _dot(_sigmoid(xg), gup_ref[...])).astype(o_ref.dtype)
    else:
        y_ref[...] = y
        b_ref[...] = bonus


def _rwkv7(proj3, mu_rkv, mu_w, mu_a, w0, w_up, a0, a_up, g_up, k_k, k_a, r_k, gn_w, avg, ones_bd, tl):
    b, t, _ = proj3.shape
    tb = tl["tb_rwkv"]
    nb = t // tb
    r8 = tb // SUBLANE
    last8 = t // SUBLANE - 1
    colb = F_RWKV // 1024
    W = GROUP_W
    mu = jnp.concatenate([mu_rkv.reshape(-1), mu_w, mu_a])[None, :]
    nl = w_up.shape[1]

    def call(fwd, d, extra):
        def bm(i):
            return i if fwd else nb - 1 - i

        lora = jnp.zeros((LANE, 2 * W), F32).at[0:nl, 0:W].set(w_up[d]).at[nl:2 * nl, W:2 * W].set(a_up[d])
        if fwd:
            halo_map = lambda bi, i: (bi, jnp.maximum(bm(i) * r8 - 1, 0), colb)
        else:
            halo_map = lambda bi, i: (bi, jnp.minimum((bm(i) + 1) * r8, last8), colb)
        row = lambda n: pl.BlockSpec((1, n), lambda bi, i: (0, 0))
        in_specs = [
            pl.BlockSpec((None, tb, 1024), lambda bi, i: (bi, bm(i), colb)),
            pl.BlockSpec((None, SUBLANE, 1024), halo_map),
            row(3 * W + LANE),
            pl.BlockSpec((LANE, 2 * W), lambda bi, i: (0, 0)),
            row(W), row(W), row(W), row(W), row(W),
            pl.BlockSpec((W, W), lambda bi, i: (0, 0)),
        ]
        args = [proj3, proj3, mu, lora, w0[d][None, :], a0[d][None, :], k_k[None, :], k_a[None, :],
                r_k.reshape(1, W), ones_bd]
        blk = pl.BlockSpec((None, tb, W), lambda bi, i: (bi, bm(i), 0))
        if fwd:
            in_specs += [pl.BlockSpec((LANE, W), lambda bi, i: (0, 0)), row(W),
                         pl.BlockSpec((W, W), lambda bi, i: (0, 0)), blk, blk]
            args += [g_up, gn_w[None, :], avg, extra[0], extra[1]]
            out_specs = blk
            out_shape = jax.ShapeDtypeStruct((b, t, W), BF16)
        else:
            out_specs = [blk, blk]
            out_shape = [jax.ShapeDtypeStruct((b, t, W), F32)] * 2
        return pl.pallas_call(
            functools.partial(_rwkv_kernel, TB=tb, NB=nb, fwd=fwd, GROUP=tl["rwkv_group"]),
            grid=(b, nb),
            in_specs=in_specs,
            out_specs=out_specs,
            out_shape=out_shape,
            scratch_shapes=[pltpu.VMEM((W, W), F32)],
            compiler_params=_params("parallel", "arbitrary"),
            name="rwkv7_fwd" if fwd else "rwkv7_bwd",
        )(*args)

    y_b, bonus_b = call(False, 1, None)
    return call(True, 0, (y_b, bonus_b))


def _out_proj_kernel(x_ref, o1_ref, o2_ref, o3_ref, o4_ref, w_ref, g_ref, rw_ref, rb_ref,
                     xo_ref, xn_ref, aff_ref, afft_ref):
    W = GROUP_W
    w = w_ref[...]
    acc = x_ref[...]
    for gi, o_ref in enumerate((o1_ref, o2_ref, o3_ref, o4_ref)):
        acc = acc + jnp.dot(_bf(o_ref[...]), w[gi * W:(gi + 1) * W], preferred_element_type=F32)
    xo_ref[...] = acc
    ms = jnp.mean(acc * acc, axis=-1, keepdims=True)
    xn = acc * lax.rsqrt(ms + NORM_EPS) * g_ref[...]
    xn_ref[...] = _bf(xn)
    raw = _dot_hp(xn, rw_ref[...]) + rb_ref[...]
    logits = raw - jnp.max(raw, axis=-1, keepdims=True)
    e = jnp.exp(logits)
    aff_ref[...] = e / jnp.sum(e, axis=-1, keepdims=True)
    lt = raw.T[0:afft_ref.shape[0]]
    lt = lt - jnp.max(lt, axis=0, keepdims=True)
    et = jnp.exp(lt)
    afft_ref[...] = et / jnp.sum(et, axis=0, keepdims=True)


def _out_proj(x2, outs, w_out_bf, gain, router_w, router_b, b, t, tl):
    n, d = x2.shape
    tm = tl["tm_out"]
    tpb = t // tm
    E = N_EXPERTS
    rw = jnp.zeros((d, LANE), F32).at[:, :E].set(router_w)
    rb = jnp.full((1, LANE), -1e30, F32).at[0, :E].set(router_b)
    full = lambda r, c: pl.BlockSpec((r, c), lambda i: (0, 0))
    tile = lambda c: pl.BlockSpec((tm, c), lambda i: (i, 0))
    return pl.pallas_call(
        _out_proj_kernel,
        grid=(n // tm,),
        in_specs=[tile(d)] + [tile(GROUP_W)] * 4 + [full(d, d), full(1, d), full(d, LANE), full(1, LANE)],
        out_specs=[tile(d), tile(d), tile(LANE),
                   pl.BlockSpec((None, E, tm), lambda i: (i // tpb, 0, i % tpb))],
        out_shape=[jax.ShapeDtypeStruct((n, d), F32), jax.ShapeDtypeStruct((n, d), BF16),
                   jax.ShapeDtypeStruct((n, LANE), F32), jax.ShapeDtypeStruct((b, E, t), F32)],
        compiler_params=_params("parallel"),
        name="out_proj_router",
    )(x2, *outs, w_out_bf, gain, rw, rb)


def _prefix_count(m, triu):
    e, t = m.shape
    nblk = t // LANE
    stacked = jnp.concatenate([m[:, j * LANE:(j + 1) * LANE] for j in range(nblk)], axis=0)
    incl = jnp.dot(_bf(stacked), triu, preferred_element_type=F32)
    pieces = []
    off = jnp.zeros((e, 1), F32)
    for j in range(nblk):
        blk = incl[j * e:(j + 1) * e]
        pieces.append(blk - stacked[j * e:(j + 1) * e] + off)
        off = off + blk[:, LANE - 1:LANE]
    return jnp.concatenate(pieces, axis=1)


def _select_kernel(afft_ref, rank_ref, rankc_ref, *, cap):
    aff = afft_ref[...]
    e, t = aff.shape
    bits = lax.bitcast_convert_type(aff, jnp.int32)

    def body(i, prefix):
        cand = prefix | jnp.left_shift(jnp.int32(1), 30 - i)
        cnt = jnp.sum((bits >= cand).astype(F32), axis=1, keepdims=True)
        return jnp.where(cnt >= cap, cand, prefix)

    thr = lax.fori_loop(0, 31, body, jnp.zeros((e, 1), jnp.int32))
    gt = (bits > thr).astype(F32)
    eq = (bits == thr).astype(F32)
    need = cap - jnp.sum(gt, axis=1, keepdims=True)
    triu = jnp.where(_tri_mask(LANE, False, False), 1.0, 0.0).astype(BF16)
    sel = gt + eq * (_prefix_count(eq, triu) < need).astype(F32)
    rank = jnp.where(sel > 0.5, _prefix_count(sel, triu), -1.0)
    rank_ref[...] = rank
    padded = jnp.concatenate([rank, jnp.full((LANE - e, t), -1.0, F32)], axis=0)
    rankc_ref[...] = padded.T


def _select(afft, cap):
    b, e, t = afft.shape
    return pl.pallas_call(
        functools.partial(_select_kernel, cap=cap),
        grid=(b,),
        in_specs=[pl.BlockSpec((None, e, t), lambda i: (i, 0, 0))],
        out_specs=[pl.BlockSpec((None, e, t), lambda i: (i, 0, 0)),
                   pl.BlockSpec((None, t, LANE), lambda i: (i, 0, 0))],
        out_shape=[jax.ShapeDtypeStruct((b, e, t), F32), jax.ShapeDtypeStruct((b, t, LANE), F32)],
        compiler_params=_params("parallel"),
        name="expert_choice_select",
    )(afft)


def _gather_kernel(rank_ref, xn_ref, xs_ref, acc_ref, *, cap, tk, group):
    g0 = pl.program_id(1) * group
    t = xn_ref.shape[0]
    slot = _iota((cap, tk), 0).astype(F32)
    for j in range(t // tk):
        onehot = jnp.concatenate(
            [jnp.where(rank_ref[pl.ds(g0 + g, 1), j * tk:(j + 1) * tk] == slot, 1.0, 0.0).astype(BF16)
             for g in range(group)], axis=0)
        part = jnp.dot(onehot, xn_ref[j * tk:(j + 1) * tk, :], preferred_element_type=F32)
        if j == 0:
            acc_ref[...] = part
        else:
            acc_ref[...] += part
    for g in range(group):
        xs_ref[g] = _bf(acc_ref[g * cap:(g + 1) * cap, :])


def _gather(rank, xn3, cap, tl):
    b, e, t = rank.shape
    d = xn3.shape[2]
    group = tl["gather_group"]
    return pl.pallas_call(
        functools.partial(_gather_kernel, cap=cap, tk=tl["tk_gather"], group=group),
        grid=(b, e // group),
        in_specs=[pl.BlockSpec((None, e, t), lambda bi, gi: (bi, 0, 0)),
                  pl.BlockSpec((None, t, d), lambda bi, gi: (bi, 0, 0))],
        out_specs=pl.BlockSpec((None, group, cap, d), lambda bi, gi: (bi, gi, 0, 0)),
        out_shape=jax.ShapeDtypeStruct((b, e, cap, d), BF16),
        scratch_shapes=[pltpu.VMEM((group * cap, d), F32)],
        compiler_params=_params("parallel", "arbitrary"),
        name="moe_gather",
    )(rank, xn3)


def _ffn_kernel(xs_ref, wg_ref, wu_ref, wd_ref, o_ref, wgb_ref, wub_ref, wdb_ref):
    @pl.when(pl.program_id(1) == 0)
    def _():
        wgb_ref[...] = _bf(wg_ref[...])
        wub_ref[...] = _bf(wu_ref[...])
        wdb_ref[...] = _bf(wd_ref[...])

    nseq, cap, d = xs_ref.shape
    xs = xs_ref[...].reshape(nseq * cap, d)
    hg = jnp.dot(xs, wgb_ref[...], preferred_element_type=F32)
    hu = jnp.dot(xs, wub_ref[...], preferred_element_type=F32)
    hid = _bf(hg * _sigmoid(hg) * hu)
    out = jnp.dot(hid, wdb_ref[...], preferred_element_type=F32)
    o_ref[...] = _bf(out).reshape(nseq, cap, d)


def _ffn(xs, w_gate, w_up, w_down, layer, tl):
    b, e, cap, d = xs.shape
    f = w_gate.shape[3]
    nseq = min(tl["ffn_seqs"], b)
    wspec = lambda r, c: pl.BlockSpec((None, None, r, c), lambda ei, bi: (layer, ei, 0, 0))
    xspec = pl.BlockSpec((nseq, None, cap, d), lambda ei, bi: (bi, ei, 0, 0))
    return pl.pallas_call(
        _ffn_kernel,
        grid=(e, b // nseq),
        in_specs=[xspec, wspec(d, f), wspec(d, f), wspec(f, d)],
        out_specs=xspec,
        out_shape=jax.ShapeDtypeStruct((b, e, cap, d), BF16),
        scratch_shapes=[pltpu.VMEM((d, f), BF16), pltpu.VMEM((d, f), BF16), pltpu.VMEM((f, d), BF16)],
        compiler_params=_params("arbitrary", "arbitrary"),
        name="moe_ffn",
    )(xs, w_gate, w_up, w_down)


def _combine_kernel(x_ref, rankc_ref, aff_ref, o_ref, g_ref, out_ref, *, cap, final_norm):
    rankc = rankc_ref[...]
    aff = aff_ref[...]
    tm = x_ref.shape[0]
    slot = _iota((tm, cap), 1).astype(F32)
    acc = x_ref[...]
    for e in range(N_EXPERTS):
        pt = jnp.where(rankc[:, e:e + 1] == slot, aff[:, e:e + 1], 0.0)
        acc = acc + jnp.dot(_bf(pt), o_ref[e * cap:(e + 1) * cap, :], preferred_element_type=F32)
    if final_norm:
        ms = jnp.mean(acc * acc, axis=-1, keepdims=True)
        acc = acc * lax.rsqrt(ms + NORM_EPS) * g_ref[...]
    out_ref[...] = acc


def _combine(x3, rankc, aff3, o_flat, gain, cap, final_norm, tl):
    b, t, d = x3.shape
    tm = tl["tm_comb"]
    tile = lambda c: pl.BlockSpec((None, tm, c), lambda bi, i: (bi, i, 0))
    return pl.pallas_call(
        functools.partial(_combine_kernel, cap=cap, final_norm=final_norm),
        grid=(b, t // tm),
        in_specs=[tile(d), tile(LANE), tile(LANE),
                  pl.BlockSpec((None, N_EXPERTS * cap, d), lambda bi, i: (bi, 0, 0)),
                  pl.BlockSpec((1, d), lambda bi, i: (0, 0))],
        out_specs=tile(d),
        out_shape=jax.ShapeDtypeStruct((b, t, d), F32),
        compiler_params=pltpu.CompilerParams(dimension_semantics=("parallel", "arbitrary"),
                                             vmem_limit_bytes=VMEM_LIMIT_COMBINE),
        name="moe_combine",
    )(x3, rankc, aff3, o_flat, gain)


def kernel(x, positions, norm_mix, w_in, w_out, ret_log_decay, ret_gn, rwkv_mu_rkv, rwkv_mu_w, rwkv_mu_a, rwkv_w0, rwkv_w_up, rwkv_a0, rwkv_a_up, rwkv_g_up, rwkv_k_k, rwkv_k_a, rwkv_r_k, rwkv_gn, lru_conv_w, lru_conv_b, lru_gate_w, lru_gate_b, lru_lambda, gla_alpha_up, gla_alpha_b, gla_gn, norm_ffn, router_w, router_b, exp_w_gate, exp_w_up, exp_w_down, norm_final):
    b, t, d = x.shape
    depth = w_in.shape[0]
    n = b * t
    tl = _tiles(t)
    cap = EC_CAPACITY_FACTOR * t // N_EXPERTS
    lane = jnp.arange(GROUP_W)
    same_head = (lane[:, None] // HEAD_DIM) == (lane[None, :] // HEAD_DIM)
    avg = jnp.where(same_head, 1.0 / HEAD_DIM, 0.0).astype(BF16)
    ones_bd = jnp.where(same_head, 1.0, 0.0).astype(BF16)
    cos, sin = _rope_tables(positions)

    x2 = x.reshape(n, d)
    for l in range(depth):
        proj_h, proj_f = _in_proj(x2, norm_mix[l][None, :], w_in, l, tl)
        proj_h, proj_f = proj_h.reshape(b, t, PROJ_H), proj_f.reshape(b, t, PROJ_F)

        o_ret = _retention(proj_h, cos, sin, ret_log_decay[l], ret_gn[l], avg, tl)
        o_rwkv = _rwkv7(proj_f, rwkv_mu_rkv[l], rwkv_mu_w[l], rwkv_mu_a[l], rwkv_w0[l], rwkv_w_up[l],
                        rwkv_a0[l], rwkv_a_up[l], rwkv_g_up[l], rwkv_k_k[l], rwkv_k_a[l], rwkv_r_k[l],
                        rwkv_gn[l], avg, ones_bd, tl)
        o_lru = _rglru(proj_f, lru_conv_w[l], lru_conv_b[l], lru_gate_w[l], lru_gate_b[l], lru_lambda[l], tl)
        o_gla = _gla(proj_h, proj_f, gla_alpha_up[l], gla_alpha_b[l], gla_gn[l], avg, tl)
        outs = [o.reshape(n, GROUP_W) for o in (o_ret, o_rwkv, o_lru, o_gla)]

        x_mid, xn, aff, afft = _out_proj(x2, outs, _bf(w_out[l]), norm_ffn[l][None, :], router_w[l],
                                         router_b[l], b, t, tl)
        rank, rankc = _select(afft, cap)
        xs = _gather(rank, xn.reshape(b, t, d), cap, tl)
        o_exp = _ffn(xs, exp_w_gate, exp_w_up, exp_w_down, l, tl)
        x3 = _combine(x_mid.reshape(b, t, d), rankc, aff.reshape(b, t, LANE),
                      o_exp.reshape(b, N_EXPERTS * cap, d), norm_final[None, :], cap, l == depth - 1, tl)
        x2 = x3.reshape(n, d)
    return x2.reshape(b, t, d)
```
